```python
import math
import jax, jax.numpy as jnp
from jax import lax
import numpy as np

D_MODEL = 2048
BATCH = 4
SEQ = 2048
DEPTH = 1
DEC_BATCH = 32
DEC_SEQ = 16
PAST_LEN = 2048

CHUNK = 64
N_HEADS = 8
D_QK = 64
D_V = 2 * D_QK
ATT_W = N_HEADS * D_V
CONV_W = D_MODEL // 2
CONV_K = 3
NUM_BUCKETS = 32
MAX_DISTANCE = 128
Q_BLOCK = 128
EPS = 1e-6
SPLIT_SIZES = (N_HEADS * 2 * D_QK,
               N_HEADS * 2 * D_QK,
               ATT_W,
               ATT_W,
               CONV_W,
               CONV_W,
               CONV_W,
               CONV_W,
               D_MODEL,
               D_MODEL)
N_IN = sum(SPLIT_SIZES)
SPLIT_IDX = tuple(int(i) for i in np.cumsum(SPLIT_SIZES)[:-1])

kernel_name = "hybrid_diffattn_shortconv_stream_step"


def rmsnorm(x, g):
    xf = x.astype(jnp.float32)
    xf = xf * lax.rsqrt(jnp.mean(xf * xf, axis=-1, keepdims=True) + EPS)
    return xf.astype(x.dtype) * g


def rel_bucket(rel):
    half = NUM_BUCKETS // 2
    max_exact = half // 2
    ret = (rel > 0).astype(jnp.int32) * half
    n = jnp.abs(rel)
    nf = jnp.maximum(n, 1).astype(jnp.float32)
    large = max_exact + (jnp.log(nf / max_exact) / math.log(MAX_DISTANCE / max_exact)
                         * (half - max_exact)).astype(jnp.int32)
    large = jnp.minimum(large, half - 1)
    return ret + jnp.where(n < max_exact, n, large)


def diff_attend(q, k, v, qpos, kpos, lam, rel_bias):
    bias = rel_bias[rel_bucket(kpos[None, :] - qpos[:, None])]
    bias = jnp.transpose(bias, (2, 0, 1)).astype(jnp.float32)
    mask = (kpos[None, :] // CHUNK) <= (qpos[:, None] // CHUNK)
    scale = 1.0 / math.sqrt(D_QK)

    def probs(qa, ka):
        s = jnp.einsum('bqhd,bkhd->bhqk', qa, ka).astype(jnp.float32) * scale + bias
        s = jnp.where(mask, s, jnp.finfo(jnp.float32).min)
        return jax.nn.softmax(s, axis=-1)

    a = probs(q[..., :D_QK], k[..., :D_QK]) - lam * probs(q[..., D_QK:], k[..., D_QK:])
    return jnp.einsum('bhqk,bkhe->bqhe', a.astype(v.dtype), v)


def attend_prompt(q, k, v, lam, rel_bias):
    b, t = q.shape[0], q.shape[1]
    nb = t // Q_BLOCK
    qb = q.reshape(b, nb, Q_BLOCK, N_HEADS, 2 * D_QK).transpose(1, 0, 2, 3, 4)
    starts = jnp.arange(nb, dtype=jnp.int32) * Q_BLOCK
    kpos = jnp.arange(t, dtype=jnp.int32)

    def blk(args):
        qi, s0 = args
        qpos = s0 + jnp.arange(Q_BLOCK, dtype=jnp.int32)
        return diff_attend(qi, k, v, qpos, kpos, lam, rel_bias)

    o = lax.map(blk, (qb, starts))
    return o.transpose(1, 0, 2, 3, 4).reshape(b, t, N_HEADS, D_V)


def short_conv(u, state, w):
    t = u.shape[1]
    p = jnp.concatenate([state, u], axis=1)
    y = w[0] * p[:, 0:t] + w[1] * p[:, 1:t + 1] + w[2] * p[:, 2:t + 2]
    return y, p[:, -(CONV_K - 1):]


def mixer_layer(x, past_k, past_v, conv_state, is_prompt, lam, lam_init,
                norm_pre, norm_post, w_in, head_norm, conv_w,
                w_proj_attn, w_proj_conv, w_out, rel_bias):
    b, t, _ = x.shape
    xn = rmsnorm(x, norm_pre)
    p = jnp.einsum('btd,dn->btn', xn, w_in)
    q, k, v, za, bg, cg, h, zc, ga, gc = jnp.split(p, SPLIT_IDX, axis=-1)
    q = q.reshape(b, t, N_HEADS, 2 * D_QK)
    k = k.reshape(b, t, N_HEADS, 2 * D_QK)
    v = v.reshape(b, t, N_HEADS, D_V)
    if is_prompt:
        o = attend_prompt(q, k, v, lam, rel_bias)
    else:
        k_all = jnp.concatenate([past_k, k], axis=1)
        v_all = jnp.concatenate([past_v, v], axis=1)
        past = past_k.shape[1]
        qpos = past + jnp.arange(t, dtype=jnp.int32)
        kpos = jnp.arange(past + t, dtype=jnp.int32)
        o = diff_attend(q, k_all, v_all, qpos, kpos, lam, rel_bias)
    o = rmsnorm(o, head_norm) * (1.0 - lam_init)
    ya = jnp.einsum('bte,ed->btd', o.reshape(b, t, ATT_W) * jax.nn.silu(za), w_proj_attn)
    yc, new_conv = short_conv(cg * h, conv_state, conv_w)
    yc = jnp.einsum('btc,cd->btd', bg * yc * jax.nn.silu(zc), w_proj_conv)
    y = jnp.einsum('btd,de->bte', jax.nn.sigmoid(ga) * ya + jax.nn.sigmoid(gc) * yc, w_out)
    return x + rmsnorm(y, norm_post), k, v, new_conv


def setup_inputs(seed: int = 0) -> dict:
    key = jax.random.key(seed)
    ks = jax.random.split(key, 20)
    f32 = jnp.float32
    nrm = lambda k, s, sc: jax.random.normal(k, s, f32) * sc
    return {
        "x_prompt": nrm(ks[0], (BATCH, SEQ, D_MODEL), 1.0),
        "x_sample": nrm(ks[1], (DEC_BATCH, DEC_SEQ, D_MODEL), 1.0),
        "cache_k": nrm(ks[2], (DEPTH, DEC_BATCH, PAST_LEN, N_HEADS, 2 * D_QK), 1.0),
        "cache_v": nrm(ks[3], (DEPTH, DEC_BATCH, PAST_LEN, N_HEADS, D_V), 1.0),
        "state_conv": nrm(ks[4], (DEPTH, DEC_BATCH, CONV_K - 1, CONV_W), 1.0),
        "norm_pre": 1.0 + nrm(ks[5], (DEPTH, D_MODEL), 0.02),
        "norm_post": 1.0 + nrm(ks[6], (DEPTH, D_MODEL), 0.02),
        "w_in": nrm(ks[7], (DEPTH, D_MODEL, N_IN), D_MODEL ** -0.5),
        "lambda_q1": nrm(ks[8], (DEPTH, D_QK), 0.1),
        "lambda_k1": nrm(ks[9], (DEPTH, D_QK), 0.1),
        "lambda_q2": nrm(ks[10], (DEPTH, D_QK), 0.1),
        "lambda_k2": nrm(ks[11], (DEPTH, D_QK), 0.1),
        "head_norm": 1.0 + nrm(ks[12], (DEPTH, D_V), 0.02),
        "conv_w": nrm(ks[13], (DEPTH, CONV_K, CONV_W), CONV_K ** -0.5),
        "w_proj_attn": nrm(ks[14], (DEPTH, ATT_W, D_MODEL), ATT_W ** -0.5),
        "w_proj_conv": nrm(ks[15], (DEPTH, CONV_W, D_MODEL), CONV_W ** -0.5),
        "w_out": nrm(ks[16], (DEPTH, D_MODEL, D_MODEL), D_MODEL ** -0.5),
        "rel_bias": nrm(ks[17], (NUM_BUCKETS, N_HEADS), 0.5),
    }


def reference(x_prompt, x_sample, cache_k, cache_v, state_conv, norm_pre, norm_post,
              w_in, lambda_q1, lambda_k1, lambda_q2, lambda_k2, head_norm, conv_w,
              w_proj_attn, w_proj_conv, w_out, rel_bias):
    xp, xs = x_prompt, x_sample
    kp_l, vp_l, cp_l, ks_l, vs_l, cs_l = [], [], [], [], [], []
    zero_conv = jnp.zeros((xp.shape[0], CONV_K - 1, CONV_W), xp.dtype)
    for l in range(DEPTH):
        lam_init = 0.8 - 0.6 * math.exp(-0.3 * l)
        lam = (jnp.exp(jnp.sum(lambda_q1[l].astype(jnp.float32) * lambda_k1[l].astype(jnp.float32)))
               - jnp.exp(jnp.sum(lambda_q2[l].astype(jnp.float32) * lambda_k2[l].astype(jnp.float32)))
               + lam_init)
        weights = (norm_pre[l], norm_post[l], w_in[l], head_norm[l], conv_w[l],
                   w_proj_attn[l], w_proj_conv[l], w_out[l], rel_bias)
        xp, kp, vp, cp = mixer_layer(xp, None, None, zero_conv, True, lam, lam_init, *weights)
        xs, ksm, vsm, csm = mixer_layer(xs, cache_k[l], cache_v[l], state_conv[l], False,
                                        lam, lam_init, *weights)
        kp_l.append(kp); vp_l.append(vp); cp_l.append(cp)
        ks_l.append(ksm); vs_l.append(vsm); cs_l.append(csm)
    new_k_prompt = jnp.stack(kp_l, axis=0)
    new_v_prompt = jnp.stack(vp_l, axis=0)
    new_conv_prompt = jnp.stack(cp_l, axis=0)
    new_k_sample = jnp.stack(ks_l, axis=0)
    new_v_sample = jnp.stack(vs_l, axis=0)
    new_conv_sample = jnp.stack(cs_l, axis=0)
    return (xp, xs, new_k_prompt, new_v_prompt, new_conv_prompt,
            new_k_sample, new_v_sample, new_conv_sample)
```

```python
import functools
import math

import jax
import jax.numpy as jnp
from jax import lax
from jax.experimental import pallas as pl
from jax.experimental.pallas import tpu as pltpu

D_MODEL = 2048
N_HEADS = 8
D_QK = 64
D_V = 2 * D_QK
ATT_W = N_HEADS * D_V
CONV_W = D_MODEL // 2
CONV_K = 3
CHUNK = 64
NUM_BUCKETS = 32
MAX_DISTANCE = 128
EPS = 1e-6
MASKED = -1e30

OFF_Q, OFF_K, OFF_V, OFF_ZA = 0, 1024, 2048, 3072
OFF_BG, OFF_CG, OFF_H, OFF_ZC = 4096, 5120, 6144, 7168
OFF_GA, OFF_GC = 8192, 10240

ATT_BLOCK = 256
V7X_VMEM_LIMIT = 56 * 1024 * 1024

F32 = jnp.float32
BF16 = jnp.bfloat16


def _params(sem, vmem=V7X_VMEM_LIMIT):
    return pltpu.CompilerParams(dimension_semantics=sem, vmem_limit_bytes=vmem)


def _bucket_py(rel):
    half = NUM_BUCKETS // 2
    max_exact = half // 2
    steps = half - max_exact
    n = abs(rel)
    if n < max_exact:
        v = n
    else:
        v = max_exact + sum(n ** steps * max_exact ** j >= MAX_DISTANCE ** j * max_exact ** steps
                            for j in range(1, steps))
    return v + (half if rel > 0 else 0)


NEAR = MAX_DISTANCE
FAR_BUCKET = _bucket_py(-NEAR)
_BREAKS = tuple((rel, _bucket_py(rel)) for rel in range(-NEAR + 1, NEAR + 1)
                if _bucket_py(rel) != _bucket_py(rel - 1))


def _bias_minus_far(rel, value_of):
    val = value_of(FAR_BUCKET)
    for first_rel, bucket in _BREAKS:
        val = jnp.where(rel >= first_rel, value_of(bucket), val)
    return val - value_of(FAR_BUCKET)


_BIAS_ROWS = 32


def _pbias_kernel(rb_ref, pb_ref):
    h = pl.program_id(0)
    tb = ATT_BLOCK
    value_of = lambda b: rb_ref[b, h]
    for t in range(2):
        def rows(i, carry, t=t):
            r0 = pl.multiple_of(i * _BIAS_ROWS, _BIAS_ROWS)
            r = r0 + lax.broadcasted_iota(jnp.int32, (_BIAS_ROWS, tb), 0)
            c = lax.broadcasted_iota(jnp.int32, (_BIAS_ROWS, tb), 1)
            val = _bias_minus_far(c - r - (1 - t) * tb, value_of)
            if t == 1:
                val = jnp.where(c // CHUNK <= r // CHUNK, val, MASKED)
            pb_ref[0, t, pl.ds(r0, _BIAS_ROWS), :] = val
            return carry

        lax.fori_loop(0, tb // _BIAS_ROWS, rows, 0)


def _prompt_bias(rel_bias):
    tb = ATT_BLOCK
    return pl.pallas_call(
        _pbias_kernel,
        out_shape=jax.ShapeDtypeStruct((N_HEADS, 2, tb, tb), F32),
        grid=(N_HEADS,),
        in_specs=[pl.BlockSpec(memory_space=pltpu.SMEM)],
        out_specs=pl.BlockSpec((1, 2, tb, tb), lambda h: (h, 0, 0, 0)),
        compiler_params=_params(("parallel",)),
        name="prompt_bias",
    )(rel_bias)


def _sbias_kernel(rbx_ref, lq1_ref, lk1_ref, lq2_ref, lk2_ref, sbc_ref, sbn_ref, lam_ref,
                  *, past, lam_init):
    nq, ncol = sbn_ref.shape
    value_of = lambda b: rbx_ref[b:b + 1, :]

    def tile(nk, k0):
        kpos = k0 + lax.broadcasted_iota(jnp.int32, (nk, ncol), 0)
        qpos = past + lax.broadcasted_iota(jnp.int32, (nk, ncol), 1) % nq
        val = _bias_minus_far(kpos - qpos, value_of)
        return jnp.where(kpos // CHUNK <= qpos // CHUNK, val, MASKED)

    sbc_ref[...] = tile(sbc_ref.shape[0], past - sbc_ref.shape[0])
    sbn_ref[...] = tile(nq, past)

    s1 = jnp.sum(lq1_ref[...].astype(F32) * lk1_ref[...].astype(F32), axis=-1, keepdims=True)
    s2 = jnp.sum(lq2_ref[...].astype(F32) * lk2_ref[...].astype(F32), axis=-1, keepdims=True)
    lam_ref[...] = jnp.exp(s1) - jnp.exp(s2) + lam_init


def _sample_bias_and_lambda(rel_bias, lq1, lk1, lq2, lk2, past, nq, lam_init):
    ncol = 2 * N_HEADS * nq
    rbx = jnp.tile(jnp.repeat(rel_bias, nq, axis=1), (1, 2))
    vmem = pl.BlockSpec(memory_space=pltpu.VMEM)
    return pl.pallas_call(
        functools.partial(_sbias_kernel, past=past, lam_init=lam_init),
        out_shape=(jax.ShapeDtypeStruct((NEAR, ncol), F32),
                   jax.ShapeDtypeStruct((nq, ncol), F32),
                   jax.ShapeDtypeStruct((1, 1), F32)),
        in_specs=[vmem] * 5,
        out_specs=(vmem, vmem, vmem),
        name="sample_bias_lambda",
    )(rbx, lq1, lk1, lq2, lk2)


def _prenorm_kernel(x_ref, g_ref, o_ref):
    xf = x_ref[...].astype(F32)
    xf = xf * lax.rsqrt(jnp.mean(xf * xf, axis=-1, keepdims=True) + EPS)
    o_ref[...] = (xf * g_ref[...]).astype(o_ref.dtype)


def _prenorm(x, g, tm):
    m, d = x.shape
    return pl.pallas_call(
        _prenorm_kernel,
        out_shape=jax.ShapeDtypeStruct((m, d), BF16),
        grid=(m // tm,),
        in_specs=[pl.BlockSpec((tm, d), lambda i: (i, 0)),
                  pl.BlockSpec((1, d), lambda i: (0, 0))],
        out_specs=pl.BlockSpec((tm, d), lambda i: (i, 0)),
        compiler_params=_params(("parallel",)),
        name="prenorm",
    )(x, g)


def _silu(x):
    return x * jax.nn.sigmoid(x)


def _proj_kernel(x_ref, w_ref, *o_refs, epilogue):
    r = jnp.dot(x_ref[...], w_ref[...], preferred_element_type=F32)
    for o_ref, o in zip(o_refs, epilogue(r)):
        o_ref[...] = o.astype(o_ref.dtype)


def _proj(xn, w, col_off, n, epilogue, out_dtypes, tm, tn, name):
    m, d = xn.shape
    jb = col_off // tn
    return pl.pallas_call(
        functools.partial(_proj_kernel, epilogue=epilogue),
        out_shape=tuple(jax.ShapeDtypeStruct((m, n), dt) for dt in out_dtypes),
        grid=(n // tn, m // tm),
        in_specs=[pl.BlockSpec((tm, d), lambda j, i: (i, 0)),
                  pl.BlockSpec((d, tn), lambda j, i: (0, jb + j))],
        out_specs=tuple(pl.BlockSpec((tm, tn), lambda j, i: (i, j)) for _ in out_dtypes),
        compiler_params=_params(("parallel", "parallel")),
        name=name,
    )(xn, w)


def _conv_kernel(x_ref, wb_ref, wc_ref, wh_ref, wz_ref, st_ref, cw_ref,
                 t_ref, nc_ref, carry_ref):
    ti = pl.program_id(2)

    @pl.when(ti == 0)
    def _():
        carry_ref[:, 6:8, :] = st_ref[...]

    tb, tt, d = x_ref.shape
    x = x_ref[...].reshape(tb * tt, d)

    def mm(w_ref):
        return jnp.dot(x, w_ref[...], preferred_element_type=F32).reshape(tb, tt, -1)

    u = mm(wc_ref) * mm(wh_ref)
    c0 = carry_ref[:, 6:7, :]
    c1 = carry_ref[:, 7:8, :]
    row = lax.broadcasted_iota(jnp.int32, u.shape, 1)
    u1 = jnp.where(row == 0, c1, pltpu.roll(u, 1, 1))
    u2 = jnp.where(row == 0, c0, jnp.where(row == 1, c1, pltpu.roll(u, 2, 1)))
    y = cw_ref[0:1, :] * u2 + cw_ref[1:2, :] * u1 + cw_ref[2:3, :] * u
    carry_ref[...] = u[:, tt - 8:, :]
    nc_ref[...] = carry_ref[:, 6:8, :]
    t_ref[...] = (mm(wb_ref) * y * _silu(mm(wz_ref))).astype(t_ref.dtype)


def _conv_branch(xn3, w, state, conv_w, tb, tt, tc):
    b, t, d = xn3.shape
    wspec = lambda off: pl.BlockSpec((d, tc), lambda c, bi, ti, o=off // tc: (0, o + c))
    return pl.pallas_call(
        _conv_kernel,
        out_shape=(jax.ShapeDtypeStruct((b, t, CONV_W), BF16),
                   jax.ShapeDtypeStruct((b, CONV_K - 1, CONV_W), F32)),
        grid=(CONV_W // tc, b // tb, t // tt),
        in_specs=[pl.BlockSpec((tb, tt, d), lambda c, bi, ti: (bi, ti, 0)),
                  wspec(OFF_BG), wspec(OFF_CG), wspec(OFF_H), wspec(OFF_ZC),
                  pl.BlockSpec((tb, CONV_K - 1, tc), lambda c, bi, ti: (bi, 0, c)),
                  pl.BlockSpec((CONV_K, tc), lambda c, bi, ti: (0, c))],
        out_specs=(pl.BlockSpec((tb, tt, tc), lambda c, bi, ti: (bi, ti, c)),
                   pl.BlockSpec((tb, CONV_K - 1, tc), lambda c, bi, ti: (bi, 0, c))),
        scratch_shapes=[pltpu.VMEM((tb, 8, tc), F32)],
        compiler_params=_params(("parallel", "parallel", "arbitrary")),
        name="conv_branch",
    )(xn3, w, w, w, w, state, conv_w)


_NT = (((1,), (1,)), ((), ()))
_TN = (((0,), (0,)), ((), ()))


def _head_out(o, hn, sza, lam_init):
    o = o * lax.rsqrt(jnp.mean(o * o, axis=-1, keepdims=True) + EPS)
    return o * hn * (1.0 - lam_init) * sza


def _pattn_kernel(lam_ref, q_ref, k_ref, v_ref, bias_ref, sza_ref, hn_ref, o_ref,
                  acc1, acc2, m1, l1, m2, l2, *, lam_init):
    qi = pl.program_id(2)
    tb = ATT_BLOCK
    q = q_ref[0]
    lane = lax.broadcasted_iota(jnp.int32, q.shape, 1)
    zero = jnp.zeros_like(q)
    q1 = jnp.where(lane < D_QK, q, zero)
    q2 = jnp.where(lane >= D_QK, q, zero)

    for m_ref, l_ref, acc in ((m1, l1, acc1), (m2, l2, acc2)):
        m_ref[...] = jnp.full(m_ref.shape, -jnp.inf, F32)
        l_ref[...] = jnp.zeros(l_ref.shape, F32)
        acc[...] = jnp.zeros(acc.shape, F32)

    def update(qm, kj, vj, bias, m_ref, l_ref, acc):
        s = lax.dot_general(qm, kj, _NT, preferred_element_type=F32)
        if bias is not None:
            s = s + bias
        m_old = m_ref[...]
        m_new = jnp.maximum(m_old, jnp.max(s, axis=-1, keepdims=True))
        p = jnp.exp(s - m_new)
        alpha = jnp.exp(m_old - m_new)
        l_ref[...] = alpha * l_ref[...] + jnp.sum(p, axis=-1, keepdims=True)
        acc[...] = alpha * acc[...] + jnp.dot(p.astype(BF16), vj, preferred_element_type=F32)
        m_ref[...] = m_new

    def step(j, bias):
        start = pl.multiple_of(j * tb, tb)
        kj = k_ref[0, pl.ds(start, tb), :]
        vj = v_ref[0, pl.ds(start, tb), :]
        update(q1, kj, vj, bias, m1, l1, acc1)
        update(q2, kj, vj, bias, m2, l2, acc2)

    def far_body(j, carry):
        step(j, None)
        return carry

    lax.fori_loop(0, qi - 1, far_body, 0)

    @pl.when(qi >= 1)
    def _():
        step(qi - 1, bias_ref[0, 0])

    step(qi, bias_ref[0, 1])

    lam = lam_ref[0, 0]
    o = acc1[...] / l1[...] - lam * (acc2[...] / l2[...])
    o_ref[0] = _head_out(o, hn_ref[...], sza_ref[0].astype(F32), lam_init).astype(o_ref.dtype)


def _prompt_attention(lam, q, kb, vb, pbias, sza, head_norm, lam_init):
    b, t, _ = q.shape
    tb = ATT_BLOCK
    assert tb >= NEAR and t % tb == 0
    blk = pl.BlockSpec((1, tb, D_V), lambda bi, h, qi: (bi, qi, h))
    full = pl.BlockSpec((1, t, D_V), lambda bi, h, qi: (bi, 0, h))
    return pl.pallas_call(
        functools.partial(_pattn_kernel, lam_init=lam_init),
        out_shape=jax.ShapeDtypeStruct((b, t, ATT_W), BF16),
        grid=(b, N_HEADS, t // tb),
        in_specs=[pl.BlockSpec(memory_space=pltpu.SMEM),
                  blk, full, full,
                  pl.BlockSpec((1, 2, tb, tb), lambda bi, h, qi: (h, 0, 0, 0)),
                  blk,
                  pl.BlockSpec((1, D_V), lambda bi, h, qi: (0, 0))],
        out_specs=blk,
        scratch_shapes=[pltpu.VMEM((tb, D_V), F32), pltpu.VMEM((tb, D_V), F32),
                        pltpu.VMEM((tb, 1), F32), pltpu.VMEM((tb, 1), F32),
                        pltpu.VMEM((tb, 1), F32), pltpu.VMEM((tb, 1), F32)],
        compiler_params=_params(("parallel", "parallel", "arbitrary")),
        name="prompt_attention",
    )(lam, q, kb, vb, pbias, sza, head_norm)


def _sattn_kernel(lam_ref, q_ref, kc_ref, vc_ref, kn_ref, vn_ref, bc_ref, bn_ref,
                  sza_ref, hn_ref, o_ref, *, lam_init):
    nq = q_ref.shape[1]
    past = kc_ref.shape[1]
    far_n = past - bc_ref.shape[0]
    half = N_HEADS * nq
    q = q_ref[0]
    qt = jnp.broadcast_to(q[None], (2 * N_HEADS, nq, ATT_W)).reshape(2 * half, ATT_W)
    r = lax.broadcasted_iota(jnp.int32, qt.shape, 0)
    c = lax.broadcasted_iota(jnp.int32, qt.shape, 1)
    group = 2 * ((r % half) // nq) + r // half
    qbd = jnp.where(c // D_QK == group, qt, jnp.zeros_like(qt))

    st = lax.dot_general(kc_ref[0].astype(BF16), qbd, _NT, preferred_element_type=F32)
    parts = (st[:far_n],
             st[far_n:] + bc_ref[...],
             lax.dot_general(kn_ref[0], qbd, _NT, preferred_element_type=F32) + bn_ref[...])
    mx = functools.reduce(jnp.maximum, [jnp.max(s, axis=0, keepdims=True) for s in parts])
    exps = [jnp.exp(s - mx) for s in parts]
    denom = functools.reduce(jnp.add, [jnp.sum(e, axis=0, keepdims=True) for e in exps])
    col = lax.broadcasted_iota(jnp.int32, denom.shape, 1)
    w = jnp.where(col < half, 1.0, -lam_ref[0, 0]) / denom

    def combine(e):
        p = e * w
        return (p[:, :half] + p[:, half:]).astype(BF16)

    vc = vc_ref[0].astype(BF16)
    of = (lax.dot_general(combine(exps[0]), vc[:far_n], _TN, preferred_element_type=F32)
          + lax.dot_general(combine(exps[1]), vc[far_n:], _TN, preferred_element_type=F32)
          + lax.dot_general(combine(exps[2]), vn_ref[0], _TN, preferred_element_type=F32))
    rr = lax.broadcasted_iota(jnp.int32, of.shape, 0) // nq
    cc = lax.broadcasted_iota(jnp.int32, of.shape, 1) // D_V
    o = jnp.sum(jnp.where(rr == cc, of, 0.0).reshape(N_HEADS, nq, ATT_W), axis=0)
    sza = sza_ref[0].astype(F32)
    for h in range(N_HEADS):
        sl = slice(h * D_V, (h + 1) * D_V)
        o_ref[0, :, sl] = _head_out(o[:, sl], hn_ref[...], sza[:, sl], lam_init).astype(o_ref.dtype)


def _sample_attention(lam, q, cache_k, cache_v, kb, vb, sbc, sbn, sza, head_norm, lam_init):
    b, nq, _ = q.shape
    past = cache_k.shape[1]
    new = pl.BlockSpec((1, nq, ATT_W), lambda bi: (bi, 0, 0))
    cache = pl.BlockSpec((1, past, ATT_W), lambda bi: (bi, 0, 0))
    return pl.pallas_call(
        functools.partial(_sattn_kernel, lam_init=lam_init),
        out_shape=jax.ShapeDtypeStruct((b, nq, ATT_W), BF16),
        grid=(b,),
        in_specs=[pl.BlockSpec(memory_space=pltpu.SMEM),
                  new, cache, cache, new, new,
                  pl.BlockSpec(sbc.shape, lambda bi: (0, 0)),
                  pl.BlockSpec(sbn.shape, lambda bi: (0, 0)),
                  new,
                  pl.BlockSpec((1, D_V), lambda bi: (0, 0))],
        out_specs=new,
        compiler_params=_params(("parallel",)),
        name="sample_attention",
    )(lam, q, cache_k, cache_v, kb, vb, sbc, sbn, sza, head_norm)


def _out_kernel(og_ref, t_ref, sga_ref, sgc_ref, wpa_ref, wpc_ref, wo_ref, x_ref, g_ref,
                o_ref, acc_ref):
    c = pl.program_id(1)
    ya = jnp.dot(og_ref[...], wpa_ref[...], preferred_element_type=F32)
    yc = jnp.dot(t_ref[...], wpc_ref[...], preferred_element_type=F32)
    merged = sga_ref[...].astype(F32) * ya + sgc_ref[...].astype(F32) * yc
    contrib = jnp.dot(merged.astype(BF16), wo_ref[...], preferred_element_type=F32)

    @pl.when(c == 0)
    def _():
        acc_ref[...] = contrib

    @pl.when(c > 0)
    def _():
        acc_ref[...] += contrib

    @pl.when(c == pl.num_programs(1) - 1)
    def _():
        y = acc_ref[...]
        y = y * lax.rsqrt(jnp.mean(y * y, axis=-1, keepdims=True) + EPS)
        o_ref[...] = x_ref[...] + y * g_ref[...]


def _merge_out(og, t, sga, sgc, wpa, wpc, wo, x, g, tm, tc):
    m, d = x.shape
    return pl.pallas_call(
        _out_kernel,
        out_shape=jax.ShapeDtypeStruct((m, d), x.dtype),
        grid=(m // tm, d // tc),
        in_specs=[pl.BlockSpec((tm, ATT_W), lambda i, c: (i, 0)),
                  pl.BlockSpec((tm, CONV_W), lambda i, c: (i, 0)),
                  pl.BlockSpec((tm, tc), lambda i, c: (i, c)),
                  pl.BlockSpec((tm, tc), lambda i, c: (i, c)),
                  pl.BlockSpec((ATT_W, tc), lambda i, c: (0, c)),
                  pl.BlockSpec((CONV_W, tc), lambda i, c: (0, c)),
                  pl.BlockSpec((tc, d), lambda i, c: (c, 0)),
                  pl.BlockSpec((tm, d), lambda i, c: (i, 0)),
                  pl.BlockSpec((1, d), lambda i, c: (0, 0))],
        out_specs=pl.BlockSpec((tm, d), lambda i, c: (i, 0)),
        scratch_shapes=[pltpu.VMEM((tm, d), F32)],
        compiler_params=_params(("parallel", "arbitrary")),
        name="merge_out",
    )(og, t, sga, sgc, wpa, wpc, wo, x, g)


def _layer(x, conv_state, weights, attend, tm, conv_tiles):
    (norm_pre, norm_post, w_in, conv_w, wpa, wpc, wo) = weights
    b, t, d = x.shape
    m = b * t
    x2 = x.reshape(m, d)
    xn = _prenorm(x2, norm_pre.reshape(1, d), tm)
    tn = 1024
    scale = 1.0 / math.sqrt(D_QK)
    (q,) = _proj(xn, w_in, OFF_Q, ATT_W, lambda r: (r * scale,), (BF16,), tm, tn, "proj_q")
    k, kb = _proj(xn, w_in, OFF_K, ATT_W, lambda r: (r, r), (F32, BF16), tm, tn, "proj_k")
    v, vb = _proj(xn, w_in, OFF_V, ATT_W, lambda r: (r, r), (F32, BF16), tm, tn, "proj_v")
    (sza,) = _proj(xn, w_in, OFF_ZA, ATT_W, lambda r: (_silu(r),), (BF16,), tm, tn, "proj_za")
    (sga,) = _proj(xn, w_in, OFF_GA, D_MODEL, lambda r: (jax.nn.sigmoid(r),), (BF16,), tm, tn, "proj_ga")
    (sgc,) = _proj(xn, w_in, OFF_GC, D_MODEL, lambda r: (jax.nn.sigmoid(r),), (BF16,), tm, tn, "proj_gc")
    tcv, new_conv = _conv_branch(xn.reshape(b, t, d), w_in, conv_state, conv_w, *conv_tiles)
    to3 = lambda a: a.reshape(b, t, ATT_W)
    og = attend(to3(q), to3(kb), to3(vb), to3(sza))
    y = _merge_out(og.reshape(m, ATT_W), tcv.reshape(m, CONV_W), sga, sgc, wpa, wpc, wo,
                   x2, norm_post.reshape(1, d), tm, 512)
    return (y.reshape(b, t, d), k.reshape(b, t, N_HEADS, 2 * D_QK), v.reshape(b, t, N_HEADS, D_V),
            new_conv)


def kernel(x_prompt, x_sample, cache_k, cache_v, state_conv, norm_pre, norm_post, w_in,
           lambda_q1, lambda_k1, lambda_q2, lambda_k2, head_norm, conv_w, w_proj_attn,
           w_proj_conv, w_out, rel_bias):
    depth = w_in.shape[0]
    past = cache_k.shape[2]
    dec_b, dec_t = x_sample.shape[0], x_sample.shape[1]
    xp, xs = x_prompt, x_sample
    zero_conv = jnp.zeros((xp.shape[0], CONV_K - 1, CONV_W), xp.dtype)
    pbias = _prompt_bias(rel_bias)
    outs = [[] for _ in range(6)]
    for l in range(depth):
        lam_init = 0.8 - 0.6 * math.exp(-0.3 * l)
        sbc, sbn, lam = _sample_bias_and_lambda(
            rel_bias, lambda_q1[l][None], lambda_k1[l][None], lambda_q2[l][None], lambda_k2[l][None],
            past, dec_t, lam_init)
        hn = head_norm[l].reshape(1, D_V)
        weights = (norm_pre[l], norm_post[l], w_in[l].astype(BF16), conv_w[l],
                   w_proj_attn[l].astype(BF16), w_proj_conv[l].astype(BF16), w_out[l].astype(BF16))

        def attend_prompt(q, kb, vb, sza):
            return _prompt_attention(lam, q, kb, vb, pbias, sza, hn, lam_init)

        ck = cache_k[l].reshape(dec_b, past, ATT_W)
        cv = cache_v[l].reshape(dec_b, past, ATT_W)

        def attend_sample(q, kb, vb, sza):
            return _sample_attention(lam, q, ck, cv, kb, vb, sbc, sbn, sza, hn, lam_init)

        xp, kp, vp, cp = _layer(xp, zero_conv, weights, attend_prompt, 512, (1, 512, 512))
        xs, ksm, vsm, csm = _layer(xs, state_conv[l], weights, attend_sample, 512,
                                   (dec_b, dec_t, 512))
        for lst, val in zip(outs, (kp, vp, cp, ksm, vsm, csm)):
            lst.append(val)
    return (xp, xs) + tuple(jnp.stack(o, axis=0) for o in outs)
```

```python
import functools
import math

import jax
import jax.numpy as jnp
from jax import lax
from jax.experimental import pallas as pl
from jax.experimental.pallas import tpu as pltpu

D_MODEL = 2048
N_HEADS = 8
D_QK = 64
D_V = 2 * D_QK
ATT_W = N_HEADS * D_V
CONV_W = D_MODEL // 2
CONV_K = 3
CHUNK = 64
NUM_BUCKETS = 32
MAX_DISTANCE = 128
EPS = 1e-6
MASKED = -1e30
LOG2E = math.log2(math.e)

OFF_Q, OFF_K, OFF_V, OFF_ZA = 0, 1024, 2048, 3072
OFF_BG, OFF_CG, OFF_H, OFF_ZC = 4096, 5120, 6144, 7168
OFF_GA, OFF_GC = 8192, 10240

ATT_BLOCK = 256
V7X_VMEM_LIMIT = 56 * 1024 * 1024

F32 = jnp.float32
BF16 = jnp.bfloat16


def _params(sem, vmem=V7X_VMEM_LIMIT):
    return pltpu.CompilerParams(dimension_semantics=sem, vmem_limit_bytes=vmem)


def _bucket_py(rel):
    half = NUM_BUCKETS // 2
    max_exact = half // 2
    steps = half - max_exact
    n = abs(rel)
    if n < max_exact:
        v = n
    else:
        v = max_exact + sum(n ** steps * max_exact ** j >= MAX_DISTANCE ** j * max_exact ** steps
                            for j in range(1, steps))
    return v + (half if rel > 0 else 0)


NEAR = MAX_DISTANCE
FAR_BUCKET = _bucket_py(-NEAR)
_BREAKS = tuple((rel, _bucket_py(rel)) for rel in range(-NEAR + 1, NEAR + 1)
                if _bucket_py(rel) != _bucket_py(rel - 1))


def _bias_minus_far(rel, value_of):
    val = value_of(FAR_BUCKET)
    for first_rel, bucket in _BREAKS:
        val = jnp.where(rel >= first_rel, value_of(bucket), val)
    return (val - value_of(FAR_BUCKET)) * LOG2E


_BIAS_ROWS = 32


def _pbias_kernel(rb_ref, pb_ref):
    h = pl.program_id(0)
    tb = ATT_BLOCK
    value_of = lambda b: rb_ref[b, h]
    for t in range(2):
        def rows(i, carry, t=t):
            r0 = pl.multiple_of(i * _BIAS_ROWS, _BIAS_ROWS)
            r = r0 + lax.broadcasted_iota(jnp.int32, (_BIAS_ROWS, tb), 0)
            c = lax.broadcasted_iota(jnp.int32, (_BIAS_ROWS, tb), 1)
            val = _bias_minus_far(c - r - (1 - t) * tb, value_of)
            if t == 1:
                val = jnp.where(c // CHUNK <= r // CHUNK, val, MASKED)
            pb_ref[0, t, pl.ds(r0, _BIAS_ROWS), :] = val
            return carry

        lax.fori_loop(0, tb // _BIAS_ROWS, rows, 0)


def _prompt_bias(rel_bias):
    tb = ATT_BLOCK
    return pl.pallas_call(
        _pbias_kernel,
        out_shape=jax.ShapeDtypeStruct((N_HEADS, 2, tb, tb), F32),
        grid=(N_HEADS,),
        in_specs=[pl.BlockSpec(memory_space=pltpu.SMEM)],
        out_specs=pl.BlockSpec((1, 2, tb, tb), lambda h: (h, 0, 0, 0)),
        compiler_params=_params(("parallel",)),
        name="prompt_bias",
    )(rel_bias)


def _sbias_kernel(rb_ref, lq1_ref, lk1_ref, lq2_ref, lk2_ref, sbc_ref, sbn_ref, lam_ref,
                  *, past, lam_init):
    nq = sbn_ref.shape[2]

    def tile(shape, k0, value_of):
        qpos = past + lax.broadcasted_iota(jnp.int32, shape, 0) % nq
        kpos = k0 + lax.broadcasted_iota(jnp.int32, shape, 1)
        val = _bias_minus_far(kpos - qpos, value_of)
        return jnp.where(kpos // CHUNK <= qpos // CHUNK, val, MASKED)

    for h in range(N_HEADS):
        value_of = lambda b, h=h: rb_ref[b, h]
        sbc_ref[h] = tile(sbc_ref.shape[1:], past - sbc_ref.shape[2], value_of)
        sbn_ref[h] = tile(sbn_ref.shape[1:], past, value_of)

    s1 = jnp.sum(lq1_ref[...].astype(F32) * lk1_ref[...].astype(F32), axis=-1, keepdims=True)
    s2 = jnp.sum(lq2_ref[...].astype(F32) * lk2_ref[...].astype(F32), axis=-1, keepdims=True)
    lam_ref[...] = jnp.exp(s1) - jnp.exp(s2) + lam_init


def _sample_bias_and_lambda(rel_bias, lq1, lk1, lq2, lk2, past, nq, lam_init):
    vmem = pl.BlockSpec(memory_space=pltpu.VMEM)
    return pl.pallas_call(
        functools.partial(_sbias_kernel, past=past, lam_init=lam_init),
        out_shape=(jax.ShapeDtypeStruct((N_HEADS, 2 * nq, NEAR), F32),
                   jax.ShapeDtypeStruct((N_HEADS, 2 * nq, nq), F32),
                   jax.ShapeDtypeStruct((1, 1), F32)),
        in_specs=[pl.BlockSpec(memory_space=pltpu.SMEM)] + [vmem] * 4,
        out_specs=(vmem, vmem, vmem),
        name="sample_bias_lambda",
    )(rel_bias, lq1, lk1, lq2, lk2)


def _prenorm_kernel(x_ref, g_ref, o_ref):
    xf = x_ref[...].astype(F32)
    xf = xf * lax.rsqrt(jnp.mean(xf * xf, axis=-1, keepdims=True) + EPS)
    o_ref[...] = (xf * g_ref[...]).astype(o_ref.dtype)


def _prenorm(x, g, tm):
    m, d = x.shape
    return pl.pallas_call(
        _prenorm_kernel,
        out_shape=jax.ShapeDtypeStruct((m, d), BF16),
        grid=(m // tm,),
        in_specs=[pl.BlockSpec((tm, d), lambda i: (i, 0)),
                  pl.BlockSpec((1, d), lambda i: (0, 0))],
        out_specs=pl.BlockSpec((tm, d), lambda i: (i, 0)),
        compiler_params=_params(("parallel",)),
        name="prenorm",
    )(x, g)


def _silu(x):
    return x * jax.nn.sigmoid(x)


def _proj_kernel(x_ref, w_ref, *o_refs, epilogue):
    r = jnp.dot(x_ref[...], w_ref[...], preferred_element_type=F32)
    for o_ref, o in zip(o_refs, epilogue(r)):
        o_ref[...] = o.astype(o_ref.dtype)


def _proj(xn, w, col_off, n, epilogue, out_dtypes, tm, tn, name):
    m, d = xn.shape
    jb = col_off // tn
    return pl.pallas_call(
        functools.partial(_proj_kernel, epilogue=epilogue),
        out_shape=tuple(jax.ShapeDtypeStruct((m, n), dt) for dt in out_dtypes),
        grid=(n // tn, m // tm),
        in_specs=[pl.BlockSpec((tm, d), lambda j, i: (i, 0)),
                  pl.BlockSpec((d, tn), lambda j, i: (0, jb + j))],
        out_specs=tuple(pl.BlockSpec((tm, tn), lambda j, i: (i, j)) for _ in out_dtypes),
        compiler_params=_params(("parallel", "parallel")),
        name=name,
    )(xn, w)


def _conv_kernel(x_ref, wb_ref, wc_ref, wh_ref, wz_ref, st_ref, cw_ref,
                 t_ref, nc_ref, carry_ref):
    ti = pl.program_id(2)

    @pl.when(ti == 0)
    def _():
        carry_ref[:, 6:8, :] = st_ref[...]

    tb, tt, d = x_ref.shape
    x = x_ref[...].reshape(tb * tt, d)

    def mm(w_ref):
        return jnp.dot(x, w_ref[...], preferred_element_type=F32).reshape(tb, tt, -1)

    u = mm(wc_ref) * mm(wh_ref)
    c0 = carry_ref[:, 6:7, :]
    c1 = carry_ref[:, 7:8, :]
    row = lax.broadcasted_iota(jnp.int32, u.shape, 1)
    u1 = jnp.where(row == 0, c1, pltpu.roll(u, 1, 1))
    u2 = jnp.where(row == 0, c0, jnp.where(row == 1, c1, pltpu.roll(u, 2, 1)))
    y = cw_ref[0:1, :] * u2 + cw_ref[1:2, :] * u1 + cw_ref[2:3, :] * u
    carry_ref[...] = u[:, tt - 8:, :]
    nc_ref[...] = carry_ref[:, 6:8, :]
    t_ref[...] = (mm(wb_ref) * y * _silu(mm(wz_ref))).astype(t_ref.dtype)


def _conv_branch(xn3, w, state, conv_w, tb, tt, tc):
    b, t, d = xn3.shape
    wspec = lambda off: pl.BlockSpec((d, tc), lambda c, bi, ti, o=off // tc: (0, o + c))
    return pl.pallas_call(
        _conv_kernel,
        out_shape=(jax.ShapeDtypeStruct((b, t, CONV_W), BF16),
                   jax.ShapeDtypeStruct((b, CONV_K - 1, CONV_W), F32)),
        grid=(CONV_W // tc, b // tb, t // tt),
        in_specs=[pl.BlockSpec((tb, tt, d), lambda c, bi, ti: (bi, ti, 0)),
                  wspec(OFF_BG), wspec(OFF_CG), wspec(OFF_H), wspec(OFF_ZC),
                  pl.BlockSpec((tb, CONV_K - 1, tc), lambda c, bi, ti: (bi, 0, c)),
                  pl.BlockSpec((CONV_K, tc), lambda c, bi, ti: (0, c))],
        out_specs=(pl.BlockSpec((tb, tt, tc), lambda c, bi, ti: (bi, ti, c)),
                   pl.BlockSpec((tb, CONV_K - 1, tc), lambda c, bi, ti: (bi, 0, c))),
        scratch_shapes=[pltpu.VMEM((tb, 8, tc), F32)],
        compiler_params=_params(("parallel", "parallel", "arbitrary")),
        name="conv_branch",
    )(xn3, w, w, w, w, state, conv_w)


_NT = (((1,), (1,)), ((), ()))


def _head_out(o, hn, sza, lam_init):
    o = o * lax.rsqrt(jnp.mean(o * o, axis=-1, keepdims=True) + EPS)
    return o * hn * (1.0 - lam_init) * sza


def _pattn_kernel(lam_ref, q_ref, k_ref, v_ref, bias_ref, sza_ref, hn_ref, o_ref,
                  s_ref, p_ref, *, lam_init):
    qi = pl.program_id(2)
    tb = ATT_BLOCK
    hw = tb // 2
    q = q_ref[0]
    lane = lax.broadcasted_iota(jnp.int32, q.shape, 1)
    zero = jnp.zeros_like(q)
    qmaps = (jnp.where(lane < D_QK, q, zero), jnp.where(lane >= D_QK, q, zero))

    def attend(n):
        outs = []
        for mi, qm in enumerate(qmaps):
            mpart = None
            for j in range(n + 1):
                s = lax.dot_general(qm, k_ref[0, j * tb:(j + 1) * tb, :], _NT,
                                    preferred_element_type=F32)
                if j == n:
                    s = s + bias_ref[0, 1]
                elif j == n - 1:
                    s = s + bias_ref[0, 0]
                s_ref[mi, :, j * tb:(j + 1) * tb] = s
                part = jnp.maximum(s[:, :hw], s[:, hw:])
                mpart = part if mpart is None else jnp.maximum(mpart, part)
            m = jnp.broadcast_to(jnp.max(mpart, axis=-1, keepdims=True), (tb, hw))
            lpart = None
            for c in range(2 * (n + 1)):
                e = jnp.exp2(s_ref[mi, :, c * hw:(c + 1) * hw] - m)
                p_ref[mi, :, c * hw:(c + 1) * hw] = e.astype(BF16)
                lpart = e if lpart is None else lpart + e
            l = jnp.sum(lpart, axis=-1, keepdims=True)
            o = jnp.dot(p_ref[mi, :, :(n + 1) * tb], v_ref[0, :(n + 1) * tb, :],
                        preferred_element_type=F32)
            outs.append(o / l)
        o = outs[0] - lam_ref[0, 0] * outs[1]
        o_ref[0] = _head_out(o, hn_ref[...], sza_ref[0].astype(F32), lam_init).astype(o_ref.dtype)

    for n in range(k_ref.shape[1] // tb):
        pl.when(qi == n)(functools.partial(attend, n))


def _prompt_attention(lam, q, kb, vb, pbias, sza, head_norm, lam_init):
    b, t, _ = q.shape
    tb = ATT_BLOCK
    assert tb >= NEAR and t % tb == 0
    blk = pl.BlockSpec((1, tb, D_V), lambda bi, h, qi: (bi, qi, h))
    full = pl.BlockSpec((1, t, D_V), lambda bi, h, qi: (bi, 0, h))
    return pl.pallas_call(
        functools.partial(_pattn_kernel, lam_init=lam_init),
        out_shape=jax.ShapeDtypeStruct((b, t, ATT_W), BF16),
        grid=(b, N_HEADS, t // tb),
        in_specs=[pl.BlockSpec(memory_space=pltpu.SMEM),
                  blk, full, full,
                  pl.BlockSpec((1, 2, tb, tb), lambda bi, h, qi: (h, 0, 0, 0)),
                  blk,
                  pl.BlockSpec((1, D_V), lambda bi, h, qi: (0, 0))],
        out_specs=blk,
        scratch_shapes=[pltpu.VMEM((2, tb, t), F32), pltpu.VMEM((2, tb, t), BF16)],
        compiler_params=_params(("parallel", "parallel", "arbitrary")),
        name="prompt_attention",
    )(lam, q, kb, vb, pbias, sza, head_norm)


def _sattn_kernel(lam_ref, q_ref, kc_ref, vc_ref, kn_ref, vn_ref, bc_ref, bn_ref,
                  sza_ref, hn_ref, o_ref, *, lam_init):
    nq = q_ref.shape[2]
    past = kc_ref.shape[1] // N_HEADS
    far_n = past - bc_ref.shape[2]
    lam = lam_ref[0, 0]

    def head(h, carry):
        q = q_ref[0, h]
        lane = lax.broadcasted_iota(jnp.int32, q.shape, 1)
        zero = jnp.zeros_like(q)
        q2m = jnp.concatenate([jnp.where(lane < D_QK, q, zero), jnp.where(lane >= D_QK, q, zero)], axis=0)
        kc = kc_ref[0, pl.ds(h, past, stride=N_HEADS), :].astype(BF16)
        s = lax.dot_general(q2m, kc, _NT, preferred_element_type=F32)
        s = jnp.concatenate([s[:, :far_n], s[:, far_n:] + bc_ref[h]], axis=1)
        sn = lax.dot_general(q2m, kn_ref[0, h], _NT, preferred_element_type=F32) + bn_ref[h]
        mx = jnp.maximum(jnp.max(s, axis=-1, keepdims=True), jnp.max(sn, axis=-1, keepdims=True))
        e = jnp.exp2(s - mx)
        en = jnp.exp2(sn - mx)
        denom = jnp.sum(e, axis=-1, keepdims=True) + jnp.sum(en, axis=-1, keepdims=True)
        row = lax.broadcasted_iota(jnp.int32, denom.shape, 0)
        w = jnp.where(row < nq, 1.0, -lam) / denom

        def combine(x):
            p = x * w
            return (p[:nq] + p[nq:]).astype(BF16)

        vc = vc_ref[0, pl.ds(h, past, stride=N_HEADS), :].astype(BF16)
        o = (jnp.dot(combine(e), vc, preferred_element_type=F32)
             + jnp.dot(combine(en), vn_ref[0, h], preferred_element_type=F32))
        o_ref[0, h] = _head_out(o, hn_ref[...], sza_ref[0, h].astype(F32), lam_init).astype(o_ref.dtype)
        return carry

    lax.fori_loop(0, N_HEADS, head, 0)


def _sample_attention(lam, q, cache_k, cache_v, kb, vb, sbc, sbn, sza, head_norm, lam_init, layer):
    b, nq, _ = q.shape
    rows = cache_k.shape[1]
    heads_major = lambda a: a.reshape(b, nq, N_HEADS, D_V).transpose(0, 2, 1, 3)
    new = pl.BlockSpec((1, N_HEADS, nq, D_V), lambda bi: (bi, 0, 0, 0))
    cache = pl.BlockSpec((1, rows, D_V), lambda bi: (layer * b + bi, 0, 0))
    whole = lambda a: pl.BlockSpec(a.shape, lambda bi: (0,) * a.ndim)
    og = pl.pallas_call(
        functools.partial(_sattn_kernel, lam_init=lam_init),
        out_shape=jax.ShapeDtypeStruct((b, N_HEADS, nq, D_V), BF16),
        grid=(b,),
        in_specs=[pl.BlockSpec(memory_space=pltpu.SMEM),
                  new, cache, cache, new, new, whole(sbc), whole(sbn), new, whole(head_norm)],
        out_specs=new,
        compiler_params=_params(("parallel",)),
        name="sample_attention",
    )(lam, heads_major(q), cache_k, cache_v, heads_major(kb), heads_major(vb), sbc, sbn,
      heads_major(sza), head_norm)
    return og.transpose(0, 2, 1, 3).reshape(b, nq, ATT_W)


def _out_kernel(og_ref, t_ref, sga_ref, sgc_ref, wpa_ref, wpc_ref, wo_ref, x_ref, g_ref,
                o_ref, acc_ref):
    c = pl.program_id(1)
    ya = jnp.dot(og_ref[...], wpa_ref[...], preferred_element_type=F32)
    yc = jnp.dot(t_ref[...], wpc_ref[...], preferred_element_type=F32)
    merged = sga_ref[...].astype(F32) * ya + sgc_ref[...].astype(F32) * yc
    contrib = jnp.dot(merged.astype(BF16), wo_ref[...], preferred_element_type=F32)

    @pl.when(c == 0)
    def _():
        acc_ref[...] = contrib

    @pl.when(c > 0)
    def _():
        acc_ref[...] += contrib

    @pl.when(c == pl.num_programs(1) - 1)
    def _():
        y = acc_ref[...]
        y = y * lax.rsqrt(jnp.mean(y * y, axis=-1, keepdims=True) + EPS)
        o_ref[...] = x_ref[...] + y * g_ref[...]


def _merge_out(og, t, sga, sgc, wpa, wpc, wo, x, g, tm, tc):
    m, d = x.shape
    return pl.pallas_call(
        _out_kernel,
        out_shape=jax.ShapeDtypeStruct((m, d), x.dtype),
        grid=(m // tm, d // tc),
        in_specs=[pl.BlockSpec((tm, ATT_W), lambda i, c: (i, 0)),
                  pl.BlockSpec((tm, CONV_W), lambda i, c: (i, 0)),
                  pl.BlockSpec((tm, tc), lambda i, c: (i, c)),
                  pl.BlockSpec((tm, tc), lambda i, c: (i, c)),
                  pl.BlockSpec((ATT_W, tc), lambda i, c: (0, c)),
                  pl.BlockSpec((CONV_W, tc), lambda i, c: (0, c)),
                  pl.BlockSpec((tc, d), lambda i, c: (c, 0)),
                  pl.BlockSpec((tm, d), lambda i, c: (i, 0)),
                  pl.BlockSpec((1, d), lambda i, c: (0, 0))],
        out_specs=pl.BlockSpec((tm, d), lambda i, c: (i, 0)),
        scratch_shapes=[pltpu.VMEM((tm, d), F32)],
        compiler_params=_params(("parallel", "arbitrary")),
        name="merge_out",
    )(og, t, sga, sgc, wpa, wpc, wo, x, g)


def _layer(x, conv_state, weights, attend, tm, conv_tiles):
    (norm_pre, norm_post, w_in, conv_w, wpa, wpc, wo) = weights
    b, t, d = x.shape
    m = b * t
    x2 = x.reshape(m, d)
    xn = _prenorm(x2, norm_pre.reshape(1, d), tm)
    tn = 1024
    scale = LOG2E / math.sqrt(D_QK)
    (q,) = _proj(xn, w_in, OFF_Q, ATT_W, lambda r: (r * scale,), (BF16,), tm, tn, "proj_q")
    k, kb = _proj(xn, w_in, OFF_K, ATT_W, lambda r: (r, r), (F32, BF16), tm, tn, "proj_k")
    v, vb = _proj(xn, w_in, OFF_V, ATT_W, lambda r: (r, r), (F32, BF16), tm, tn, "proj_v")
    (sza,) = _proj(xn, w_in, OFF_ZA, ATT_W, lambda r: (_silu(r),), (BF16,), tm, tn, "proj_za")
    (sga,) = _proj(xn, w_in, OFF_GA, D_MODEL, lambda r: (jax.nn.sigmoid(r),), (BF16,), tm, tn, "proj_ga")
    (sgc,) = _proj(xn, w_in, OFF_GC, D_MODEL, lambda r: (jax.nn.sigmoid(r),), (BF16,), tm, tn, "proj_gc")
    tcv, new_conv = _conv_branch(xn.reshape(b, t, d), w_in, conv_state, conv_w, *conv_tiles)
    to3 = lambda a: a.reshape(b, t, ATT_W)
    og = attend(to3(q), to3(kb), to3(vb), to3(sza))
    y = _merge_out(og.reshape(m, ATT_W), tcv.reshape(m, CONV_W), sga, sgc, wpa, wpc, wo,
                   x2, norm_post.reshape(1, d), tm, 512)
    return (y.reshape(b, t, d), k.reshape(b, t, N_HEADS, 2 * D_QK), v.reshape(b, t, N_HEADS, D_V),
            new_conv)


def kernel(x_prompt, x_sample, cache_k, cache_v, state_conv, norm_pre, norm_post, w_in,
           lambda_q1, lambda_k1, lambda_q2, lambda_k2, head_norm, conv_w, w_proj_attn,
           w_proj_conv, w_out, rel_bias):
    depth = w_in.shape[0]
    past = cache_k.shape[2]
    dec_b, dec_t = x_sample.shape[0], x_sample.shape[1]
    xp, xs = x_prompt, x_sample
    zero_conv = jnp.zeros((xp.shape[0], CONV_K - 1, CONV_W), xp.dtype)
    pbias = _prompt_bias(rel_bias)
    outs = [[] for _ in range(6)]
    for l in range(depth):
        lam_init = 0.8 - 0.6 * math.exp(-0.3 * l)
        sbc, sbn, lam = _sample_bias_and_lambda(
            rel_bias, lambda_q1[l][None], lambda_k1[l][None], lambda_q2[l][None], lambda_k2[l][None],
            past, dec_t, lam_init)
        hn = head_norm[l].reshape(1, D_V)
        weights = (norm_pre[l], norm_post[l], w_in[l].astype(BF16), conv_w[l],
                   w_proj_attn[l].astype(BF16), w_proj_conv[l].astype(BF16), w_out[l].astype(BF16))

        def attend_prompt(q, kb, vb, sza):
            return _prompt_attention(lam, q, kb, vb, pbias, sza, hn, lam_init)

        ck = cache_k.reshape(depth * dec_b, past * N_HEADS, D_V)
        cv = cache_v.reshape(depth * dec_b, past * N_HEADS, D_V)

        def attend_sample(q, kb, vb, sza, l=l):
            return _sample_attention(lam, q, ck, cv, kb, vb, sbc, sbn, sza, hn, lam_init, l)

        xp, kp, vp, cp = _layer(xp, zero_conv, weights, attend_prompt, 512, (1, 512, 512))
        xs, ksm, vsm, csm = _layer(xs, state_conv[l], weights, attend_sample, 512,
                                   (dec_b, dec_t, 512))
        for lst, val in zip(outs, (kp, vp, cp, ksm, vsm, csm)):
            lst.append(val)
    return (xp, xs) + tuple(jnp.stack(o, axis=0) for o in outs)
```

```python
import functools
import math

import jax
import jax.numpy as jnp
from jax import lax
from jax.experimental import pallas as pl
from jax.experimental.pallas import tpu as pltpu

D_MODEL = 2048
N_HEADS = 8
D_QK = 64
D_V = 2 * D_QK
ATT_W = N_HEADS * D_V
CONV_W = D_MODEL // 2
CONV_K = 3
CHUNK = 64
NUM_BUCKETS = 32
MAX_DISTANCE = 128
EPS = 1e-6
MASKED = -1e30
LOG2E = math.log2(math.e)

OFF_Q, OFF_K, OFF_V, OFF_ZA = 0, 1024, 2048, 3072
OFF_BG, OFF_CG, OFF_H, OFF_ZC = 4096, 5120, 6144, 7168
OFF_GA, OFF_GC = 8192, 10240

ATT_BLOCK = 256
ATT_GROUP = 4
V7X_VMEM_LIMIT = 56 * 1024 * 1024

F32 = jnp.float32
BF16 = jnp.bfloat16


def _params(sem, vmem=V7X_VMEM_LIMIT):
    return pltpu.CompilerParams(dimension_semantics=sem, vmem_limit_bytes=vmem)


def _bucket_py(rel):
    half = NUM_BUCKETS // 2
    max_exact = half // 2
    steps = half - max_exact
    n = abs(rel)
    if n < max_exact:
        v = n
    else:
        v = max_exact + sum(n ** steps * max_exact ** j >= MAX_DISTANCE ** j * max_exact ** steps
                            for j in range(1, steps))
    return v + (half if rel > 0 else 0)


NEAR = MAX_DISTANCE
FAR_BUCKET = _bucket_py(-NEAR)
_BREAKS = tuple((rel, _bucket_py(rel)) for rel in range(-NEAR + 1, NEAR + 1)
                if _bucket_py(rel) != _bucket_py(rel - 1))


def _bias_minus_far(rel, value_of):
    val = value_of(FAR_BUCKET)
    for first_rel, bucket in _BREAKS:
        val = jnp.where(rel >= first_rel, value_of(bucket), val)
    return (val - value_of(FAR_BUCKET)) * LOG2E


_BIAS_ROWS = 32


def _pbias_kernel(rb_ref, pb_ref):
    h = pl.program_id(0)
    tb = ATT_BLOCK
    value_of = lambda b: rb_ref[b, h]
    for t in range(2):
        def rows(i, carry, t=t):
            r0 = pl.multiple_of(i * _BIAS_ROWS, _BIAS_ROWS)
            r = r0 + lax.broadcasted_iota(jnp.int32, (_BIAS_ROWS, tb), 0)
            c = lax.broadcasted_iota(jnp.int32, (_BIAS_ROWS, tb), 1)
            val = _bias_minus_far(c - r - (1 - t) * tb, value_of)
            if t == 1:
                val = jnp.where(c // CHUNK <= r // CHUNK, val, MASKED)
            pb_ref[0, t, pl.ds(r0, _BIAS_ROWS), :] = val
            return carry

        lax.fori_loop(0, tb // _BIAS_ROWS, rows, 0)


def _prompt_bias(rel_bias):
    tb = ATT_BLOCK
    return pl.pallas_call(
        _pbias_kernel,
        out_shape=jax.ShapeDtypeStruct((N_HEADS, 2, tb, tb), F32),
        grid=(N_HEADS,),
        in_specs=[pl.BlockSpec(memory_space=pltpu.SMEM)],
        out_specs=pl.BlockSpec((1, 2, tb, tb), lambda h: (h, 0, 0, 0)),
        compiler_params=_params(("parallel",)),
        name="prompt_bias",
    )(rel_bias)


def _sbias_kernel(rb_ref, lq1_ref, lk1_ref, lq2_ref, lk2_ref, sbc_ref, sbn_ref, lam_ref,
                  *, past, lam_init):
    nq = sbn_ref.shape[2]

    def tile(shape, k0, value_of):
        qpos = past + lax.broadcasted_iota(jnp.int32, shape, 0) % nq
        kpos = k0 + lax.broadcasted_iota(jnp.int32, shape, 1)
        val = _bias_minus_far(kpos - qpos, value_of)
        return jnp.where(kpos // CHUNK <= qpos // CHUNK, val, MASKED)

    for h in range(N_HEADS):
        value_of = lambda b, h=h: rb_ref[b, h]
        sbc_ref[h] = tile(sbc_ref.shape[1:], past - sbc_ref.shape[2], value_of)
        sbn_ref[h] = tile(sbn_ref.shape[1:], past, value_of)

    s1 = jnp.sum(lq1_ref[...].astype(F32) * lk1_ref[...].astype(F32), axis=-1, keepdims=True)
    s2 = jnp.sum(lq2_ref[...].astype(F32) * lk2_ref[...].astype(F32), axis=-1, keepdims=True)
    lam_ref[...] = jnp.exp(s1) - jnp.exp(s2) + lam_init


def _sample_bias_and_lambda(rel_bias, lq1, lk1, lq2, lk2, past, nq, lam_init):
    vmem = pl.BlockSpec(memory_space=pltpu.VMEM)
    return pl.pallas_call(
        functools.partial(_sbias_kernel, past=past, lam_init=lam_init),
        out_shape=(jax.ShapeDtypeStruct((N_HEADS, 2 * nq, NEAR), F32),
                   jax.ShapeDtypeStruct((N_HEADS, 2 * nq, nq), F32),
                   jax.ShapeDtypeStruct((1, 1), F32)),
        in_specs=[pl.BlockSpec(memory_space=pltpu.SMEM)] + [vmem] * 4,
        out_specs=(vmem, vmem, vmem),
        name="sample_bias_lambda",
    )(rel_bias, lq1, lk1, lq2, lk2)


def _prenorm_kernel(x_ref, g_ref, o_ref):
    xf = x_ref[...].astype(F32)
    xf = xf * lax.rsqrt(jnp.mean(xf * xf, axis=-1, keepdims=True) + EPS)
    o_ref[...] = (xf * g_ref[...]).astype(o_ref.dtype)


def _prenorm(x, g, tm):
    m, d = x.shape
    return pl.pallas_call(
        _prenorm_kernel,
        out_shape=jax.ShapeDtypeStruct((m, d), BF16),
        grid=(m // tm,),
        in_specs=[pl.BlockSpec((tm, d), lambda i: (i, 0)),
                  pl.BlockSpec((1, d), lambda i: (0, 0))],
        out_specs=pl.BlockSpec((tm, d), lambda i: (i, 0)),
        compiler_params=_params(("parallel",)),
        name="prenorm",
    )(x, g)


def _silu(x):
    return x * jax.nn.sigmoid(x)


def _proj_kernel(x_ref, w_ref, *o_refs, epilogue):
    r = jnp.dot(x_ref[...], w_ref[...], preferred_element_type=F32)
    for o_ref, o in zip(o_refs, epilogue(r)):
        o_ref[...] = o.astype(o_ref.dtype)


def _proj(xn, w, col_off, n, epilogue, out_dtypes, tm, tn, name):
    m, d = xn.shape
    jb = col_off // tn
    return pl.pallas_call(
        functools.partial(_proj_kernel, epilogue=epilogue),
        out_shape=tuple(jax.ShapeDtypeStruct((m, n), dt) for dt in out_dtypes),
        grid=(n // tn, m // tm),
        in_specs=[pl.BlockSpec((tm, d), lambda j, i: (i, 0)),
                  pl.BlockSpec((d, tn), lambda j, i: (0, jb + j))],
        out_specs=tuple(pl.BlockSpec((tm, tn), lambda j, i: (i, j)) for _ in out_dtypes),
        compiler_params=_params(("parallel", "parallel")),
        name=name,
    )(xn, w)


def _conv_kernel(x_ref, wb_ref, wc_ref, wh_ref, wz_ref, st_ref, cw_ref,
                 t_ref, nc_ref, carry_ref):
    ti = pl.program_id(2)

    @pl.when(ti == 0)
    def _():
        carry_ref[:, 6:8, :] = st_ref[...]

    tb, tt, d = x_ref.shape
    x = x_ref[...].reshape(tb * tt, d)

    def mm(w_ref):
        return jnp.dot(x, w_ref[...], preferred_element_type=F32).reshape(tb, tt, -1)

    u = mm(wc_ref) * mm(wh_ref)
    c0 = carry_ref[:, 6:7, :]
    c1 = carry_ref[:, 7:8, :]
    row = lax.broadcasted_iota(jnp.int32, u.shape, 1)
    u1 = jnp.where(row == 0, c1, pltpu.roll(u, 1, 1))
    u2 = jnp.where(row == 0, c0, jnp.where(row == 1, c1, pltpu.roll(u, 2, 1)))
    y = cw_ref[0:1, :] * u2 + cw_ref[1:2, :] * u1 + cw_ref[2:3, :] * u
    carry_ref[...] = u[:, tt - 8:, :]
    nc_ref[...] = carry_ref[:, 6:8, :]
    t_ref[...] = (mm(wb_ref) * y * _silu(mm(wz_ref))).astype(t_ref.dtype)


def _conv_branch(xn3, w, state, conv_w, tb, tt, tc):
    b, t, d = xn3.shape
    wspec = lambda off: pl.BlockSpec((d, tc), lambda c, bi, ti, o=off // tc: (0, o + c))
    return pl.pallas_call(
        _conv_kernel,
        out_shape=(jax.ShapeDtypeStruct((b, t, CONV_W), BF16),
                   jax.ShapeDtypeStruct((b, CONV_K - 1, CONV_W), F32)),
        grid=(CONV_W // tc, b // tb, t // tt),
        in_specs=[pl.BlockSpec((tb, tt, d), lambda c, bi, ti: (bi, ti, 0)),
                  wspec(OFF_BG), wspec(OFF_CG), wspec(OFF_H), wspec(OFF_ZC),
                  pl.BlockSpec((tb, CONV_K - 1, tc), lambda c, bi, ti: (bi, 0, c)),
                  pl.BlockSpec((CONV_K, tc), lambda c, bi, ti: (0, c))],
        out_specs=(pl.BlockSpec((tb, tt, tc), lambda c, bi, ti: (bi, ti, c)),
                   pl.BlockSpec((tb, CONV_K - 1, tc), lambda c, bi, ti: (bi, 0, c))),
        scratch_shapes=[pltpu.VMEM((tb, 8, tc), F32)],
        compiler_params=_params(("parallel", "parallel", "arbitrary")),
        name="conv_branch",
    )(xn3, w, w, w, w, state, conv_w)


_NT = (((1,), (1,)), ((), ()))


def _head_out(o, hn, sza, lam_init):
    o = o * lax.rsqrt(jnp.mean(o * o, axis=-1, keepdims=True) + EPS)
    return o * hn * (1.0 - lam_init) * sza


def _pattn_kernel(lam_ref, q_ref, k_ref, v_ref, bias_ref, sza_ref, hn_ref, o_ref,
                  s_ref, p_ref, *, lam_init):
    gi = pl.program_id(2)
    tb = ATT_BLOCK
    hw = tb // 2

    def attend(n, r):
        rows = slice(r * tb, (r + 1) * tb)
        q = q_ref[0, rows, :]
        lane = lax.broadcasted_iota(jnp.int32, q.shape, 1)
        zero = jnp.zeros_like(q)
        qmaps = (jnp.where(lane < D_QK, q, zero), jnp.where(lane >= D_QK, q, zero))
        outs = []
        for mi, qm in enumerate(qmaps):
            slab = 2 * r + mi
            mpart = None
            for j in range(n + 1):
                s = lax.dot_general(qm, k_ref[0, j * tb:(j + 1) * tb, :], _NT,
                                    preferred_element_type=F32)
                if j == n:
                    s = s + bias_ref[0, 1]
                elif j == n - 1:
                    s = s + bias_ref[0, 0]
                s_ref[slab, :, j * tb:(j + 1) * tb] = s
                part = jnp.maximum(s[:, :hw], s[:, hw:])
                mpart = part if mpart is None else jnp.maximum(mpart, part)
            m = jnp.broadcast_to(jnp.max(mpart, axis=-1, keepdims=True), (tb, hw))
            lpart = None
            for c in range(2 * (n + 1)):
                e = jnp.exp2(s_ref[slab, :, c * hw:(c + 1) * hw] - m)
                p_ref[slab, :, c * hw:(c + 1) * hw] = e.astype(BF16)
                lpart = e if lpart is None else lpart + e
            l = jnp.sum(lpart, axis=-1, keepdims=True)
            o = jnp.dot(p_ref[slab, :, :(n + 1) * tb], v_ref[0, :(n + 1) * tb, :],
                        preferred_element_type=F32)
            outs.append(o / l)
        o = outs[0] - lam_ref[0, 0] * outs[1]
        o_ref[0, rows, :] = _head_out(o, hn_ref[...], sza_ref[0, rows, :].astype(F32),
                                      lam_init).astype(o_ref.dtype)

    for g in range(k_ref.shape[1] // (tb * ATT_GROUP)):
        @pl.when(gi == g)
        def _(g=g):
            for r in range(ATT_GROUP):
                attend(g * ATT_GROUP + r, r)


def _prompt_attention(lam, q, kb, vb, pbias, sza, head_norm, lam_init):
    b, t, _ = q.shape
    tb = ATT_BLOCK
    tg = tb * ATT_GROUP
    assert tb >= NEAR and t % tg == 0
    blk = pl.BlockSpec((1, tg, D_V), lambda bi, h, gi: (bi, gi, h))
    full = pl.BlockSpec((1, t, D_V), lambda bi, h, gi: (bi, 0, h))
    return pl.pallas_call(
        functools.partial(_pattn_kernel, lam_init=lam_init),
        out_shape=jax.ShapeDtypeStruct((b, t, ATT_W), BF16),
        grid=(b, N_HEADS, t // tg),
        in_specs=[pl.BlockSpec(memory_space=pltpu.SMEM),
                  blk, full, full,
                  pl.BlockSpec((1, 2, tb, tb), lambda bi, h, gi: (h, 0, 0, 0)),
                  blk,
                  pl.BlockSpec((1, D_V), lambda bi, h, gi: (0, 0))],
        out_specs=blk,
        scratch_shapes=[pltpu.VMEM((2 * ATT_GROUP, tb, t), F32),
                        pltpu.VMEM((2 * ATT_GROUP, tb, t), BF16)],
        compiler_params=_params(("parallel", "parallel", "arbitrary")),
        name="prompt_attention",
    )(lam, q, kb, vb, pbias, sza, head_norm)


def _sattn_kernel(lam_ref, q_ref, kc_ref, vc_ref, kn_ref, vn_ref, bc_ref, bn_ref,
                  sza_ref, hn_ref, o_ref, *, lam_init):
    nq = q_ref.shape[1]
    past = kc_ref.shape[1] // N_HEADS
    far_n = past - bc_ref.shape[2]
    lam = lam_ref[0, 0]

    for h in range(N_HEADS):
        cols = slice(h * D_V, (h + 1) * D_V)
        q = q_ref[0, :, cols]
        lane = lax.broadcasted_iota(jnp.int32, q.shape, 1)
        zero = jnp.zeros_like(q)
        q2m = jnp.concatenate([jnp.where(lane < D_QK, q, zero), jnp.where(lane >= D_QK, q, zero)], axis=0)
        kc = kc_ref[0, pl.ds(h, past, stride=N_HEADS), :].astype(BF16)
        s = lax.dot_general(q2m, kc, _NT, preferred_element_type=F32)
        s = jnp.concatenate([s[:, :far_n], s[:, far_n:] + bc_ref[h]], axis=1)
        sn = lax.dot_general(q2m, kn_ref[0, :, cols], _NT, preferred_element_type=F32) + bn_ref[h]
        mx = jnp.maximum(jnp.max(s, axis=-1, keepdims=True), jnp.max(sn, axis=-1, keepdims=True))
        e = jnp.exp2(s - mx)
        en = jnp.exp2(sn - mx)
        denom = jnp.sum(e, axis=-1, keepdims=True) + jnp.sum(en, axis=-1, keepdims=True)
        row = lax.broadcasted_iota(jnp.int32, denom.shape, 0)
        w = jnp.where(row < nq, 1.0, -lam) / denom

        def combine(x, w=w):
            p = x * w
            return (p[:nq] + p[nq:]).astype(BF16)

        vc = vc_ref[0, pl.ds(h, past, stride=N_HEADS), :].astype(BF16)
        o = (jnp.dot(combine(e), vc, preferred_element_type=F32)
             + jnp.dot(combine(en), vn_ref[0, :, cols], preferred_element_type=F32))
        o_ref[0, :, cols] = _head_out(o, hn_ref[...], sza_ref[0, :, cols].astype(F32),
                                      lam_init).astype(o_ref.dtype)


def _sample_attention(lam, q, cache_k, cache_v, kb, vb, sbc, sbn, sza, head_norm, lam_init, layer):
    b, nq, _ = q.shape
    rows = cache_k.shape[1]
    new = pl.BlockSpec((1, nq, ATT_W), lambda bi: (bi, 0, 0))
    cache = pl.BlockSpec((1, rows, D_V), lambda bi: (layer * b + bi, 0, 0))
    whole = lambda a: pl.BlockSpec(a.shape, lambda bi: (0,) * a.ndim)
    return pl.pallas_call(
        functools.partial(_sattn_kernel, lam_init=lam_init),
        out_shape=jax.ShapeDtypeStruct((b, nq, ATT_W), BF16),
        grid=(b,),
        in_specs=[pl.BlockSpec(memory_space=pltpu.SMEM),
                  new, cache, cache, new, new, whole(sbc), whole(sbn), new, whole(head_norm)],
        out_specs=new,
        compiler_params=_params(("parallel",)),
        name="sample_attention",
    )(lam, q, cache_k, cache_v, kb, vb, sbc, sbn, sza, head_norm)


def _out_kernel(og_ref, t_ref, sga_ref, sgc_ref, wpa_ref, wpc_ref, wo_ref, x_ref, g_ref,
                o_ref, m_ref):
    c = pl.program_id(1)
    nchunk, _, tc = m_ref.shape
    ya = jnp.dot(og_ref[...], wpa_ref[...], preferred_element_type=F32)
    yc = jnp.dot(t_ref[...], wpc_ref[...], preferred_element_type=F32)
    merged = sga_ref[...].astype(F32) * ya + sgc_ref[...].astype(F32) * yc
    m_ref[c] = merged.astype(BF16)

    @pl.when(c == nchunk - 1)
    def _():
        y = jnp.dot(m_ref[0], wo_ref[0:tc, :], preferred_element_type=F32)
        for cc in range(1, nchunk):
            y = y + jnp.dot(m_ref[cc], wo_ref[cc * tc:(cc + 1) * tc, :], preferred_element_type=F32)
        y = y * lax.rsqrt(jnp.mean(y * y, axis=-1, keepdims=True) + EPS)
        o_ref[...] = x_ref[...] + y * g_ref[...]


def _merge_out(og, t, sga, sgc, wpa, wpc, wo, x, g, tm, tc):
    m, d = x.shape
    return pl.pallas_call(
        _out_kernel,
        out_shape=jax.ShapeDtypeStruct((m, d), x.dtype),
        grid=(m // tm, d // tc),
        in_specs=[pl.BlockSpec((tm, ATT_W), lambda i, c: (i, 0)),
                  pl.BlockSpec((tm, CONV_W), lambda i, c: (i, 0)),
                  pl.BlockSpec((tm, tc), lambda i, c: (i, c)),
                  pl.BlockSpec((tm, tc), lambda i, c: (i, c)),
                  pl.BlockSpec((ATT_W, tc), lambda i, c: (0, c)),
                  pl.BlockSpec((CONV_W, tc), lambda i, c: (0, c)),
                  pl.BlockSpec((d, d), lambda i, c: (0, 0)),
                  pl.BlockSpec((tm, d), lambda i, c: (i, 0)),
                  pl.BlockSpec((1, d), lambda i, c: (0, 0))],
        out_specs=pl.BlockSpec((tm, d), lambda i, c: (i, 0)),
        scratch_shapes=[pltpu.VMEM((d // tc, tm, tc), BF16)],
        compiler_params=_params(("parallel", "arbitrary")),
        name="merge_out",
    )(og, t, sga, sgc, wpa, wpc, wo, x, g)


def _layer(x, conv_state, weights, attend, tm, conv_tiles):
    (norm_pre, norm_post, w_in, conv_w, wpa, wpc, wo) = weights
    b, t, d = x.shape
    m = b * t
    x2 = x.reshape(m, d)
    xn = _prenorm(x2, norm_pre.reshape(1, d), tm)
    tn = 1024
    tmp = min(m, 1024)
    scale = LOG2E / math.sqrt(D_QK)
    (q,) = _proj(xn, w_in, OFF_Q, ATT_W, lambda r: (r * scale,), (BF16,), tmp, tn, "proj_q")
    k, kb = _proj(xn, w_in, OFF_K, ATT_W, lambda r: (r, r), (F32, BF16), tmp, tn, "proj_k")
    v, vb = _proj(xn, w_in, OFF_V, ATT_W, lambda r: (r, r), (F32, BF16), tmp, tn, "proj_v")
    (sza,) = _proj(xn, w_in, OFF_ZA, ATT_W, lambda r: (_silu(r),), (BF16,), tmp, tn, "proj_za")
    (sga,) = _proj(xn, w_in, OFF_GA, D_MODEL, lambda r: (jax.nn.sigmoid(r),), (BF16,), tmp, tn, "proj_ga")
    (sgc,) = _proj(xn, w_in, OFF_GC, D_MODEL, lambda r: (jax.nn.sigmoid(r),), (BF16,), tmp, tn, "proj_gc")
    tcv, new_conv = _conv_branch(xn.reshape(b, t, d), w_in, conv_state, conv_w, *conv_tiles)
    to3 = lambda a: a.reshape(b, t, ATT_W)
    og = attend(to3(q), to3(kb), to3(vb), to3(sza))
    y = _merge_out(og.reshape(m, ATT_W), tcv.reshape(m, CONV_W), sga, sgc, wpa, wpc, wo,
                   x2, norm_post.reshape(1, d), tm, 512)
    return (y.reshape(b, t, d), k.reshape(b, t, N_HEADS, 2 * D_QK), v.reshape(b, t, N_HEADS, D_V),
            new_conv)


def kernel(x_prompt, x_sample, cache_k, cache_v, state_conv, norm_pre, norm_post, w_in,
           lambda_q1, lambda_k1, lambda_q2, lambda_k2, head_norm, conv_w, w_proj_attn,
           w_proj_conv, w_out, rel_bias):
    depth = w_in.shape[0]
    past = cache_k.shape[2]
    dec_b, dec_t = x_sample.shape[0], x_sample.shape[1]
    xp, xs = x_prompt, x_sample
    zero_conv = jnp.zeros((xp.shape[0], CONV_K - 1, CONV_W), xp.dtype)
    pbias = _prompt_bias(rel_bias)
    outs = [[] for _ in range(6)]
    for l in range(depth):
        lam_init = 0.8 - 0.6 * math.exp(-0.3 * l)
        sbc, sbn, lam = _sample_bias_and_lambda(
            rel_bias, lambda_q1[l][None], lambda_k1[l][None], lambda_q2[l][None], lambda_k2[l][None],
            past, dec_t, lam_init)
        hn = head_norm[l].reshape(1, D_V)
        weights = (norm_pre[l], norm_post[l], w_in[l].astype(BF16), conv_w[l],
                   w_proj_attn[l].astype(BF16), w_proj_conv[l].astype(BF16), w_out[l].astype(BF16))

        def attend_prompt(q, kb, vb, sza):
            return _prompt_attention(lam, q, kb, vb, pbias, sza, hn, lam_init)

        ck = cache_k.reshape(depth * dec_b, past * N_HEADS, D_V)
        cv = cache_v.reshape(depth * dec_b, past * N_HEADS, D_V)

        def attend_sample(q, kb, vb, sza, l=l):
            return _sample_attention(lam, q, ck, cv, kb, vb, sbc, sbn, sza, hn, lam_init, l)

        xp, kp, vp, cp = _layer(xp, zero_conv, weights, attend_prompt, 512, (1, 512, 512))
        xs, ksm, vsm, csm = _layer(xs, state_conv[l], weights, attend_sample, 512,
                                   (dec_b, dec_t, 512))
        for lst, val in zip(outs, (kp, vp, cp, ksm, vsm, csm)):
            lst.append(val)
    return (xp, xs) + tuple(jnp.stack(o, axis=0) for o in outs)
```

```python
import functools
import math

import jax
import jax.numpy as jnp
from jax import lax
from jax.experimental import pallas as pl
from jax.experimental.pallas import tpu as pltpu

D_MODEL = 2048
N_HEADS = 8
D_QK = 64
D_V = 2 * D_QK
ATT_W = N_HEADS * D_V
CONV_W = D_MODEL // 2
CONV_K = 3
CHUNK = 64
NUM_BUCKETS = 32
MAX_DISTANCE = 128
EPS = 1e-6
MASKED = -1e30
LOG2E = math.log2(math.e)

OFF_Q, OFF_K, OFF_V, OFF_ZA = 0, 1024, 2048, 3072
OFF_BG, OFF_CG, OFF_H, OFF_ZC = 4096, 5120, 6144, 7168
OFF_GA, OFF_GC = 8192, 10240

ATT_BLOCK = 256
ATT_GROUP = 8
ATT_SLABS = 4
ATT_STRIP = 64
V7X_VMEM_LIMIT = 56 * 1024 * 1024

F32 = jnp.float32
BF16 = jnp.bfloat16


def _params(sem, vmem=V7X_VMEM_LIMIT):
    return pltpu.CompilerParams(dimension_semantics=sem, vmem_limit_bytes=vmem)


def _bucket_py(rel):
    half = NUM_BUCKETS // 2
    max_exact = half // 2
    steps = half - max_exact
    n = abs(rel)
    if n < max_exact:
        v = n
    else:
        v = max_exact + sum(n ** steps * max_exact ** j >= MAX_DISTANCE ** j * max_exact ** steps
                            for j in range(1, steps))
    return v + (half if rel > 0 else 0)


NEAR = MAX_DISTANCE
FAR_BUCKET = _bucket_py(-NEAR)
_BREAKS = tuple((rel, _bucket_py(rel)) for rel in range(-NEAR + 1, NEAR + 1)
                if _bucket_py(rel) != _bucket_py(rel - 1))


def _bias_minus_far(rel, value_of):
    val = value_of(FAR_BUCKET)
    for first_rel, bucket in _BREAKS:
        val = jnp.where(rel >= first_rel, value_of(bucket), val)
    return (val - value_of(FAR_BUCKET)) * LOG2E


_BIAS_ROWS = 32


def _pbias_kernel(rb_ref, pb_ref):
    h = pl.program_id(0)
    tb = ATT_BLOCK
    value_of = lambda b: rb_ref[b, h]
    for t in range(2):
        def rows(i, carry, t=t):
            r0 = pl.multiple_of(i * _BIAS_ROWS, _BIAS_ROWS)
            r = r0 + lax.broadcasted_iota(jnp.int32, (_BIAS_ROWS, tb), 0)
            c = lax.broadcasted_iota(jnp.int32, (_BIAS_ROWS, tb), 1)
            val = _bias_minus_far(c - r - (1 - t) * tb, value_of)
            if t == 1:
                val = jnp.where(c // CHUNK <= r // CHUNK, val, MASKED)
            pb_ref[0, t, pl.ds(r0, _BIAS_ROWS), :] = val
            return carry

        lax.fori_loop(0, tb // _BIAS_ROWS, rows, 0)


def _prompt_bias(rel_bias):
    tb = ATT_BLOCK
    return pl.pallas_call(
        _pbias_kernel,
        out_shape=jax.ShapeDtypeStruct((N_HEADS, 2, tb, tb), F32),
        grid=(N_HEADS,),
        in_specs=[pl.BlockSpec(memory_space=pltpu.SMEM)],
        out_specs=pl.BlockSpec((1, 2, tb, tb), lambda h: (h, 0, 0, 0)),
        compiler_params=_params(("parallel",)),
        name="prompt_bias",
    )(rel_bias)


def _sbias_kernel(rb_ref, lq1_ref, lk1_ref, lq2_ref, lk2_ref, sbc_ref, sbn_ref, lam_ref,
                  *, past, lam_init):
    nq = sbn_ref.shape[2]

    def tile(shape, k0, value_of):
        qpos = past + lax.broadcasted_iota(jnp.int32, shape, 0) % nq
        kpos = k0 + lax.broadcasted_iota(jnp.int32, shape, 1)
        val = _bias_minus_far(kpos - qpos, value_of)
        return jnp.where(kpos // CHUNK <= qpos // CHUNK, val, MASKED)

    for h in range(N_HEADS):
        value_of = lambda b, h=h: rb_ref[b, h]
        sbc_ref[h] = tile(sbc_ref.shape[1:], past - sbc_ref.shape[2], value_of)
        sbn_ref[h] = tile(sbn_ref.shape[1:], past, value_of)

    s1 = jnp.sum(lq1_ref[...].astype(F32) * lk1_ref[...].astype(F32), axis=-1, keepdims=True)
    s2 = jnp.sum(lq2_ref[...].astype(F32) * lk2_ref[...].astype(F32), axis=-1, keepdims=True)
    lam_ref[...] = jnp.exp(s1) - jnp.exp(s2) + lam_init


def _sample_bias_and_lambda(rel_bias, lq1, lk1, lq2, lk2, past, nq, lam_init):
    vmem = pl.BlockSpec(memory_space=pltpu.VMEM)
    return pl.pallas_call(
        functools.partial(_sbias_kernel, past=past, lam_init=lam_init),
        out_shape=(jax.ShapeDtypeStruct((N_HEADS, 2 * nq, NEAR), F32),
                   jax.ShapeDtypeStruct((N_HEADS, 2 * nq, nq), F32),
                   jax.ShapeDtypeStruct((1, 1), F32)),
        in_specs=[pl.BlockSpec(memory_space=pltpu.SMEM)] + [vmem] * 4,
        out_specs=(vmem, vmem, vmem),
        name="sample_bias_lambda",
    )(rel_bias, lq1, lk1, lq2, lk2)


def _prenorm_kernel(x_ref, g_ref, o_ref):
    xf = x_ref[...].astype(F32)
    xf = xf * lax.rsqrt(jnp.mean(xf * xf, axis=-1, keepdims=True) + EPS)
    o_ref[...] = (xf * g_ref[...]).astype(o_ref.dtype)


def _prenorm(x, g, tm):
    m, d = x.shape
    return pl.pallas_call(
        _prenorm_kernel,
        out_shape=jax.ShapeDtypeStruct((m, d), BF16),
        grid=(m // tm,),
        in_specs=[pl.BlockSpec((tm, d), lambda i: (i, 0)),
                  pl.BlockSpec((1, d), lambda i: (0, 0))],
        out_specs=pl.BlockSpec((tm, d), lambda i: (i, 0)),
        compiler_params=_params(("parallel",)),
        name="prenorm",
    )(x, g)


def _silu(x):
    return x * jax.nn.sigmoid(x)


def _proj_kernel(x_ref, w_ref, *o_refs, epilogue):
    r = jnp.dot(x_ref[...], w_ref[...], preferred_element_type=F32)
    for o_ref, o in zip(o_refs, epilogue(r)):
        o_ref[...] = o.astype(o_ref.dtype)


def _proj(xn, w, col_off, n, epilogue, out_dtypes, tm, tn, name):
    m, d = xn.shape
    jb = col_off // tn
    return pl.pallas_call(
        functools.partial(_proj_kernel, epilogue=epilogue),
        out_shape=tuple(jax.ShapeDtypeStruct((m, n), dt) for dt in out_dtypes),
        grid=(n // tn, m // tm),
        in_specs=[pl.BlockSpec((tm, d), lambda j, i: (i, 0)),
                  pl.BlockSpec((d, tn), lambda j, i: (0, jb + j))],
        out_specs=tuple(pl.BlockSpec((tm, tn), lambda j, i: (i, j)) for _ in out_dtypes),
        compiler_params=_params(("parallel", "parallel")),
        name=name,
    )(xn, w)


def _conv_kernel(x_ref, wb_ref, wc_ref, wh_ref, wz_ref, st_ref, cw_ref,
                 t_ref, nc_ref, carry_ref):
    ti = pl.program_id(2)

    @pl.when(ti == 0)
    def _():
        carry_ref[:, 6:8, :] = st_ref[...]

    tb, tt, d = x_ref.shape
    x = x_ref[...].reshape(tb * tt, d)

    def mm(w_ref):
        return jnp.dot(x, w_ref[...], preferred_element_type=F32).reshape(tb, tt, -1)

    u = mm(wc_ref) * mm(wh_ref)
    c0 = carry_ref[:, 6:7, :]
    c1 = carry_ref[:, 7:8, :]
    row = lax.broadcasted_iota(jnp.int32, u.shape, 1)
    u1 = jnp.where(row == 0, c1, pltpu.roll(u, 1, 1))
    u2 = jnp.where(row == 0, c0, jnp.where(row == 1, c1, pltpu.roll(u, 2, 1)))
    y = cw_ref[0:1, :] * u2 + cw_ref[1:2, :] * u1 + cw_ref[2:3, :] * u
    carry_ref[...] = u[:, tt - 8:, :]
    nc_ref[...] = carry_ref[:, 6:8, :]
    t_ref[...] = (mm(wb_ref) * y * _silu(mm(wz_ref))).astype(t_ref.dtype)


def _conv_branch(xn3, w, state, conv_w, tb, tt, tc):
    b, t, d = xn3.shape
    wspec = lambda off: pl.BlockSpec((d, tc), lambda c, bi, ti, o=off // tc: (0, o + c))
    return pl.pallas_call(
        _conv_kernel,
        out_shape=(jax.ShapeDtypeStruct((b, t, CONV_W), BF16),
                   jax.ShapeDtypeStruct((b, CONV_K - 1, CONV_W), F32)),
        grid=(CONV_W // tc, b // tb, t // tt),
        in_specs=[pl.BlockSpec((tb, tt, d), lambda c, bi, ti: (bi, ti, 0)),
                  wspec(OFF_BG), wspec(OFF_CG), wspec(OFF_H), wspec(OFF_ZC),
                  pl.BlockSpec((tb, CONV_K - 1, tc), lambda c, bi, ti: (bi, 0, c)),
                  pl.BlockSpec((CONV_K, tc), lambda c, bi, ti: (0, c))],
        out_specs=(pl.BlockSpec((tb, tt, tc), lambda c, bi, ti: (bi, ti, c)),
                   pl.BlockSpec((tb, CONV_K - 1, tc), lambda c, bi, ti: (bi, 0, c))),
        scratch_shapes=[pltpu.VMEM((tb, 8, tc), F32)],
        compiler_params=_params(("parallel", "parallel", "arbitrary")),
        name="conv_branch",
    )(xn3, w, w, w, w, state, conv_w)


_NT = (((1,), (1,)), ((), ()))


def _head_out(o, hn, sza, lam_init):
    o = o * lax.rsqrt(jnp.mean(o * o, axis=-1, keepdims=True) + EPS)
    return o * hn * (1.0 - lam_init) * sza


def _pattn_kernel(lam_ref, q_ref, k_ref, v_ref, bias_ref, sza_ref, hn_ref, o_ref,
                  s_ref, p_ref, *, lam_init):
    gi = pl.program_id(2)
    tb = ATT_BLOCK
    hw = tb // 2

    def block_tasks(n, r):
        rows = slice(r * tb, (r + 1) * tb)
        q = q_ref[0, rows, :]
        lane = lax.broadcasted_iota(jnp.int32, q.shape, 1)
        zero = jnp.zeros_like(q)
        qmaps = (jnp.where(lane < D_QK, q, zero), jnp.where(lane >= D_QK, q, zero))
        chunks = [slice(c * hw, (c + 1) * hw) for c in range(2 * (n + 1))]
        sums = [[], []]
        outs = [None, None]
        slab_of = lambda mi: (2 * r + mi) % ATT_SLABS

        def scores(mi):
            for j in range(n + 1):
                s = lax.dot_general(qmaps[mi], k_ref[0, j * tb:(j + 1) * tb, :], _NT,
                                    preferred_element_type=F32)
                if j == n:
                    s = s + bias_ref[0, 1]
                elif j == n - 1:
                    s = s + bias_ref[0, 0]
                s_ref[slab_of(mi), :, j * tb:(j + 1) * tb] = s
                yield

        def numerator(mi):
            slab = slab_of(mi)
            for st in range(tb // ATT_STRIP):
                srows = slice(st * ATT_STRIP, (st + 1) * ATT_STRIP)
                mpart = functools.reduce(jnp.maximum, [s_ref[slab, srows, c] for c in chunks])
                m = jnp.broadcast_to(jnp.max(mpart, axis=-1, keepdims=True), (ATT_STRIP, hw))
                yield
                lpart = None
                for c in chunks:
                    e = jnp.exp2(s_ref[slab, srows, c] - m)
                    p_ref[slab, srows, c] = e.astype(BF16)
                    lpart = e if lpart is None else lpart + e
                    yield
                sums[mi].append(jnp.sum(lpart, axis=-1, keepdims=True))

        def values(mi):
            o = jnp.dot(p_ref[slab_of(mi), :, :(n + 1) * tb], v_ref[0, :(n + 1) * tb, :],
                        preferred_element_type=F32)
            outs[mi] = o / jnp.concatenate(sums[mi], axis=0)
            if mi == 1:
                o = outs[0] - lam_ref[0, 0] * outs[1]
                o_ref[0, rows, :] = _head_out(o, hn_ref[...], sza_ref[0, rows, :].astype(F32),
                                              lam_init).astype(o_ref.dtype)
            yield

        return [(scores(mi), n + 1, numerator(mi), (tb // ATT_STRIP) * (1 + len(chunks)), values(mi))
                for mi in range(2)]

    def emit_group(g):
        units = [u for r in range(ATT_GROUP) for u in block_tasks(g * ATT_GROUP + r, r)]
        for t in range(len(units) + 2):
            if t >= 2:
                next(units[t - 2][4])
            a, na = (units[t][0], units[t][1]) if t < len(units) else (iter(()), 0)
            b, nb = (units[t - 1][2], units[t - 1][3]) if 1 <= t <= len(units) else (iter(()), 0)
            da = db = 0
            while da < na or db < nb:
                if db >= nb or (da < na and da * nb <= db * na):
                    next(a)
                    da += 1
                else:
                    next(b)
                    db += 1
            for rest in (a, b):
                for _ in rest:
                    pass

    for g in range(k_ref.shape[1] // (tb * ATT_GROUP)):
        pl.when(gi == g)(functools.partial(emit_group, g))


def _prompt_attention(lam, q, kb, vb, pbias, sza, head_norm, lam_init):
    b, t, _ = q.shape
    tb = ATT_BLOCK
    tg = tb * ATT_GROUP
    assert tb >= NEAR and t % tg == 0
    blk = pl.BlockSpec((1, tg, D_V), lambda bi, h, gi: (bi, gi, h))
    full = pl.BlockSpec((1, t, D_V), lambda bi, h, gi: (bi, 0, h))
    return pl.pallas_call(
        functools.partial(_pattn_kernel, lam_init=lam_init),
        out_shape=jax.ShapeDtypeStruct((b, t, ATT_W), BF16),
        grid=(b, N_HEADS, t // tg),
        in_specs=[pl.BlockSpec(memory_space=pltpu.SMEM),
                  blk, full, full,
                  pl.BlockSpec((1, 2, tb, tb), lambda bi, h, gi: (h, 0, 0, 0)),
                  blk,
                  pl.BlockSpec((1, D_V), lambda bi, h, gi: (0, 0))],
        out_specs=blk,
        scratch_shapes=[pltpu.VMEM((ATT_SLABS, tb, t), F32),
                        pltpu.VMEM((ATT_SLABS, tb, t), BF16)],
        compiler_params=_params(("parallel", "parallel", "arbitrary")),
        name="prompt_attention",
    )(lam, q, kb, vb, pbias, sza, head_norm)


def _sattn_kernel(lam_ref, q_ref, kc_ref, vc_ref, kn_ref, vn_ref, bc_ref, bn_ref,
                  sza_ref, hn_ref, o_ref, *, lam_init):
    nq = q_ref.shape[1]
    past = kc_ref.shape[1] // N_HEADS
    far_n = past - bc_ref.shape[2]
    lam = lam_ref[0, 0]

    for h in range(N_HEADS):
        cols = slice(h * D_V, (h + 1) * D_V)
        q = q_ref[0, :, cols]
        lane = lax.broadcasted_iota(jnp.int32, q.shape, 1)
        zero = jnp.zeros_like(q)
        q2m = jnp.concatenate([jnp.where(lane < D_QK, q, zero), jnp.where(lane >= D_QK, q, zero)], axis=0)
        kc = kc_ref[0, pl.ds(h, past, stride=N_HEADS), :].astype(BF16)
        s = lax.dot_general(q2m, kc, _NT, preferred_element_type=F32)
        s = jnp.concatenate([s[:, :far_n], s[:, far_n:] + bc_ref[h]], axis=1)
        sn = lax.dot_general(q2m, kn_ref[0, :, cols], _NT, preferred_element_type=F32) + bn_ref[h]
        mx = jnp.maximum(jnp.max(s, axis=-1, keepdims=True), jnp.max(sn, axis=-1, keepdims=True))
        e = jnp.exp2(s - mx)
        en = jnp.exp2(sn - mx)
        denom = jnp.sum(e, axis=-1, keepdims=True) + jnp.sum(en, axis=-1, keepdims=True)
        row = lax.broadcasted_iota(jnp.int32, denom.shape, 0)
        w = jnp.where(row < nq, 1.0, -lam) / denom

        def combine(x, w=w):
            p = x * w
            return (p[:nq] + p[nq:]).astype(BF16)

        vc = vc_ref[0, pl.ds(h, past, stride=N_HEADS), :].astype(BF16)
        o = (jnp.dot(combine(e), vc, preferred_element_type=F32)
             + jnp.dot(combine(en), vn_ref[0, :, cols], preferred_element_type=F32))
        o_ref[0, :, cols] = _head_out(o, hn_ref[...], sza_ref[0, :, cols].astype(F32),
                                      lam_init).astype(o_ref.dtype)


def _sample_attention(lam, q, cache_k, cache_v, kb, vb, sbc, sbn, sza, head_norm, lam_init, layer):
    b, nq, _ = q.shape
    rows = cache_k.shape[1]
    new = pl.BlockSpec((1, nq, ATT_W), lambda bi: (bi, 0, 0))
    cache = pl.BlockSpec((1, rows, D_V), lambda bi: (layer * b + bi, 0, 0))
    whole = lambda a: pl.BlockSpec(a.shape, lambda bi: (0,) * a.ndim)
    return pl.pallas_call(
        functools.partial(_sattn_kernel, lam_init=lam_init),
        out_shape=jax.ShapeDtypeStruct((b, nq, ATT_W), BF16),
        grid=(b,),
        in_specs=[pl.BlockSpec(memory_space=pltpu.SMEM),
                  new, cache, cache, new, new, whole(sbc), whole(sbn), new, whole(head_norm)],
        out_specs=new,
        compiler_params=_params(("parallel",)),
        name="sample_attention",
    )(lam, q, cache_k, cache_v, kb, vb, sbc, sbn, sza, head_norm)


def _out_kernel(og_ref, t_ref, sga_ref, sgc_ref, wpa_ref, wpc_ref, wo_ref, x_ref, g_ref,
                o_ref, m_ref):
    c = pl.program_id(1)
    nchunk, _, tc = m_ref.shape
    ya = jnp.dot(og_ref[...], wpa_ref[...], preferred_element_type=F32)
    yc = jnp.dot(t_ref[...], wpc_ref[...], preferred_element_type=F32)
    merged = sga_ref[...].astype(F32) * ya + sgc_ref[...].astype(F32) * yc
    m_ref[c] = merged.astype(BF16)

    @pl.when(c == nchunk - 1)
    def _():
        y = jnp.dot(m_ref[0], wo_ref[0:tc, :], preferred_element_type=F32)
        for cc in range(1, nchunk):
            y = y + jnp.dot(m_ref[cc], wo_ref[cc * tc:(cc + 1) * tc, :], preferred_element_type=F32)
        y = y * lax.rsqrt(jnp.mean(y * y, axis=-1, keepdims=True) + EPS)
        o_ref[...] = x_ref[...] + y * g_ref[...]


def _merge_out(og, t, sga, sgc, wpa, wpc, wo, x, g, tm, tc):
    m, d = x.shape
    return pl.pallas_call(
        _out_kernel,
        out_shape=jax.ShapeDtypeStruct((m, d), x.dtype),
        grid=(m // tm, d // tc),
        in_specs=[pl.BlockSpec((tm, ATT_W), lambda i, c: (i, 0)),
                  pl.BlockSpec((tm, CONV_W), lambda i, c: (i, 0)),
                  pl.BlockSpec((tm, tc), lambda i, c: (i, c)),
                  pl.BlockSpec((tm, tc), lambda i, c: (i, c)),
                  pl.BlockSpec((ATT_W, tc), lambda i, c: (0, c)),
                  pl.BlockSpec((CONV_W, tc), lambda i, c: (0, c)),
                  pl.BlockSpec((d, d), lambda i, c: (0, 0)),
                  pl.BlockSpec((tm, d), lambda i, c: (i, 0)),
                  pl.BlockSpec((1, d), lambda i, c: (0, 0))],
        out_specs=pl.BlockSpec((tm, d), lambda i, c: (i, 0)),
        scratch_shapes=[pltpu.VMEM((d // tc, tm, tc), BF16)],
        compiler_params=_params(("parallel", "arbitrary")),
        name="merge_out",
    )(og, t, sga, sgc, wpa, wpc, wo, x, g)


def _layer(x, conv_state, weights, attend, tm, conv_tiles):
    (norm_pre, norm_post, w_in, conv_w, wpa, wpc, wo) = weights
    b, t, d = x.shape
    m = b * t
    x2 = x.reshape(m, d)
    xn = _prenorm(x2, norm_pre.reshape(1, d), tm)
    tn = 1024
    tmp = min(m, 1024)
    scale = LOG2E / math.sqrt(D_QK)
    (q,) = _proj(xn, w_in, OFF_Q, ATT_W, lambda r: (r * scale,), (BF16,), tmp, tn, "proj_q")
    k, kb = _proj(xn, w_in, OFF_K, ATT_W, lambda r: (r, r), (F32, BF16), tmp, tn, "proj_k")
    v, vb = _proj(xn, w_in, OFF_V, ATT_W, lambda r: (r, r), (F32, BF16), tmp, tn, "proj_v")
    (sza,) = _proj(xn, w_in, OFF_ZA, ATT_W, lambda r: (_silu(r),), (BF16,), tmp, tn, "proj_za")
    (sga,) = _proj(xn, w_in, OFF_GA, D_MODEL, lambda r: (jax.nn.sigmoid(r),), (BF16,), tmp, tn, "proj_ga")
    (sgc,) = _proj(xn, w_in, OFF_GC, D_MODEL, lambda r: (jax.nn.sigmoid(r),), (BF16,), tmp, tn, "proj_gc")
    tcv, new_conv = _conv_branch(xn.reshape(b, t, d), w_in, conv_state, conv_w, *conv_tiles)
    to3 = lambda a: a.reshape(b, t, ATT_W)
    og = attend(to3(q), to3(kb), to3(vb), to3(sza))
    y = _merge_out(og.reshape(m, ATT_W), tcv.reshape(m, CONV_W), sga, sgc, wpa, wpc, wo,
                   x2, norm_post.reshape(1, d), tm, 512)
    return (y.reshape(b, t, d), k.reshape(b, t, N_HEADS, 2 * D_QK), v.reshape(b, t, N_HEADS, D_V),
            new_conv)


def kernel(x_prompt, x_sample, cache_k, cache_v, state_conv, norm_pre, norm_post, w_in,
           lambda_q1, lambda_k1, lambda_q2, lambda_k2, head_norm, conv_w, w_proj_attn,
           w_proj_conv, w_out, rel_bias):
    depth = w_in.shape[0]
    past = cache_k.shape[2]
    dec_b, dec_t = x_sample.shape[0], x_sample.shape[1]
    xp, xs = x_prompt, x_sample
    zero_conv = jnp.zeros((xp.shape[0], CONV_K - 1, CONV_W), xp.dtype)
    pbias = _prompt_bias(rel_bias)
    outs = [[] for _ in range(6)]
    for l in range(depth):
        lam_init = 0.8 - 0.6 * math.exp(-0.3 * l)
        sbc, sbn, lam = _sample_bias_and_lambda(
            rel_bias, lambda_q1[l][None], lambda_k1[l][None], lambda_q2[l][None], lambda_k2[l][None],
            past, dec_t, lam_init)
        hn = head_norm[l].reshape(1, D_V)
        weights = (norm_pre[l], norm_post[l], w_in[l].astype(BF16), conv_w[l],
                   w_proj_attn[l].astype(BF16), w_proj_conv[l].astype(BF16), w_out[l].astype(BF16))

        def attend_prompt(q, kb, vb, sza):
            return _prompt_attention(lam, q, kb, vb, pbias, sza, hn, lam_init)

        ck = cache_k.reshape(depth * dec_b, past * N_HEADS, D_V)
        cv = cache_v.reshape(depth * dec_b, past * N_HEADS, D_V)

        def attend_sample(q, kb, vb, sza, l=l):
            return _sample_attention(lam, q, ck, cv, kb, vb, sbc, sbn, sza, hn, lam_init, l)

        xp, kp, vp, cp = _layer(xp, zero_conv, weights, attend_prompt, 512, (1, 512, 512))
        xs, ksm, vsm, csm = _layer(xs, state_conv[l], weights, attend_sample, 512,
                                   (dec_b, dec_t, 512))
        for lst, val in zip(outs, (kp, vp, cp, ksm, vsm, csm)):
            lst.append(val)
    return (xp, xs) + tuple(jnp.stack(o, axis=0) for o in outs)
```

```python
import functools
import math

import jax
import jax.numpy as jnp
from jax import lax
from jax.experimental import pallas as pl
from jax.experimental.pallas import tpu as pltpu

D_MODEL = 2048
N_HEADS = 8
D_QK = 64
D_V = 2 * D_QK
ATT_W = N_HEADS * D_V
CONV_W = D_MODEL // 2
CONV_K = 3
CHUNK = 64
NUM_BUCKETS = 32
MAX_DISTANCE = 128
EPS = 1e-6
MASKED = -1e30
LOG2E = math.log2(math.e)

OFF_Q, OFF_K, OFF_V, OFF_ZA = 0, 1024, 2048, 3072
OFF_BG, OFF_CG, OFF_H, OFF_ZC = 4096, 5120, 6144, 7168
OFF_GA, OFF_GC = 8192, 10240

ATT_BLOCK = 256
ATT_GROUP = 8
ATT_SLABS = 4
ATT_STRIP = 64
V7X_VMEM_LIMIT = 56 * 1024 * 1024

F32 = jnp.float32
BF16 = jnp.bfloat16


def _params(sem, vmem=V7X_VMEM_LIMIT):
    return pltpu.CompilerParams(dimension_semantics=sem, vmem_limit_bytes=vmem)


def _bucket_py(rel):
    half = NUM_BUCKETS // 2
    max_exact = half // 2
    steps = half - max_exact
    n = abs(rel)
    if n < max_exact:
        v = n
    else:
        v = max_exact + sum(n ** steps * max_exact ** j >= MAX_DISTANCE ** j * max_exact ** steps
                            for j in range(1, steps))
    return v + (half if rel > 0 else 0)


NEAR = MAX_DISTANCE
FAR_BUCKET = _bucket_py(-NEAR)
_BREAKS = tuple((rel, _bucket_py(rel)) for rel in range(-NEAR + 1, NEAR + 1)
                if _bucket_py(rel) != _bucket_py(rel - 1))


def _bias_minus_far(rel, value_of, max_rel):
    val = value_of(FAR_BUCKET)
    for first_rel, bucket in _BREAKS:
        if first_rel > max_rel:
            break
        val = jnp.where(rel >= first_rel, value_of(bucket), val)
    return (val - value_of(FAR_BUCKET)) * LOG2E


_BIAS_ROWS = 32


def _pbias_kernel(rb_ref, pb_ref):
    h = pl.program_id(0)
    tb = ATT_BLOCK
    value_of = lambda b: rb_ref[b, h]
    for t in range(2):
        def rows(i, carry, t=t):
            r0 = pl.multiple_of(i * _BIAS_ROWS, _BIAS_ROWS)
            r = r0 + lax.broadcasted_iota(jnp.int32, (_BIAS_ROWS, tb), 0)
            c = lax.broadcasted_iota(jnp.int32, (_BIAS_ROWS, tb), 1)
            val = _bias_minus_far(c - r - (1 - t) * tb, value_of, max_rel=tb - 1 if t == 1 else -1)
            if t == 1:
                val = jnp.where(c // CHUNK <= r // CHUNK, val, MASKED)
            pb_ref[0, t, pl.ds(r0, _BIAS_ROWS), :] = val
            return carry

        lax.fori_loop(0, tb // _BIAS_ROWS, rows, 0)


def _prompt_bias(rel_bias):
    tb = ATT_BLOCK
    return pl.pallas_call(
        _pbias_kernel,
        out_shape=jax.ShapeDtypeStruct((N_HEADS, 2, tb, tb), F32),
        grid=(N_HEADS,),
        in_specs=[pl.BlockSpec(memory_space=pltpu.SMEM)],
        out_specs=pl.BlockSpec((1, 2, tb, tb), lambda h: (h, 0, 0, 0)),
        compiler_params=_params(("parallel",)),
        name="prompt_bias",
    )(rel_bias)


def _sbias_kernel(rb_ref, lq1_ref, lk1_ref, lq2_ref, lk2_ref, sbc_ref, sbn_ref, lam_ref,
                  *, past, lam_init):
    nq = sbn_ref.shape[2]

    def tile(shape, k0, value_of):
        qpos = past + lax.broadcasted_iota(jnp.int32, shape, 0) % nq
        kpos = k0 + lax.broadcasted_iota(jnp.int32, shape, 1)
        val = _bias_minus_far(kpos - qpos, value_of, max_rel=k0 + shape[1] - 1 - past)
        return jnp.where(kpos // CHUNK <= qpos // CHUNK, val, MASKED)

    for h in range(N_HEADS):
        value_of = lambda b, h=h: rb_ref[b, h]
        sbc_ref[h] = tile(sbc_ref.shape[1:], past - sbc_ref.shape[2], value_of)
        sbn_ref[h] = tile(sbn_ref.shape[1:], past, value_of)

    s1 = jnp.sum(lq1_ref[...].astype(F32) * lk1_ref[...].astype(F32), axis=-1, keepdims=True)
    s2 = jnp.sum(lq2_ref[...].astype(F32) * lk2_ref[...].astype(F32), axis=-1, keepdims=True)
    lam_ref[...] = jnp.exp(s1) - jnp.exp(s2) + lam_init


def _sample_bias_and_lambda(rel_bias, lq1, lk1, lq2, lk2, past, nq, lam_init):
    vmem = pl.BlockSpec(memory_space=pltpu.VMEM)
    return pl.pallas_call(
        functools.partial(_sbias_kernel, past=past, lam_init=lam_init),
        out_shape=(jax.ShapeDtypeStruct((N_HEADS, 2 * nq, NEAR), F32),
                   jax.ShapeDtypeStruct((N_HEADS, 2 * nq, nq), F32),
                   jax.ShapeDtypeStruct((1, 1), F32)),
        in_specs=[pl.BlockSpec(memory_space=pltpu.SMEM)] + [vmem] * 4,
        out_specs=(vmem, vmem, vmem),
        name="sample_bias_lambda",
    )(rel_bias, lq1, lk1, lq2, lk2)


def _silu(x):
    return x * jax.nn.sigmoid(x)


def _proj_kernel(*refs, epilogue, n_out, norm, cast):
    refs = list(refs)
    x_ref = refs.pop(0)
    g_ref = refs.pop(0) if norm else None
    w_ref = refs.pop(0)
    xn_ref = refs.pop(0) if norm else None
    if cast:
        wb_ref = refs[n_out]

        @pl.when(pl.program_id(1) == 0)
        def _():
            wb_ref[...] = w_ref[...].astype(BF16)

        w = wb_ref[...]
    else:
        w = w_ref[...]
    if norm:
        xf = x_ref[...].astype(F32)
        xf = xf * lax.rsqrt(jnp.mean(xf * xf, axis=-1, keepdims=True) + EPS)
        x = (xf * g_ref[...]).astype(BF16)
        xn_ref[...] = x
    else:
        x = x_ref[...]
    r = jnp.dot(x, w, preferred_element_type=F32)
    for o_ref, o in zip(refs[:n_out], epilogue(r)):
        o_ref[...] = o.astype(o_ref.dtype)


def _proj(x, w, col_off, n, epilogue, out_dtypes, tm, tn, name, gain=None, cast=False):
    m, d = x.shape
    jb = col_off // tn
    norm = gain is not None
    assert not norm or n == tn
    in_specs = [pl.BlockSpec((tm, d), lambda j, i: (i, 0))]
    out_shape, out_specs = [], []
    if norm:
        in_specs.append(pl.BlockSpec((1, d), lambda j, i: (0, 0)))
        out_shape.append(jax.ShapeDtypeStruct((m, d), BF16))
        out_specs.append(pl.BlockSpec((tm, d), lambda j, i: (i, 0)))
    in_specs.append(pl.BlockSpec((d, tn), lambda j, i: (0, jb + j)))
    out_shape += [jax.ShapeDtypeStruct((m, n), dt) for dt in out_dtypes]
    out_specs += [pl.BlockSpec((tm, tn), lambda j, i: (i, j)) for _ in out_dtypes]
    if cast:
        out_shape.append(jax.ShapeDtypeStruct((d, n), BF16))
        out_specs.append(pl.BlockSpec((d, tn), lambda j, i: (0, j)))
    args = (x, gain.reshape(1, d), w) if norm else (x, w)
    return pl.pallas_call(
        functools.partial(_proj_kernel, epilogue=epilogue, n_out=len(out_dtypes), norm=norm, cast=cast),
        out_shape=tuple(out_shape),
        grid=(n // tn, m // tm),
        in_specs=in_specs,
        out_specs=tuple(out_specs),
        compiler_params=_params(("parallel", "arbitrary" if cast else "parallel")),
        name=name,
    )(*args)


def _conv_kernel(x_ref, wb_ref, wc_ref, wh_ref, wz_ref, st_ref, cw_ref,
                 t_ref, nc_ref, carry_ref):
    ti = pl.program_id(2)

    @pl.when(ti == 0)
    def _():
        carry_ref[:, 6:8, :] = st_ref[...]

    tb, tt, d = x_ref.shape
    x = x_ref[...].reshape(tb * tt, d)

    def mm(w_ref):
        return jnp.dot(x, w_ref[...], preferred_element_type=F32).reshape(tb, tt, -1)

    u = mm(wc_ref) * mm(wh_ref)
    c0 = carry_ref[:, 6:7, :]
    c1 = carry_ref[:, 7:8, :]
    row = lax.broadcasted_iota(jnp.int32, u.shape, 1)
    u1 = jnp.where(row == 0, c1, pltpu.roll(u, 1, 1))
    u2 = jnp.where(row == 0, c0, jnp.where(row == 1, c1, pltpu.roll(u, 2, 1)))
    y = cw_ref[0:1, :] * u2 + cw_ref[1:2, :] * u1 + cw_ref[2:3, :] * u
    carry_ref[...] = u[:, tt - 8:, :]
    nc_ref[...] = carry_ref[:, 6:8, :]
    t_ref[...] = (mm(wb_ref) * y * _silu(mm(wz_ref))).astype(t_ref.dtype)


def _conv_branch(xn3, w, state, conv_w, tb, tt, tc):
    b, t, d = xn3.shape
    wspec = lambda off: pl.BlockSpec((d, tc), lambda c, bi, ti, o=(off - OFF_BG) // tc: (0, o + c))
    return pl.pallas_call(
        _conv_kernel,
        out_shape=(jax.ShapeDtypeStruct((b, t, CONV_W), BF16),
                   jax.ShapeDtypeStruct((b, CONV_K - 1, CONV_W), F32)),
        grid=(CONV_W // tc, b // tb, t // tt),
        in_specs=[pl.BlockSpec((tb, tt, d), lambda c, bi, ti: (bi, ti, 0)),
                  wspec(OFF_BG), wspec(OFF_CG), wspec(OFF_H), wspec(OFF_ZC),
                  pl.BlockSpec((tb, CONV_K - 1, tc), lambda c, bi, ti: (bi, 0, c)),
                  pl.BlockSpec((CONV_K, tc), lambda c, bi, ti: (0, c))],
        out_specs=(pl.BlockSpec((tb, tt, tc), lambda c, bi, ti: (bi, ti, c)),
                   pl.BlockSpec((tb, CONV_K - 1, tc), lambda c, bi, ti: (bi, 0, c))),
        scratch_shapes=[pltpu.VMEM((tb, 8, tc), F32)],
        compiler_params=_params(("parallel", "parallel", "arbitrary")),
        name="conv_branch",
    )(xn3, w, w, w, w, state, conv_w)


_NT = (((1,), (1,)), ((), ()))


def _head_out(o, hn, sza, lam_init):
    o = o * lax.rsqrt(jnp.mean(o * o, axis=-1, keepdims=True) + EPS)
    return o * hn * (1.0 - lam_init) * sza


def _pattn_kernel(lam_ref, q_ref, k_ref, v_ref, bias_ref, sza_ref, hn_ref, o_ref,
                  s_ref, p_ref, *, lam_init):
    gi = pl.program_id(2)
    tb = ATT_BLOCK
    hw = tb // 2

    def block_tasks(n, r):
        rows = slice(r * tb, (r + 1) * tb)
        q = q_ref[0, rows, :]
        lane = lax.broadcasted_iota(jnp.int32, q.shape, 1)
        zero = jnp.zeros_like(q)
        qmaps = (jnp.where(lane < D_QK, q, zero), jnp.where(lane >= D_QK, q, zero))
        chunks = [slice(c * hw, (c + 1) * hw) for c in range(2 * (n + 1))]
        sums = [[], []]
        outs = [None, None]
        slab_of = lambda mi: (2 * r + mi) % ATT_SLABS

        def scores(mi):
            for j in range(n + 1):
                s = lax.dot_general(qmaps[mi], k_ref[0, j * tb:(j + 1) * tb, :], _NT,
                                    preferred_element_type=F32)
                if j == n:
                    s = s + bias_ref[0, 1]
                elif j == n - 1:
                    s = s + bias_ref[0, 0]
                s_ref[slab_of(mi), :, j * tb:(j + 1) * tb] = s
                yield

        def numerator(mi):
            slab = slab_of(mi)
            for st in range(tb // ATT_STRIP):
                srows = slice(st * ATT_STRIP, (st + 1) * ATT_STRIP)
                mpart = functools.reduce(jnp.maximum, [s_ref[slab, srows, c] for c in chunks])
                m = jnp.broadcast_to(jnp.max(mpart, axis=-1, keepdims=True), (ATT_STRIP, hw))
                yield
                lpart = None
                for c in chunks:
                    e = jnp.exp2(s_ref[slab, srows, c] - m)
                    p_ref[slab, srows, c] = e.astype(BF16)
                    lpart = e if lpart is None else lpart + e
                    yield
                sums[mi].append(jnp.sum(lpart, axis=-1, keepdims=True))

        def values(mi):
            o = jnp.dot(p_ref[slab_of(mi), :, :(n + 1) * tb], v_ref[0, :(n + 1) * tb, :],
                        preferred_element_type=F32)
            outs[mi] = o / jnp.concatenate(sums[mi], axis=0)
            if mi == 1:
                o = outs[0] - lam_ref[0, 0] * outs[1]
                o_ref[0, rows, :] = _head_out(o, hn_ref[...], sza_ref[0, rows, :].astype(F32),
                                              lam_init).astype(o_ref.dtype)
            yield

        return [(scores(mi), n + 1, numerator(mi), (tb // ATT_STRIP) * (1 + len(chunks)), values(mi))
                for mi in range(2)]

    def emit_group(g):
        units = [u for r in range(ATT_GROUP) for u in block_tasks(g * ATT_GROUP + r, r)]
        for t in range(len(units) + 2):
            if t >= 2:
                next(units[t - 2][4])
            a, na = (units[t][0], units[t][1]) if t < len(units) else (iter(()), 0)
            b, nb = (units[t - 1][2], units[t - 1][3]) if 1 <= t <= len(units) else (iter(()), 0)
            da = db = 0
            while da < na or db < nb:
                if db >= nb or (da < na and da * nb <= db * na):
                    next(a)
                    da += 1
                else:
                    next(b)
                    db += 1
            for rest in (a, b):
                for _ in rest:
                    pass

    for g in range(k_ref.shape[1] // (tb * ATT_GROUP)):
        pl.when(gi == g)(functools.partial(emit_group, g))


def _prompt_attention(lam, q, kb, vb, pbias, sza, head_norm, lam_init):
    b, t, _ = q.shape
    tb = ATT_BLOCK
    tg = tb * ATT_GROUP
    assert tb >= NEAR and t % tg == 0
    blk = pl.BlockSpec((1, tg, D_V), lambda bi, h, gi: (bi, gi, h))
    full = pl.BlockSpec((1, t, D_V), lambda bi, h, gi: (bi, 0, h))
    return pl.pallas_call(
        functools.partial(_pattn_kernel, lam_init=lam_init),
        out_shape=jax.ShapeDtypeStruct((b, t, ATT_W), BF16),
        grid=(b, N_HEADS, t // tg),
        in_specs=[pl.BlockSpec(memory_space=pltpu.SMEM),
                  blk, full, full,
                  pl.BlockSpec((1, 2, tb, tb), lambda bi, h, gi: (h, 0, 0, 0)),
                  blk,
                  pl.BlockSpec((1, D_V), lambda bi, h, gi: (0, 0))],
        out_specs=blk,
        scratch_shapes=[pltpu.VMEM((ATT_SLABS, tb, t), F32),
                        pltpu.VMEM((ATT_SLABS, tb, t), BF16)],
        compiler_params=_params(("parallel", "parallel", "arbitrary")),
        name="prompt_attention",
    )(lam, q, kb, vb, pbias, sza, head_norm)


def _sattn_kernel(lam_ref, q_ref, kc_ref, vc_ref, kn_ref, vn_ref, bc_ref, bn_ref,
                  sza_ref, hn_ref, o_ref, *, lam_init):
    nq = q_ref.shape[1]
    past = kc_ref.shape[1] // N_HEADS
    far_n = past - bc_ref.shape[2]
    lam = lam_ref[0, 0]
    kt = pltpu.einshape("thc->htc", kc_ref[0].reshape(past, N_HEADS, D_V)).astype(BF16)
    vt = pltpu.einshape("thc->htc", vc_ref[0].reshape(past, N_HEADS, D_V)).astype(BF16)

    for h in range(N_HEADS):
        cols = slice(h * D_V, (h + 1) * D_V)
        q = q_ref[0, :, cols]
        lane = lax.broadcasted_iota(jnp.int32, q.shape, 1)
        zero = jnp.zeros_like(q)
        q2m = jnp.concatenate([jnp.where(lane < D_QK, q, zero), jnp.where(lane >= D_QK, q, zero)], axis=0)
        kc = kt[h]
        s = lax.dot_general(q2m, kc, _NT, preferred_element_type=F32)
        s = jnp.concatenate([s[:, :far_n], s[:, far_n:] + bc_ref[h]], axis=1)
        sn = lax.dot_general(q2m, kn_ref[0, :, cols], _NT, preferred_element_type=F32) + bn_ref[h]
        mx = jnp.maximum(jnp.max(s, axis=-1, keepdims=True), jnp.max(sn, axis=-1, keepdims=True))
        e = jnp.exp2(s - mx)
        en = jnp.exp2(sn - mx)
        denom = jnp.sum(e, axis=-1, keepdims=True) + jnp.sum(en, axis=-1, keepdims=True)
        row = lax.broadcasted_iota(jnp.int32, denom.shape, 0)
        w = jnp.where(row < nq, 1.0, -lam) / denom

        def combine(x, w=w):
            p = x * w
            return (p[:nq] + p[nq:]).astype(BF16)

        vc = vt[h]
        o = (jnp.dot(combine(e), vc, preferred_element_type=F32)
             + jnp.dot(combine(en), vn_ref[0, :, cols], preferred_element_type=F32))
        o_ref[0, :, cols] = _head_out(o, hn_ref[...], sza_ref[0, :, cols].astype(F32),
                                      lam_init).astype(o_ref.dtype)


def _sample_attention(lam, q, cache_k, cache_v, kb, vb, sbc, sbn, sza, head_norm, lam_init, layer):
    b, nq, _ = q.shape
    rows = cache_k.shape[1]
    new = pl.BlockSpec((1, nq, ATT_W), lambda bi: (bi, 0, 0))
    cache = pl.BlockSpec((1, rows, D_V), lambda bi: (layer * b + bi, 0, 0))
    whole = lambda a: pl.BlockSpec(a.shape, lambda bi: (0,) * a.ndim)
    return pl.pallas_call(
        functools.partial(_sattn_kernel, lam_init=lam_init),
        out_shape=jax.ShapeDtypeStruct((b, nq, ATT_W), BF16),
        grid=(b,),
        in_specs=[pl.BlockSpec(memory_space=pltpu.SMEM),
                  new, cache, cache, new, new, whole(sbc), whole(sbn), new, whole(head_norm)],
        out_specs=new,
        compiler_params=_params(("parallel",)),
        name="sample_attention",
    )(lam, q, cache_k, cache_v, kb, vb, sbc, sbn, sza, head_norm)


def _out_kernel(og_ref, t_ref, sga_ref, sgc_ref, wpa_ref, wpc_ref, wo_ref, x_ref, g_ref,
                o_ref, m_ref):
    c = pl.program_id(1)
    nchunk, _, tc = m_ref.shape
    ya = jnp.dot(og_ref[...], wpa_ref[...], preferred_element_type=F32)
    yc = jnp.dot(t_ref[...], wpc_ref[...], preferred_element_type=F32)
    merged = sga_ref[...].astype(F32) * ya + sgc_ref[...].astype(F32) * yc
    m_ref[c] = merged.astype(BF16)

    @pl.when(c == nchunk - 1)
    def _():
        y = jnp.dot(m_ref[0], wo_ref[0:tc, :], preferred_element_type=F32)
        for cc in range(1, nchunk):
            y = y + jnp.dot(m_ref[cc], wo_ref[cc * tc:(cc + 1) * tc, :], preferred_element_type=F32)
        y = y * lax.rsqrt(jnp.mean(y * y, axis=-1, keepdims=True) + EPS)
        o_ref[...] = x_ref[...] + y * g_ref[...]


def _merge_out(og, t, sga, sgc, wpa, wpc, wo, x, g, tm, tc):
    m, d = x.shape
    return pl.pallas_call(
        _out_kernel,
        out_shape=jax.ShapeDtypeStruct((m, d), x.dtype),
        grid=(m // tm, d // tc),
        in_specs=[pl.BlockSpec((tm, ATT_W), lambda i, c: (i, 0)),
                  pl.BlockSpec((tm, CONV_W), lambda i, c: (i, 0)),
                  pl.BlockSpec((tm, tc), lambda i, c: (i, c)),
                  pl.BlockSpec((tm, tc), lambda i, c: (i, c)),
                  pl.BlockSpec((ATT_W, tc), lambda i, c: (0, c)),
                  pl.BlockSpec((CONV_W, tc), lambda i, c: (0, c)),
                  pl.BlockSpec((d, d), lambda i, c: (0, 0)),
                  pl.BlockSpec((tm, d), lambda i, c: (i, 0)),
                  pl.BlockSpec((1, d), lambda i, c: (0, 0))],
        out_specs=pl.BlockSpec((tm, d), lambda i, c: (i, 0)),
        scratch_shapes=[pltpu.VMEM((d // tc, tm, tc), BF16)],
        compiler_params=_params(("parallel", "arbitrary")),
        name="merge_out",
    )(og, t, sga, sgc, wpa, wpc, wo, x, g)


def _proj_specs():
    scale = LOG2E / math.sqrt(D_QK)
    sigmoid = lambda r: (jax.nn.sigmoid(r),)
    return (("q", OFF_Q, ATT_W, lambda r: (r * scale,), (BF16,)),
            ("k", OFF_K, ATT_W, lambda r: (r, r), (F32, BF16)),
            ("v", OFF_V, ATT_W, lambda r: (r, r), (F32, BF16)),
            ("za", OFF_ZA, ATT_W, lambda r: (_silu(r),), (BF16,)),
            ("ga", OFF_GA, D_MODEL, sigmoid, (BF16,)),
            ("gc", OFF_GC, D_MODEL, sigmoid, (BF16,)))


def _layer(x, conv_state, weights, attend, tm, conv_tiles, w_cache=None):
    (norm_pre, norm_post, w_in, w_conv, conv_w, wpa, wpc, wo) = weights
    b, t, d = x.shape
    m = b * t
    x2 = x.reshape(m, d)
    cast = w_cache is None
    new_cache, res, xn = {}, {}, None
    for name, off, n, epilogue, dtypes in _proj_specs():
        first = xn is None
        w, col = (w_in, off) if cast else (w_cache[name], 0)
        outs = list(_proj(x2 if first else xn, w, col, n, epilogue, dtypes,
                          min(m, 512 if first else 1024), 1024, "proj_" + name,
                          gain=norm_pre if first else None, cast=cast))
        if first:
            xn = outs.pop(0)
        if cast:
            new_cache[name] = outs.pop()
        res[name] = outs
    tcv, new_conv = _conv_branch(xn.reshape(b, t, d), w_conv, conv_state, conv_w, *conv_tiles)
    to3 = lambda a: a.reshape(b, t, ATT_W)
    (k, kb), (v, vb) = res["k"], res["v"]
    og = attend(to3(res["q"][0]), to3(kb), to3(vb), to3(res["za"][0]))
    y = _merge_out(og.reshape(m, ATT_W), tcv.reshape(m, CONV_W), res["ga"][0], res["gc"][0],
                   wpa, wpc, wo, x2, norm_post.reshape(1, d), tm, 512)
    return (y.reshape(b, t, d), k.reshape(b, t, N_HEADS, 2 * D_QK), v.reshape(b, t, N_HEADS, D_V),
            new_conv, new_cache if cast else w_cache)


def kernel(x_prompt, x_sample, cache_k, cache_v, state_conv, norm_pre, norm_post, w_in,
           lambda_q1, lambda_k1, lambda_q2, lambda_k2, head_norm, conv_w, w_proj_attn,
           w_proj_conv, w_out, rel_bias):
    depth = w_in.shape[0]
    past = cache_k.shape[2]
    dec_b, dec_t = x_sample.shape[0], x_sample.shape[1]
    xp, xs = x_prompt, x_sample
    zero_conv = jnp.zeros((xp.shape[0], CONV_K - 1, CONV_W), xp.dtype)
    pbias = _prompt_bias(rel_bias)
    outs = [[] for _ in range(6)]
    for l in range(depth):
        lam_init = 0.8 - 0.6 * math.exp(-0.3 * l)
        sbc, sbn, lam = _sample_bias_and_lambda(
            rel_bias, lambda_q1[l][None], lambda_k1[l][None], lambda_q2[l][None], lambda_k2[l][None],
            past, dec_t, lam_init)
        hn = head_norm[l].reshape(1, D_V)
        weights = (norm_pre[l], norm_post[l], w_in[l], w_in[l][:, OFF_BG:OFF_GA].astype(BF16), conv_w[l],
                   w_proj_attn[l].astype(BF16), w_proj_conv[l].astype(BF16), w_out[l].astype(BF16))

        def attend_prompt(q, kb, vb, sza):
            return _prompt_attention(lam, q, kb, vb, pbias, sza, hn, lam_init)

        ck = cache_k.reshape(depth * dec_b, past * N_HEADS, D_V)
        cv = cache_v.reshape(depth * dec_b, past * N_HEADS, D_V)

        def attend_sample(q, kb, vb, sza, l=l):
            return _sample_attention(lam, q, ck, cv, kb, vb, sbc, sbn, sza, hn, lam_init, l)

        xp, kp, vp, cp, w_cache = _layer(xp, zero_conv, weights, attend_prompt, 512, (1, 512, 512))
        xs, ksm, vsm, csm, _ = _layer(xs, state_conv[l], weights, attend_sample, 512,
                                      (dec_b, dec_t, 512), w_cache)
        for lst, val in zip(outs, (kp, vp, cp, ksm, vsm, csm)):
            lst.append(val)
    return (xp, xs) + tuple(jnp.stack(o, axis=0) for o in outs)
```

```python
import functools
import math

import jax
import jax.numpy as jnp
from jax import lax
from jax.experimental import pallas as pl
from jax.experimental.pallas import tpu as pltpu

D_MODEL = 2048
N_HEADS = 8
D_QK = 64
D_V = 2 * D_QK
ATT_W = N_HEADS * D_V
CONV_W = D_MODEL // 2
CONV_K = 3
CHUNK = 64
NUM_BUCKETS = 32
MAX_DISTANCE = 128
EPS = 1e-6
MASKED = -1e30
LOG2E = math.log2(math.e)

OFF_Q, OFF_K, OFF_V, OFF_ZA = 0, 1024, 2048, 3072
OFF_BG, OFF_CG, OFF_H, OFF_ZC = 4096, 5120, 6144, 7168
OFF_GA, OFF_GC = 8192, 10240

ATT_BLOCK = 256
ATT_GROUP = 8
ATT_SLABS = 4
ATT_STRIP = 64
V7X_VMEM_LIMIT = 56 * 1024 * 1024

F32 = jnp.float32
BF16 = jnp.bfloat16


def _params(sem, vmem=V7X_VMEM_LIMIT):
    return pltpu.CompilerParams(dimension_semantics=sem, vmem_limit_bytes=vmem)


def _bucket_py(rel):
    half = NUM_BUCKETS // 2
    max_exact = half // 2
    steps = half - max_exact
    n = abs(rel)
    if n < max_exact:
        v = n
    else:
        v = max_exact + sum(n ** steps * max_exact ** j >= MAX_DISTANCE ** j * max_exact ** steps
                            for j in range(1, steps))
    return v + (half if rel > 0 else 0)


NEAR = MAX_DISTANCE
FAR_BUCKET = _bucket_py(-NEAR)
_BREAKS = tuple((rel, _bucket_py(rel)) for rel in range(-NEAR + 1, NEAR + 1)
                if _bucket_py(rel) != _bucket_py(rel - 1))


def _bias_minus_far(rel, value_of, max_rel):
    val = value_of(FAR_BUCKET)
    for first_rel, bucket in _BREAKS:
        if first_rel > max_rel:
            break
        val = jnp.where(rel >= first_rel, value_of(bucket), val)
    return (val - value_of(FAR_BUCKET)) * LOG2E


_BIAS_ROWS = 32


def _pbias_kernel(rb_ref, pb_ref):
    h = pl.program_id(0)
    tb = ATT_BLOCK
    value_of = lambda b: rb_ref[b, h]
    for t in range(2):
        def rows(i, carry, t=t):
            r0 = pl.multiple_of(i * _BIAS_ROWS, _BIAS_ROWS)
            r = r0 + lax.broadcasted_iota(jnp.int32, (_BIAS_ROWS, tb), 0)
            c = lax.broadcasted_iota(jnp.int32, (_BIAS_ROWS, tb), 1)
            val = _bias_minus_far(c - r - (1 - t) * tb, value_of, max_rel=tb - 1 if t == 1 else -1)
            if t == 1:
                val = jnp.where(c // CHUNK <= r // CHUNK, val, MASKED)
            pb_ref[0, t, pl.ds(r0, _BIAS_ROWS), :] = val
            return carry

        lax.fori_loop(0, tb // _BIAS_ROWS, rows, 0)


def _prompt_bias(rel_bias):
    tb = ATT_BLOCK
    return pl.pallas_call(
        _pbias_kernel,
        out_shape=jax.ShapeDtypeStruct((N_HEADS, 2, tb, tb), F32),
        grid=(N_HEADS,),
        in_specs=[pl.BlockSpec(memory_space=pltpu.SMEM)],
        out_specs=pl.BlockSpec((1, 2, tb, tb), lambda h: (h, 0, 0, 0)),
        compiler_params=_params(("parallel",)),
        name="prompt_bias",
    )(rel_bias)


def _sbias_kernel(rb_ref, lq1_ref, lk1_ref, lq2_ref, lk2_ref, sbc_ref, sbn_ref, lam_ref,
                  *, past, lam_init):
    nq = sbn_ref.shape[2]

    def tile(shape, k0, value_of):
        qpos = past + lax.broadcasted_iota(jnp.int32, shape, 0) % nq
        kpos = k0 + lax.broadcasted_iota(jnp.int32, shape, 1)
        val = _bias_minus_far(kpos - qpos, value_of, max_rel=k0 + shape[1] - 1 - past)
        return jnp.where(kpos // CHUNK <= qpos // CHUNK, val, MASKED)

    for h in range(N_HEADS):
        value_of = lambda b, h=h: rb_ref[b, h]
        sbc_ref[h] = tile(sbc_ref.shape[1:], past - sbc_ref.shape[2], value_of)
        sbn_ref[h] = tile(sbn_ref.shape[1:], past, value_of)

    s1 = jnp.sum(lq1_ref[...].astype(F32) * lk1_ref[...].astype(F32), axis=-1, keepdims=True)
    s2 = jnp.sum(lq2_ref[...].astype(F32) * lk2_ref[...].astype(F32), axis=-1, keepdims=True)
    lam_ref[...] = jnp.exp(s1) - jnp.exp(s2) + lam_init


def _sample_bias_and_lambda(rel_bias, lq1, lk1, lq2, lk2, past, nq, lam_init):
    vmem = pl.BlockSpec(memory_space=pltpu.VMEM)
    return pl.pallas_call(
        functools.partial(_sbias_kernel, past=past, lam_init=lam_init),
        out_shape=(jax.ShapeDtypeStruct((N_HEADS, 2 * nq, NEAR), F32),
                   jax.ShapeDtypeStruct((N_HEADS, 2 * nq, nq), F32),
                   jax.ShapeDtypeStruct((1, 1), F32)),
        in_specs=[pl.BlockSpec(memory_space=pltpu.SMEM)] + [vmem] * 4,
        out_specs=(vmem, vmem, vmem),
        name="sample_bias_lambda",
    )(rel_bias, lq1, lk1, lq2, lk2)


def _silu(x):
    return x * jax.nn.sigmoid(x)


def _proj_kernel(*refs, epilogue, norm):
    refs = list(refs)
    x_ref = refs.pop(0)
    g_ref = refs.pop(0) if norm else None
    w_ref = refs.pop(0)
    xn_ref = refs.pop(0) if norm else None
    if norm:
        xf = x_ref[...].astype(F32)
        xf = xf * lax.rsqrt(jnp.mean(xf * xf, axis=-1, keepdims=True) + EPS)
        x = (xf * g_ref[...]).astype(BF16)
        xn_ref[...] = x
    else:
        x = x_ref[...]
    r = jnp.dot(x, w_ref[...].astype(BF16), preferred_element_type=F32)
    for o_ref, o in zip(refs, epilogue(r)):
        if len(o_ref.shape) == 3:
            for h in range(o_ref.shape[0]):
                o_ref[h] = o[:, h * D_V:(h + 1) * D_V].astype(o_ref.dtype)
        else:
            o_ref[...] = o.astype(o_ref.dtype)


def _proj(x, w, col_off, n, epilogue, out_dtypes, tm, tn, name, gain=None):
    m, d = x.shape
    jb = col_off // tn
    norm = gain is not None
    assert not norm or n == tn
    in_specs = [pl.BlockSpec((tm, d), lambda j, i: (i, 0))]
    out_shape, out_specs = [], []
    if norm:
        in_specs.append(pl.BlockSpec((1, d), lambda j, i: (0, 0)))
        out_shape.append(jax.ShapeDtypeStruct((m, d), BF16))
        out_specs.append(pl.BlockSpec((tm, d), lambda j, i: (i, 0)))
    in_specs.append(pl.BlockSpec((d, tn), lambda j, i: (0, jb + j)))
    for dt in out_dtypes:
        if isinstance(dt, tuple):
            out_shape.append(jax.ShapeDtypeStruct((n // D_V, m, D_V), dt[0]))
            out_specs.append(pl.BlockSpec((tn // D_V, tm, D_V), lambda j, i: (j, i, 0)))
        else:
            out_shape.append(jax.ShapeDtypeStruct((m, n), dt))
            out_specs.append(pl.BlockSpec((tm, tn), lambda j, i: (i, j)))
    args = (x, gain.reshape(1, d), w) if norm else (x, w)
    return pl.pallas_call(
        functools.partial(_proj_kernel, epilogue=epilogue, norm=norm),
        out_shape=tuple(out_shape),
        grid=(n // tn, m // tm),
        in_specs=in_specs,
        out_specs=tuple(out_specs),
        compiler_params=_params(("parallel", "parallel")),
        name=name,
    )(*args)


def _conv_kernel(x_ref, wb_ref, wc_ref, wh_ref, wz_ref, st_ref, cw_ref,
                 t_ref, nc_ref, carry_ref):
    ti = pl.program_id(2)

    @pl.when(ti == 0)
    def _():
        carry_ref[:, 6:8, :] = st_ref[...]

    tb, tt, d = x_ref.shape
    x = x_ref[...].reshape(tb * tt, d)

    def mm(w_ref):
        return jnp.dot(x, w_ref[...].astype(BF16), preferred_element_type=F32).reshape(tb, tt, -1)

    u = mm(wc_ref) * mm(wh_ref)
    c0 = carry_ref[:, 6:7, :]
    c1 = carry_ref[:, 7:8, :]
    row = lax.broadcasted_iota(jnp.int32, u.shape, 1)
    u1 = jnp.where(row == 0, c1, pltpu.roll(u, 1, 1))
    u2 = jnp.where(row == 0, c0, jnp.where(row == 1, c1, pltpu.roll(u, 2, 1)))
    y = cw_ref[0:1, :] * u2 + cw_ref[1:2, :] * u1 + cw_ref[2:3, :] * u
    carry_ref[...] = u[:, tt - 8:, :]
    nc_ref[...] = carry_ref[:, 6:8, :]
    t_ref[...] = (mm(wb_ref) * y * _silu(mm(wz_ref))).astype(t_ref.dtype)


def _conv_branch(xn3, w, state, conv_w, tb, tt, tc):
    b, t, d = xn3.shape
    wspec = lambda off: pl.BlockSpec((d, tc), lambda c, bi, ti, o=off // tc: (0, o + c))
    return pl.pallas_call(
        _conv_kernel,
        out_shape=(jax.ShapeDtypeStruct((b, t, CONV_W), BF16),
                   jax.ShapeDtypeStruct((b, CONV_K - 1, CONV_W), F32)),
        grid=(CONV_W // tc, b // tb, t // tt),
        in_specs=[pl.BlockSpec((tb, tt, d), lambda c, bi, ti: (bi, ti, 0)),
                  wspec(OFF_BG), wspec(OFF_CG), wspec(OFF_H), wspec(OFF_ZC),
                  pl.BlockSpec((tb, CONV_K - 1, tc), lambda c, bi, ti: (bi, 0, c)),
                  pl.BlockSpec((CONV_K, tc), lambda c, bi, ti: (0, c))],
        out_specs=(pl.BlockSpec((tb, tt, tc), lambda c, bi, ti: (bi, ti, c)),
                   pl.BlockSpec((tb, CONV_K - 1, tc), lambda c, bi, ti: (bi, 0, c))),
        scratch_shapes=[pltpu.VMEM((tb, 8, tc), F32)],
        compiler_params=_params(("parallel", "parallel", "arbitrary")),
        name="conv_branch",
    )(xn3, w, w, w, w, state, conv_w)


_NT = (((1,), (1,)), ((), ()))


def _head_out(o, hn, sza, lam_init):
    o = o * lax.rsqrt(jnp.mean(o * o, axis=-1, keepdims=True) + EPS)
    return o * hn * (1.0 - lam_init) * sza


def _pattn_kernel(lam_ref, q_ref, k_ref, v_ref, bias_ref, sza_ref, hn_ref, o_ref,
                  s_ref, p_ref, *, lam_init):
    gi = pl.program_id(2)
    tb = ATT_BLOCK
    hw = tb // 2

    def block_tasks(n, r):
        rows = slice(r * tb, (r + 1) * tb)
        q = q_ref[0, rows, :]
        lane = lax.broadcasted_iota(jnp.int32, q.shape, 1)
        zero = jnp.zeros_like(q)
        qmaps = (jnp.where(lane < D_QK, q, zero), jnp.where(lane >= D_QK, q, zero))
        chunks = [slice(c * hw, (c + 1) * hw) for c in range(2 * (n + 1))]
        sums = [[], []]
        outs = [None, None]
        slab_of = lambda mi: (2 * r + mi) % ATT_SLABS

        def scores(mi):
            for j in range(n + 1):
                s = lax.dot_general(qmaps[mi], k_ref[0, j * tb:(j + 1) * tb, :], _NT,
                                    preferred_element_type=F32)
                if j == n:
                    s = s + bias_ref[0, 1]
                elif j == n - 1:
                    s = s + bias_ref[0, 0]
                s_ref[slab_of(mi), :, j * tb:(j + 1) * tb] = s
                yield

        def numerator(mi):
            slab = slab_of(mi)
            for st in range(tb // ATT_STRIP):
                srows = slice(st * ATT_STRIP, (st + 1) * ATT_STRIP)
                mpart = functools.reduce(jnp.maximum, [s_ref[slab, srows, c] for c in chunks])
                m = jnp.broadcast_to(jnp.max(mpart, axis=-1, keepdims=True), (ATT_STRIP, hw))
                yield
                lpart = None
                for c in chunks:
                    e = jnp.exp2(s_ref[slab, srows, c] - m)
                    p_ref[slab, srows, c] = e.astype(BF16)
                    lpart = e if lpart is None else lpart + e
                    yield
                sums[mi].append(jnp.sum(lpart, axis=-1, keepdims=True))

        def values(mi):
            o = jnp.dot(p_ref[slab_of(mi), :, :(n + 1) * tb], v_ref[0, :(n + 1) * tb, :],
                        preferred_element_type=F32)
            outs[mi] = o / jnp.concatenate(sums[mi], axis=0)
            if mi == 1:
                o = outs[0] - lam_ref[0, 0] * outs[1]
                o_ref[0, rows, :] = _head_out(o, hn_ref[...], sza_ref[0, rows, :].astype(F32),
                                              lam_init).astype(o_ref.dtype)
            yield

        return [(scores(mi), n + 1, numerator(mi), (tb // ATT_STRIP) * (1 + len(chunks)), values(mi))
                for mi in range(2)]

    def emit_group(g):
        units = [u for r in range(ATT_GROUP) for u in block_tasks(g * ATT_GROUP + r, r)]
        for t in range(len(units) + 2):
            if t >= 2:
                next(units[t - 2][4])
            a, na = (units[t][0], units[t][1]) if t < len(units) else (iter(()), 0)
            b, nb = (units[t - 1][2], units[t - 1][3]) if 1 <= t <= len(units) else (iter(()), 0)
            da = db = 0
            while da < na or db < nb:
                if db >= nb or (da < na and da * nb <= db * na):
                    next(a)
                    da += 1
                else:
                    next(b)
                    db += 1
            for rest in (a, b):
                for _ in rest:
                    pass

    for g in range(k_ref.shape[1] // (tb * ATT_GROUP)):
        pl.when(gi == g)(functools.partial(emit_group, g))


def _prompt_attention(lam, q, kb, vb, pbias, sza, head_norm, lam_init, t):
    _, m, _ = q.shape
    b = m // t
    tb = ATT_BLOCK
    tg = tb * ATT_GROUP
    assert tb >= NEAR and t % tg == 0
    blk = pl.BlockSpec((1, tg, D_V), lambda bi, h, gi: (h, bi * (t // tg) + gi, 0))
    full = pl.BlockSpec((1, t, D_V), lambda bi, h, gi: (h, bi, 0))
    return pl.pallas_call(
        functools.partial(_pattn_kernel, lam_init=lam_init),
        out_shape=jax.ShapeDtypeStruct((N_HEADS, m, D_V), BF16),
        grid=(b, N_HEADS, t // tg),
        in_specs=[pl.BlockSpec(memory_space=pltpu.SMEM),
                  blk, full, full,
                  pl.BlockSpec((1, 2, tb, tb), lambda bi, h, gi: (h, 0, 0, 0)),
                  blk,
                  pl.BlockSpec((1, D_V), lambda bi, h, gi: (0, 0))],
        out_specs=blk,
        scratch_shapes=[pltpu.VMEM((ATT_SLABS, tb, t), F32),
                        pltpu.VMEM((ATT_SLABS, tb, t), BF16)],
        compiler_params=_params(("parallel", "parallel", "arbitrary")),
        name="prompt_attention",
    )(lam, q, kb, vb, pbias, sza, head_norm)


def _sattn_kernel(lam_ref, q_ref, kc_ref, vc_ref, kn_ref, vn_ref, bc_ref, bn_ref,
                  sza_ref, hn_ref, o_ref, *, lam_init):
    nq = q_ref.shape[1]
    past = kc_ref.shape[1] // N_HEADS
    far_n = past - bc_ref.shape[2]
    lam = lam_ref[0, 0]
    kt = jnp.swapaxes(kc_ref[0].reshape(past, N_HEADS, D_V), 0, 1).astype(BF16)
    vt = jnp.swapaxes(vc_ref[0].reshape(past, N_HEADS, D_V), 0, 1).astype(BF16)

    for h in range(N_HEADS):
        q = q_ref[h]
        lane = lax.broadcasted_iota(jnp.int32, q.shape, 1)
        zero = jnp.zeros_like(q)
        q2m = jnp.concatenate([jnp.where(lane < D_QK, q, zero), jnp.where(lane >= D_QK, q, zero)], axis=0)
        kc = kt[h]
        s = lax.dot_general(q2m, kc, _NT, preferred_element_type=F32)
        s = jnp.concatenate([s[:, :far_n], s[:, far_n:] + bc_ref[h]], axis=1)
        sn = lax.dot_general(q2m, kn_ref[h], _NT, preferred_element_type=F32) + bn_ref[h]
        mx = jnp.maximum(jnp.max(s, axis=-1, keepdims=True), jnp.max(sn, axis=-1, keepdims=True))
        e = jnp.exp2(s - mx)
        en = jnp.exp2(sn - mx)
        denom = jnp.sum(e, axis=-1, keepdims=True) + jnp.sum(en, axis=-1, keepdims=True)
        row = lax.broadcasted_iota(jnp.int32, denom.shape, 0)
        w = jnp.where(row < nq, 1.0, -lam) / denom

        def combine(x, w=w):
            p = x * w
            return (p[:nq] + p[nq:]).astype(BF16)

        vc = vt[h]
        o = (jnp.dot(combine(e), vc, preferred_element_type=F32)
             + jnp.dot(combine(en), vn_ref[h], preferred_element_type=F32))
        o_ref[h] = _head_out(o, hn_ref[...], sza_ref[h].astype(F32), lam_init).astype(o_ref.dtype)


def _sample_attention(lam, q, cache_k, cache_v, kb, vb, sbc, sbn, sza, head_norm, lam_init, layer):
    nq = sbn.shape[2]
    b = q.shape[1] // nq
    rows = cache_k.shape[1]
    new = pl.BlockSpec((N_HEADS, nq, D_V), lambda bi: (0, bi, 0))
    cache = pl.BlockSpec((1, rows, D_V), lambda bi: (layer * b + bi, 0, 0))
    whole = lambda a: pl.BlockSpec(a.shape, lambda bi: (0,) * a.ndim)
    return pl.pallas_call(
        functools.partial(_sattn_kernel, lam_init=lam_init),
        out_shape=jax.ShapeDtypeStruct(q.shape, BF16),
        grid=(b,),
        in_specs=[pl.BlockSpec(memory_space=pltpu.SMEM),
                  new, cache, cache, new, new, whole(sbc), whole(sbn), new, whole(head_norm)],
        out_specs=new,
        compiler_params=_params(("parallel",)),
        name="sample_attention",
    )(lam, q, cache_k, cache_v, kb, vb, sbc, sbn, sza, head_norm)


def _out_kernel(og_ref, t_ref, sga_ref, sgc_ref, wpa_ref, wpc_ref, wo_ref, x_ref, g_ref,
                o_ref, m_ref):
    c = pl.program_id(1)
    nchunk, _, tc = m_ref.shape
    og = jnp.concatenate([og_ref[h] for h in range(og_ref.shape[0])], axis=1)
    ya = jnp.dot(og, wpa_ref[...].astype(BF16), preferred_element_type=F32)
    yc = jnp.dot(t_ref[...], wpc_ref[...].astype(BF16), preferred_element_type=F32)
    merged = sga_ref[...].astype(F32) * ya + sgc_ref[...].astype(F32) * yc
    m_ref[c] = merged.astype(BF16)

    @pl.when(c == nchunk - 1)
    def _():
        y = jnp.dot(m_ref[0], wo_ref[0:tc, :].astype(BF16), preferred_element_type=F32)
        for cc in range(1, nchunk):
            y = y + jnp.dot(m_ref[cc], wo_ref[cc * tc:(cc + 1) * tc, :].astype(BF16),
                            preferred_element_type=F32)
        y = y * lax.rsqrt(jnp.mean(y * y, axis=-1, keepdims=True) + EPS)
        o_ref[...] = x_ref[...] + y * g_ref[...]


def _merge_out(og, t, sga, sgc, wpa, wpc, wo, x, g, tm, tc):
    m, d = x.shape
    return pl.pallas_call(
        _out_kernel,
        out_shape=jax.ShapeDtypeStruct((m, d), x.dtype),
        grid=(m // tm, d // tc),
        in_specs=[pl.BlockSpec((N_HEADS, tm, D_V), lambda i, c: (0, i, 0)),
                  pl.BlockSpec((tm, CONV_W), lambda i, c: (i, 0)),
                  pl.BlockSpec((tm, tc), lambda i, c: (i, c)),
                  pl.BlockSpec((tm, tc), lambda i, c: (i, c)),
                  pl.BlockSpec((ATT_W, tc), lambda i, c: (0, c)),
                  pl.BlockSpec((CONV_W, tc), lambda i, c: (0, c)),
                  pl.BlockSpec((d, d), lambda i, c: (0, 0), pipeline_mode=pl.Buffered(1)),
                  pl.BlockSpec((tm, d), lambda i, c: (i, 0)),
                  pl.BlockSpec((1, d), lambda i, c: (0, 0))],
        out_specs=pl.BlockSpec((tm, d), lambda i, c: (i, 0)),
        scratch_shapes=[pltpu.VMEM((d // tc, tm, tc), BF16)],
        compiler_params=_params(("parallel", "arbitrary")),
        name="merge_out",
    )(og, t, sga, sgc, wpa, wpc, wo, x, g)


def _proj_specs():
    scale = LOG2E / math.sqrt(D_QK)
    sigmoid = lambda r: (jax.nn.sigmoid(r),)
    heads = (BF16, "heads")
    return (("q", OFF_Q, ATT_W, lambda r: (r * scale,), (heads,)),
            ("k", OFF_K, ATT_W, lambda r: (r, r), (F32, heads)),
            ("v", OFF_V, ATT_W, lambda r: (r, r), (F32, heads)),
            ("za", OFF_ZA, ATT_W, lambda r: (_silu(r),), (heads,)),
            ("ga", OFF_GA, D_MODEL, sigmoid, (BF16,)),
            ("gc", OFF_GC, D_MODEL, sigmoid, (BF16,)))


def _layer(x, conv_state, weights, attend, tm, conv_tiles):
    (norm_pre, norm_post, w_in, conv_w, wpa, wpc, wo) = weights
    b, t, d = x.shape
    m = b * t
    x2 = x.reshape(m, d)
    res, xn = {}, None
    for name, off, n, epilogue, dtypes in _proj_specs():
        first = xn is None
        outs = list(_proj(x2 if first else xn, w_in, off, n, epilogue, dtypes,
                          min(m, 512 if first else 1024), 1024, "proj_" + name,
                          gain=norm_pre if first else None))
        if first:
            xn = outs.pop(0)
        res[name] = outs
    tcv, new_conv = _conv_branch(xn.reshape(b, t, d), w_in, conv_state, conv_w, *conv_tiles)
    (k, kb), (v, vb) = res["k"], res["v"]
    og = attend(res["q"][0], kb, vb, res["za"][0])
    y = _merge_out(og, tcv.reshape(m, CONV_W), res["ga"][0], res["gc"][0],
                   wpa, wpc, wo, x2, norm_post.reshape(1, d), tm, 512)
    return (y.reshape(b, t, d), k.reshape(b, t, N_HEADS, 2 * D_QK), v.reshape(b, t, N_HEADS, D_V),
            new_conv)


def kernel(x_prompt, x_sample, cache_k, cache_v, state_conv, norm_pre, norm_post, w_in,
           lambda_q1, lambda_k1, lambda_q2, lambda_k2, head_norm, conv_w, w_proj_attn,
           w_proj_conv, w_out, rel_bias):
    depth = w_in.shape[0]
    past = cache_k.shape[2]
    dec_b, dec_t = x_sample.shape[0], x_sample.shape[1]
    xp, xs = x_prompt, x_sample
    zero_conv = jnp.zeros((xp.shape[0], CONV_K - 1, CONV_W), xp.dtype)
    pbias = _prompt_bias(rel_bias)
    outs = [[] for _ in range(6)]
    for l in range(depth):
        lam_init = 0.8 - 0.6 * math.exp(-0.3 * l)
        sbc, sbn, lam = _sample_bias_and_lambda(
            rel_bias, lambda_q1[l][None], lambda_k1[l][None], lambda_q2[l][None], lambda_k2[l][None],
            past, dec_t, lam_init)
        hn = head_norm[l].reshape(1, D_V)
        weights = (norm_pre[l], norm_post[l], w_in[l], conv_w[l],
                   w_proj_attn[l], w_proj_conv[l], w_out[l])

        def attend_prompt(q, kb, vb, sza):
            return _prompt_attention(lam, q, kb, vb, pbias, sza, hn, lam_init, x_prompt.shape[1])

        ck = cache_k.reshape(depth * dec_b, past * N_HEADS, D_V)
        cv = cache_v.reshape(depth * dec_b, past * N_HEADS, D_V)

        def attend_sample(q, kb, vb, sza, l=l):
            return _sample_attention(lam, q, ck, cv, kb, vb, sbc, sbn, sza, hn, lam_init, l)

        xp, kp, vp, cp = _layer(xp, zero_conv, weights, attend_prompt, 512, (1, 512, 512))
        xs, ksm, vsm, csm = _layer(xs, state_conv[l], weights, attend_sample, 512,
                                   (dec_b, dec_t, 512))
        for lst, val in zip(outs, (kp, vp, cp, ksm, vsm, csm)):
            lst.append(val)
    return (xp, xs) + tuple(jnp.stack(o, axis=0) for o in outs)
```

```python
import functools
import math

import jax
import jax.numpy as jnp
from jax import lax
from jax.experimental import pallas as pl
from jax.experimental.pallas import tpu as pltpu

D_MODEL = 2048
N_HEADS = 8
D_QK = 64
D_V = 2 * D_QK
ATT_W = N_HEADS * D_V
CONV_W = D_MODEL // 2
CONV_K = 3
CHUNK = 64
NUM_BUCKETS = 32
MAX_DISTANCE = 128
EPS = 1e-6
MASKED = -1e30
LOG2E = math.log2(math.e)

OFF_Q, OFF_K, OFF_V, OFF_ZA = 0, 1024, 2048, 3072
OFF_BG, OFF_CG, OFF_H, OFF_ZC = 4096, 5120, 6144, 7168
OFF_GA, OFF_GC = 8192, 10240

ATT_BLOCK = 256
ATT_GROUP = 8
ATT_SLABS = 4
ATT_STRIP = 64
MERGE_CHUNK = 512
V7X_VMEM_LIMIT = 56 * 1024 * 1024

F32 = jnp.float32
BF16 = jnp.bfloat16


def _params(sem, vmem=V7X_VMEM_LIMIT):
    return pltpu.CompilerParams(dimension_semantics=sem, vmem_limit_bytes=vmem)


def _bucket_py(rel):
    half = NUM_BUCKETS // 2
    max_exact = half // 2
    steps = half - max_exact
    n = abs(rel)
    if n < max_exact:
        v = n
    else:
        v = max_exact + sum(n ** steps * max_exact ** j >= MAX_DISTANCE ** j * max_exact ** steps
                            for j in range(1, steps))
    return v + (half if rel > 0 else 0)


NEAR = MAX_DISTANCE
FAR_BUCKET = _bucket_py(-NEAR)
_BREAKS = tuple((rel, _bucket_py(rel)) for rel in range(-NEAR + 1, NEAR + 1)
                if _bucket_py(rel) != _bucket_py(rel - 1))


def _bias_minus_far(rel, value_of, max_rel):
    val = value_of(FAR_BUCKET)
    for first_rel, bucket in _BREAKS:
        if first_rel > max_rel:
            break
        val = jnp.where(rel >= first_rel, value_of(bucket), val)
    return (val - value_of(FAR_BUCKET)) * LOG2E


_BIAS_ROWS = 32


def _pbias_kernel(rb_ref, pb_ref):
    h = pl.program_id(0)
    tb = ATT_BLOCK
    value_of = lambda b: rb_ref[b, h]
    for t in range(2):
        def rows(i, carry, t=t):
            r0 = pl.multiple_of(i * _BIAS_ROWS, _BIAS_ROWS)
            r = r0 + lax.broadcasted_iota(jnp.int32, (_BIAS_ROWS, tb), 0)
            c = lax.broadcasted_iota(jnp.int32, (_BIAS_ROWS, tb), 1)
            val = _bias_minus_far(c - r - (1 - t) * tb, value_of, max_rel=tb - 1 if t == 1 else -1)
            if t == 1:
                val = jnp.where(c // CHUNK <= r // CHUNK, val, MASKED)
            pb_ref[0, t, pl.ds(r0, _BIAS_ROWS), :] = val
            return carry

        lax.fori_loop(0, tb // _BIAS_ROWS, rows, 0)


def _prompt_bias(rel_bias):
    tb = ATT_BLOCK
    return pl.pallas_call(
        _pbias_kernel,
        out_shape=jax.ShapeDtypeStruct((N_HEADS, 2, tb, tb), F32),
        grid=(N_HEADS,),
        in_specs=[pl.BlockSpec(memory_space=pltpu.SMEM)],
        out_specs=pl.BlockSpec((1, 2, tb, tb), lambda h: (h, 0, 0, 0)),
        compiler_params=_params(("parallel",)),
        name="prompt_bias",
    )(rel_bias)


def _sbias_kernel(rb_ref, lq1_ref, lk1_ref, lq2_ref, lk2_ref, sbc_ref, sbn_ref, lam_ref,
                  *, past, lam_init):
    nq = sbn_ref.shape[2]

    def tile(shape, k0, value_of):
        qpos = past + lax.broadcasted_iota(jnp.int32, shape, 0) % nq
        kpos = k0 + lax.broadcasted_iota(jnp.int32, shape, 1)
        val = _bias_minus_far(kpos - qpos, value_of, max_rel=k0 + shape[1] - 1 - past)
        return jnp.where(kpos // CHUNK <= qpos // CHUNK, val, MASKED)

    for h in range(N_HEADS):
        value_of = lambda b, h=h: rb_ref[b, h]
        sbc_ref[h] = tile(sbc_ref.shape[1:], past - sbc_ref.shape[2], value_of)
        sbn_ref[h] = tile(sbn_ref.shape[1:], past, value_of)

    s1 = jnp.sum(lq1_ref[...].astype(F32) * lk1_ref[...].astype(F32), axis=-1, keepdims=True)
    s2 = jnp.sum(lq2_ref[...].astype(F32) * lk2_ref[...].astype(F32), axis=-1, keepdims=True)
    lam_ref[...] = jnp.exp(s1) - jnp.exp(s2) + lam_init


def _sample_bias_and_lambda(rel_bias, lq1, lk1, lq2, lk2, past, nq, lam_init):
    vmem = pl.BlockSpec(memory_space=pltpu.VMEM)
    return pl.pallas_call(
        functools.partial(_sbias_kernel, past=past, lam_init=lam_init),
        out_shape=(jax.ShapeDtypeStruct((N_HEADS, 2 * nq, NEAR), F32),
                   jax.ShapeDtypeStruct((N_HEADS, 2 * nq, nq), F32),
                   jax.ShapeDtypeStruct((1, 1), F32)),
        in_specs=[pl.BlockSpec(memory_space=pltpu.SMEM)] + [vmem] * 4,
        out_specs=(vmem, vmem, vmem),
        name="sample_bias_lambda",
    )(rel_bias, lq1, lk1, lq2, lk2)


def _silu(x):
    return x * jax.nn.sigmoid(x)


def _proj_kernel(*refs, epilogue, norm):
    refs = list(refs)
    x_ref = refs.pop(0)
    g_ref = refs.pop(0) if norm else None
    w_ref = refs.pop(0)
    xn_ref = refs.pop(0) if norm else None
    if norm:
        xf = x_ref[...].astype(F32)
        xf = xf * lax.rsqrt(jnp.mean(xf * xf, axis=-1, keepdims=True) + EPS)
        x = (xf * g_ref[...]).astype(BF16)
        xn_ref[...] = x
    else:
        x = x_ref[...]
    r = jnp.dot(x, w_ref[...].astype(BF16), preferred_element_type=F32)
    for o_ref, o in zip(refs, epilogue(r)):
        if len(o_ref.shape) == 3:
            for h in range(o_ref.shape[0]):
                o_ref[h] = o[:, h * D_V:(h + 1) * D_V].astype(o_ref.dtype)
        else:
            o_ref[...] = o.astype(o_ref.dtype)


def _proj(x, w, col_off, n, epilogue, out_dtypes, tm, tn, name, gain=None):
    m, d = x.shape
    jb = col_off // tn
    norm = gain is not None
    assert not norm or n == tn
    in_specs = [pl.BlockSpec((tm, d), lambda j, i: (i, 0))]
    out_shape, out_specs = [], []
    if norm:
        in_specs.append(pl.BlockSpec((1, d), lambda j, i: (0, 0)))
        out_shape.append(jax.ShapeDtypeStruct((m, d), BF16))
        out_specs.append(pl.BlockSpec((tm, d), lambda j, i: (i, 0)))
    in_specs.append(pl.BlockSpec((d, tn), lambda j, i: (0, jb + j)))
    for dt in out_dtypes:
        if isinstance(dt, tuple):
            out_shape.append(jax.ShapeDtypeStruct((n // D_V, m, D_V), dt[0]))
            out_specs.append(pl.BlockSpec((tn // D_V, tm, D_V), lambda j, i: (j, i, 0)))
        else:
            out_shape.append(jax.ShapeDtypeStruct((m, n), dt))
            out_specs.append(pl.BlockSpec((tm, tn), lambda j, i: (i, j)))
    args = (x, gain.reshape(1, d), w) if norm else (x, w)
    return pl.pallas_call(
        functools.partial(_proj_kernel, epilogue=epilogue, norm=norm),
        out_shape=tuple(out_shape),
        grid=(n // tn, m // tm),
        in_specs=in_specs,
        out_specs=tuple(out_specs),
        compiler_params=_params(("parallel", "parallel")),
        name=name,
    )(*args)


def _conv_kernel(x_ref, wb_ref, wc_ref, wh_ref, wz_ref, st_ref, cw_ref,
                 t_ref, nc_ref, carry_ref):
    ti = pl.program_id(2)

    @pl.when(ti == 0)
    def _():
        carry_ref[:, 6:8, :] = st_ref[...]

    tb, tt, d = x_ref.shape
    x = x_ref[...].reshape(tb * tt, d)

    def mm(w_ref):
        return jnp.dot(x, w_ref[...].astype(BF16), preferred_element_type=F32).reshape(tb, tt, -1)

    u = mm(wc_ref) * mm(wh_ref)
    c0 = carry_ref[:, 6:7, :]
    c1 = carry_ref[:, 7:8, :]
    row = lax.broadcasted_iota(jnp.int32, u.shape, 1)
    u1 = jnp.where(row == 0, c1, pltpu.roll(u, 1, 1))
    u2 = jnp.where(row == 0, c0, jnp.where(row == 1, c1, pltpu.roll(u, 2, 1)))
    y = cw_ref[0:1, :] * u2 + cw_ref[1:2, :] * u1 + cw_ref[2:3, :] * u
    carry_ref[...] = u[:, tt - 8:, :]
    nc_ref[...] = carry_ref[:, 6:8, :]
    t_ref[...] = (mm(wb_ref) * y * _silu(mm(wz_ref))).astype(t_ref.dtype)


def _conv_branch(xn3, w, state, conv_w, tb, tt, tc):
    b, t, d = xn3.shape
    wspec = lambda off: pl.BlockSpec((d, tc), lambda c, bi, ti, o=off // tc: (0, o + c))
    return pl.pallas_call(
        _conv_kernel,
        out_shape=(jax.ShapeDtypeStruct((b, t, CONV_W), BF16),
                   jax.ShapeDtypeStruct((b, CONV_K - 1, CONV_W), F32)),
        grid=(CONV_W // tc, b // tb, t // tt),
        in_specs=[pl.BlockSpec((tb, tt, d), lambda c, bi, ti: (bi, ti, 0)),
                  wspec(OFF_BG), wspec(OFF_CG), wspec(OFF_H), wspec(OFF_ZC),
                  pl.BlockSpec((tb, CONV_K - 1, tc), lambda c, bi, ti: (bi, 0, c)),
                  pl.BlockSpec((CONV_K, tc), lambda c, bi, ti: (0, c))],
        out_specs=(pl.BlockSpec((tb, tt, tc), lambda c, bi, ti: (bi, ti, c)),
                   pl.BlockSpec((tb, CONV_K - 1, tc), lambda c, bi, ti: (bi, 0, c))),
        scratch_shapes=[pltpu.VMEM((tb, 8, tc), F32)],
        compiler_params=_params(("parallel", "parallel", "arbitrary")),
        name="conv_branch",
    )(xn3, w, w, w, w, state, conv_w)


_NT = (((1,), (1,)), ((), ()))


def _head_out(o, hn, sza, lam_init):
    o = o * lax.rsqrt(jnp.mean(o * o, axis=-1, keepdims=True) + EPS)
    return o * hn * (1.0 - lam_init) * sza


def _pattn_kernel(lam_ref, q_ref, k_ref, v_ref, bias_ref, sza_ref, hn_ref, o_ref,
                  s_ref, p_ref, *, lam_init):
    gi = pl.program_id(2)
    tb = ATT_BLOCK
    hw = tb // 2

    def block_tasks(n, r):
        rows = slice(r * tb, (r + 1) * tb)
        q = q_ref[0, rows, :]
        lane = lax.broadcasted_iota(jnp.int32, q.shape, 1)
        zero = jnp.zeros_like(q)
        qmaps = (jnp.where(lane < D_QK, q, zero), jnp.where(lane >= D_QK, q, zero))
        chunks = [slice(c * hw, (c + 1) * hw) for c in range(2 * (n + 1))]
        sums = [[], []]
        outs = [None, None]
        slab_of = lambda mi: (2 * r + mi) % ATT_SLABS

        def scores(mi):
            for j in range(n + 1):
                s = lax.dot_general(qmaps[mi], k_ref[0, j * tb:(j + 1) * tb, :], _NT,
                                    preferred_element_type=F32)
                if j == n:
                    s = s + bias_ref[0, 1]
                elif j == n - 1:
                    s = s + bias_ref[0, 0]
                s_ref[slab_of(mi), :, j * tb:(j + 1) * tb] = s
                yield

        def numerator(mi):
            slab = slab_of(mi)
            for st in range(tb // ATT_STRIP):
                srows = slice(st * ATT_STRIP, (st + 1) * ATT_STRIP)
                mpart = functools.reduce(jnp.maximum, [s_ref[slab, srows, c] for c in chunks])
                m = jnp.broadcast_to(jnp.max(mpart, axis=-1, keepdims=True), (ATT_STRIP, hw))
                yield
                lpart = None
                for c in chunks:
                    e = jnp.exp2(s_ref[slab, srows, c] - m)
                    p_ref[slab, srows, c] = e.astype(BF16)
                    lpart = e if lpart is None else lpart + e
                    yield
                sums[mi].append(jnp.sum(lpart, axis=-1, keepdims=True))

        def values(mi):
            o = jnp.dot(p_ref[slab_of(mi), :, :(n + 1) * tb], v_ref[0, :(n + 1) * tb, :],
                        preferred_element_type=F32)
            outs[mi] = o / jnp.concatenate(sums[mi], axis=0)
            if mi == 1:
                o = outs[0] - lam_ref[0, 0] * outs[1]
                o_ref[0, rows, :] = _head_out(o, hn_ref[...], sza_ref[0, rows, :].astype(F32),
                                              lam_init).astype(o_ref.dtype)
            yield

        return [(scores(mi), n + 1, numerator(mi), (tb // ATT_STRIP) * (1 + len(chunks)), values(mi))
                for mi in range(2)]

    def emit_group(g):
        units = [u for r in range(ATT_GROUP) for u in block_tasks(g * ATT_GROUP + r, r)]
        for t in range(len(units) + 2):
            if t >= 2:
                next(units[t - 2][4])
            a, na = (units[t][0], units[t][1]) if t < len(units) else (iter(()), 0)
            b, nb = (units[t - 1][2], units[t - 1][3]) if 1 <= t <= len(units) else (iter(()), 0)
            da = db = 0
            while da < na or db < nb:
                if db >= nb or (da < na and da * nb <= db * na):
                    next(a)
                    da += 1
                else:
                    next(b)
                    db += 1
            for rest in (a, b):
                for _ in rest:
                    pass

    for g in range(k_ref.shape[1] // (tb * ATT_GROUP)):
        pl.when(gi == g)(functools.partial(emit_group, g))


def _prompt_attention(lam, q, kb, vb, pbias, sza, head_norm, lam_init, t):
    _, m, _ = q.shape
    b = m // t
    tb = ATT_BLOCK
    tg = tb * ATT_GROUP
    assert tb >= NEAR and t % tg == 0
    blk = pl.BlockSpec((1, tg, D_V), lambda bi, h, gi: (h, bi * (t // tg) + gi, 0))
    full = pl.BlockSpec((1, t, D_V), lambda bi, h, gi: (h, bi, 0))
    return pl.pallas_call(
        functools.partial(_pattn_kernel, lam_init=lam_init),
        out_shape=jax.ShapeDtypeStruct((N_HEADS, m, D_V), BF16),
        grid=(b, N_HEADS, t // tg),
        in_specs=[pl.BlockSpec(memory_space=pltpu.SMEM),
                  blk, full, full,
                  pl.BlockSpec((1, 2, tb, tb), lambda bi, h, gi: (h, 0, 0, 0)),
                  blk,
                  pl.BlockSpec((1, D_V), lambda bi, h, gi: (0, 0))],
        out_specs=blk,
        scratch_shapes=[pltpu.VMEM((ATT_SLABS, tb, t), F32),
                        pltpu.VMEM((ATT_SLABS, tb, t), BF16)],
        compiler_params=_params(("parallel", "parallel", "arbitrary")),
        name="prompt_attention",
    )(lam, q, kb, vb, pbias, sza, head_norm)


def _sattn_kernel(lam_ref, q_ref, kc_ref, vc_ref, kn_ref, vn_ref, bc_ref, bn_ref,
                  sza_ref, hn_ref, o_ref, *, lam_init):
    nq = q_ref.shape[1]
    past = kc_ref.shape[1] // N_HEADS
    far_n = past - bc_ref.shape[2]
    lam = lam_ref[0, 0]
    kt = jnp.swapaxes(kc_ref[0].reshape(past, N_HEADS, D_V), 0, 1).astype(BF16)
    vt = jnp.swapaxes(vc_ref[0].reshape(past, N_HEADS, D_V), 0, 1).astype(BF16)

    for h in range(N_HEADS):
        q = q_ref[h]
        lane = lax.broadcasted_iota(jnp.int32, q.shape, 1)
        zero = jnp.zeros_like(q)
        q2m = jnp.concatenate([jnp.where(lane < D_QK, q, zero), jnp.where(lane >= D_QK, q, zero)], axis=0)
        kc = kt[h]
        s = lax.dot_general(q2m, kc, _NT, preferred_element_type=F32)
        s = jnp.concatenate([s[:, :far_n], s[:, far_n:] + bc_ref[h]], axis=1)
        sn = lax.dot_general(q2m, kn_ref[h], _NT, preferred_element_type=F32) + bn_ref[h]
        mx = jnp.maximum(jnp.max(s, axis=-1, keepdims=True), jnp.max(sn, axis=-1, keepdims=True))
        e = jnp.exp2(s - mx)
        en = jnp.exp2(sn - mx)
        denom = jnp.sum(e, axis=-1, keepdims=True) + jnp.sum(en, axis=-1, keepdims=True)
        row = lax.broadcasted_iota(jnp.int32, denom.shape, 0)
        w = jnp.where(row < nq, 1.0, -lam) / denom

        def combine(x, w=w):
            p = x * w
            return (p[:nq] + p[nq:]).astype(BF16)

        vc = vt[h]
        o = (jnp.dot(combine(e), vc, preferred_element_type=F32)
             + jnp.dot(combine(en), vn_ref[h], preferred_element_type=F32))
        o_ref[h] = _head_out(o, hn_ref[...], sza_ref[h].astype(F32), lam_init).astype(o_ref.dtype)


def _sample_attention(lam, q, cache_k, cache_v, kb, vb, sbc, sbn, sza, head_norm, lam_init, layer):
    nq = sbn.shape[2]
    b = q.shape[1] // nq
    rows = cache_k.shape[1]
    new = pl.BlockSpec((N_HEADS, nq, D_V), lambda bi: (0, bi, 0))
    cache = pl.BlockSpec((1, rows, D_V), lambda bi: (layer * b + bi, 0, 0))
    whole = lambda a: pl.BlockSpec(a.shape, lambda bi: (0,) * a.ndim)
    return pl.pallas_call(
        functools.partial(_sattn_kernel, lam_init=lam_init),
        out_shape=jax.ShapeDtypeStruct(q.shape, BF16),
        grid=(b,),
        in_specs=[pl.BlockSpec(memory_space=pltpu.SMEM),
                  new, cache, cache, new, new, whole(sbc), whole(sbn), new, whole(head_norm)],
        out_specs=new,
        compiler_params=_params(("parallel",)),
        name="sample_attention",
    )(lam, q, cache_k, cache_v, kb, vb, sbc, sbn, sza, head_norm)


def _out_kernel(og_ref, t_ref, sga_ref, sgc_ref, wpa_ref, wpc_ref, wo_ref, x_ref, g_ref,
                o_ref, m_ref):
    c = pl.program_id(1)
    nchunk, _, tc = m_ref.shape
    og = jnp.concatenate([og_ref[h] for h in range(og_ref.shape[0])], axis=1)
    ya = jnp.dot(og, wpa_ref[c], preferred_element_type=F32)
    yc = jnp.dot(t_ref[...], wpc_ref[c], preferred_element_type=F32)
    merged = sga_ref[...].astype(F32) * ya + sgc_ref[...].astype(F32) * yc
    m_ref[c] = merged.astype(BF16)

    @pl.when(c == nchunk - 1)
    def _():
        y = jnp.dot(m_ref[0], wo_ref[0], preferred_element_type=F32)
        for cc in range(1, nchunk):
            y = y + jnp.dot(m_ref[cc], wo_ref[cc], preferred_element_type=F32)
        y = y * lax.rsqrt(jnp.mean(y * y, axis=-1, keepdims=True) + EPS)
        o_ref[...] = x_ref[...] + y * g_ref[...]


def _cast_chunks_kernel(w_ref, o_ref):
    o_ref[0] = w_ref[...].astype(o_ref.dtype)


def _cast_chunks(w, tc, rows=False):
    k, n = w.shape
    blk, imap = ((tc, n), lambda j: (j, 0)) if rows else ((k, tc), lambda j: (0, j))
    nchunk = (k if rows else n) // tc
    return pl.pallas_call(
        _cast_chunks_kernel,
        out_shape=jax.ShapeDtypeStruct((nchunk,) + blk, BF16),
        grid=(nchunk,),
        in_specs=[pl.BlockSpec(blk, imap)],
        out_specs=pl.BlockSpec((1,) + blk, lambda j: (j, 0, 0)),
        compiler_params=_params(("parallel",)),
        name="cast_chunks",
    )(w)


def _merge_out(og, t, sga, sgc, wpa, wpc, wo, x, g, tm, tc):
    m, d = x.shape
    resident = lambda a: pl.BlockSpec(a.shape, lambda i, c: (0, 0, 0), pipeline_mode=pl.Buffered(1))
    return pl.pallas_call(
        _out_kernel,
        out_shape=jax.ShapeDtypeStruct((m, d), x.dtype),
        grid=(m // tm, d // tc),
        in_specs=[pl.BlockSpec((N_HEADS, tm, D_V), lambda i, c: (0, i, 0)),
                  pl.BlockSpec((tm, CONV_W), lambda i, c: (i, 0)),
                  pl.BlockSpec((tm, tc), lambda i, c: (i, c)),
                  pl.BlockSpec((tm, tc), lambda i, c: (i, c)),
                  resident(wpa), resident(wpc), resident(wo),
                  pl.BlockSpec((tm, d), lambda i, c: (i, 0)),
                  pl.BlockSpec((1, d), lambda i, c: (0, 0))],
        out_specs=pl.BlockSpec((tm, d), lambda i, c: (i, 0)),
        scratch_shapes=[pltpu.VMEM((d // tc, tm, tc), BF16)],
        compiler_params=_params(("parallel", "arbitrary")),
        name="merge_out",
    )(og, t, sga, sgc, wpa, wpc, wo, x, g)


def _proj_specs():
    scale = LOG2E / math.sqrt(D_QK)
    sigmoid = lambda r: (jax.nn.sigmoid(r),)
    heads = (BF16, "heads")
    return (("q", OFF_Q, ATT_W, lambda r: (r * scale,), (heads,)),
            ("k", OFF_K, ATT_W, lambda r: (r, r), (F32, heads)),
            ("v", OFF_V, ATT_W, lambda r: (r, r), (F32, heads)),
            ("za", OFF_ZA, ATT_W, lambda r: (_silu(r),), (heads,)),
            ("ga", OFF_GA, D_MODEL, sigmoid, (BF16,)),
            ("gc", OFF_GC, D_MODEL, sigmoid, (BF16,)))


def _layer(x, conv_state, weights, attend, tm, conv_tiles):
    (norm_pre, norm_post, w_in, conv_w, wpa, wpc, wo) = weights
    b, t, d = x.shape
    m = b * t
    x2 = x.reshape(m, d)
    res, xn = {}, None
    for name, off, n, epilogue, dtypes in _proj_specs():
        first = xn is None
        outs = list(_proj(x2 if first else xn, w_in, off, n, epilogue, dtypes,
                          min(m, 512 if first else 1024), 1024, "proj_" + name,
                          gain=norm_pre if first else None))
        if first:
            xn = outs.pop(0)
        res[name] = outs
    tcv, new_conv = _conv_branch(xn.reshape(b, t, d), w_in, conv_state, conv_w, *conv_tiles)
    (k, kb), (v, vb) = res["k"], res["v"]
    og = attend(res["q"][0], kb, vb, res["za"][0])
    y = _merge_out(og, tcv.reshape(m, CONV_W), res["ga"][0], res["gc"][0],
                   wpa, wpc, wo, x2, norm_post.reshape(1, d), tm, MERGE_CHUNK)
    return (y.reshape(b, t, d), k.reshape(b, t, N_HEADS, 2 * D_QK), v.reshape(b, t, N_HEADS, D_V),
            new_conv)


def kernel(x_prompt, x_sample, cache_k, cache_v, state_conv, norm_pre, norm_post, w_in,
           lambda_q1, lambda_k1, lambda_q2, lambda_k2, head_norm, conv_w, w_proj_attn,
           w_proj_conv, w_out, rel_bias):
    depth = w_in.shape[0]
    past = cache_k.shape[2]
    dec_b, dec_t = x_sample.shape[0], x_sample.shape[1]
    xp, xs = x_prompt, x_sample
    zero_conv = jnp.zeros((xp.shape[0], CONV_K - 1, CONV_W), xp.dtype)
    pbias = _prompt_bias(rel_bias)
    outs = [[] for _ in range(6)]
    for l in range(depth):
        lam_init = 0.8 - 0.6 * math.exp(-0.3 * l)
        sbc, sbn, lam = _sample_bias_and_lambda(
            rel_bias, lambda_q1[l][None], lambda_k1[l][None], lambda_q2[l][None], lambda_k2[l][None],
            past, dec_t, lam_init)
        hn = head_norm[l].reshape(1, D_V)
        weights = (norm_pre[l], norm_post[l], w_in[l], conv_w[l],
                   _cast_chunks(w_proj_attn[l], MERGE_CHUNK), _cast_chunks(w_proj_conv[l], MERGE_CHUNK),
                   _cast_chunks(w_out[l], MERGE_CHUNK, rows=True))

        def attend_prompt(q, kb, vb, sza):
            return _prompt_attention(lam, q, kb, vb, pbias, sza, hn, lam_init, x_prompt.shape[1])

        ck = cache_k.reshape(depth * dec_b, past * N_HEADS, D_V)
        cv = cache_v.reshape(depth * dec_b, past * N_HEADS, D_V)

        def attend_sample(q, kb, vb, sza, l=l):
            return _sample_attention(lam, q, ck, cv, kb, vb, sbc, sbn, sza, hn, lam_init, l)

        xp, kp, vp, cp = _layer(xp, zero_conv, weights, attend_prompt, 512, (1, 512, 512))
        xs, ksm, vsm, csm = _layer(xs, state_conv[l], weights, attend_sample, 512,
                                   (dec_b, dec_t, 512))
        for lst, val in zip(outs, (kp, vp, cp, ksm, vsm, csm)):
            lst.append(val)
    return (xp, xs) + tuple(jnp.stack(o, axis=0) for o in outs)
```

```python
import functools
import math

import jax
import jax.numpy as jnp
from jax import lax
from jax.experimental import pallas as pl
from jax.experimental.pallas import tpu as pltpu

D_MODEL = 2048
N_HEADS = 8
D_QK = 64
D_V = 2 * D_QK
ATT_W = N_HEADS * D_V
CONV_W = D_MODEL // 2
CONV_K = 3
CHUNK = 64
NUM_BUCKETS = 32
MAX_DISTANCE = 128
EPS = 1e-6
MASKED = -1e30
LOG2E = math.log2(math.e)

OFF_Q, OFF_K, OFF_V, OFF_ZA = 0, 1024, 2048, 3072
OFF_BG, OFF_CG, OFF_H, OFF_ZC = 4096, 5120, 6144, 7168
OFF_GA, OFF_GC = 8192, 10240

ATT_BLOCK = 256
ATT_GROUP = 8
ATT_SLABS = 4
ATT_STRIP = 64
MERGE_CHUNK = 512
SAMPLE_KEYS = 256
V7X_VMEM_LIMIT = 56 * 1024 * 1024

F32 = jnp.float32
BF16 = jnp.bfloat16


def _params(sem, vmem=V7X_VMEM_LIMIT):
    return pltpu.CompilerParams(dimension_semantics=sem, vmem_limit_bytes=vmem)


def _bucket_py(rel):
    half = NUM_BUCKETS // 2
    max_exact = half // 2
    steps = half - max_exact
    n = abs(rel)
    if n < max_exact:
        v = n
    else:
        v = max_exact + sum(n ** steps * max_exact ** j >= MAX_DISTANCE ** j * max_exact ** steps
                            for j in range(1, steps))
    return v + (half if rel > 0 else 0)


NEAR = MAX_DISTANCE
FAR_BUCKET = _bucket_py(-NEAR)
_BREAKS = tuple((rel, _bucket_py(rel)) for rel in range(-NEAR + 1, NEAR + 1)
                if _bucket_py(rel) != _bucket_py(rel - 1))


def _bias_minus_far(rel, value_of, max_rel):
    val = value_of(FAR_BUCKET)
    for first_rel, bucket in _BREAKS:
        if first_rel > max_rel:
            break
        val = jnp.where(rel >= first_rel, value_of(bucket), val)
    return (val - value_of(FAR_BUCKET)) * LOG2E


_BIAS_ROWS = 32


def _pbias_kernel(rb_ref, pb_ref):
    h = pl.program_id(0)
    tb = ATT_BLOCK
    value_of = lambda b: rb_ref[b, h]
    for t in range(2):
        def rows(i, carry, t=t):
            r0 = pl.multiple_of(i * _BIAS_ROWS, _BIAS_ROWS)
            r = r0 + lax.broadcasted_iota(jnp.int32, (_BIAS_ROWS, tb), 0)
            c = lax.broadcasted_iota(jnp.int32, (_BIAS_ROWS, tb), 1)
            val = _bias_minus_far(c - r - (1 - t) * tb, value_of, max_rel=tb - 1 if t == 1 else -1)
            if t == 1:
                val = jnp.where(c // CHUNK <= r // CHUNK, val, MASKED)
            pb_ref[0, t, pl.ds(r0, _BIAS_ROWS), :] = val
            return carry

        lax.fori_loop(0, tb // _BIAS_ROWS, rows, 0)


def _prompt_bias(rel_bias):
    tb = ATT_BLOCK
    return pl.pallas_call(
        _pbias_kernel,
        out_shape=jax.ShapeDtypeStruct((N_HEADS, 2, tb, tb), F32),
        grid=(N_HEADS,),
        in_specs=[pl.BlockSpec(memory_space=pltpu.SMEM)],
        out_specs=pl.BlockSpec((1, 2, tb, tb), lambda h: (h, 0, 0, 0)),
        compiler_params=_params(("parallel",)),
        name="prompt_bias",
    )(rel_bias)


def _sbias_kernel(rb_ref, lq1_ref, lk1_ref, lq2_ref, lk2_ref, sbc_ref, sbn_ref, lam_ref,
                  *, past, lam_init):
    nq = sbn_ref.shape[2]

    def tile(shape, k0, value_of):
        qpos = past + lax.broadcasted_iota(jnp.int32, shape, 0) % nq
        kpos = k0 + lax.broadcasted_iota(jnp.int32, shape, 1)
        val = _bias_minus_far(kpos - qpos, value_of, max_rel=k0 + shape[1] - 1 - past)
        return jnp.where(kpos // CHUNK <= qpos // CHUNK, val, MASKED)

    for h in range(N_HEADS):
        value_of = lambda b, h=h: rb_ref[b, h]
        sbc_ref[h] = tile(sbc_ref.shape[1:], past - sbc_ref.shape[2], value_of)
        sbn_ref[h] = tile(sbn_ref.shape[1:], past, value_of)

    s1 = jnp.sum(lq1_ref[...].astype(F32) * lk1_ref[...].astype(F32), axis=-1, keepdims=True)
    s2 = jnp.sum(lq2_ref[...].astype(F32) * lk2_ref[...].astype(F32), axis=-1, keepdims=True)
    lam_ref[...] = jnp.exp(s1) - jnp.exp(s2) + lam_init


def _sample_bias_and_lambda(rel_bias, lq1, lk1, lq2, lk2, past, nq, lam_init):
    vmem = pl.BlockSpec(memory_space=pltpu.VMEM)
    return pl.pallas_call(
        functools.partial(_sbias_kernel, past=past, lam_init=lam_init),
        out_shape=(jax.ShapeDtypeStruct((N_HEADS, 2 * nq, NEAR), F32),
                   jax.ShapeDtypeStruct((N_HEADS, 2 * nq, nq), F32),
                   jax.ShapeDtypeStruct((1, 1), F32)),
        in_specs=[pl.BlockSpec(memory_space=pltpu.SMEM)] + [vmem] * 4,
        out_specs=(vmem, vmem, vmem),
        name="sample_bias_lambda",
    )(rel_bias, lq1, lk1, lq2, lk2)


def _silu(x):
    return x * jax.nn.sigmoid(x)


def _proj_kernel(*refs, epilogue, norm):
    refs = list(refs)
    x_ref = refs.pop(0)
    g_ref = refs.pop(0) if norm else None
    w_ref = refs.pop(0)
    xn_ref = refs.pop(0) if norm else None
    if norm:
        xf = x_ref[...].astype(F32)
        xf = xf * lax.rsqrt(jnp.mean(xf * xf, axis=-1, keepdims=True) + EPS)
        x = (xf * g_ref[...]).astype(BF16)
        xn_ref[...] = x
    else:
        x = x_ref[...]
    r = jnp.dot(x, w_ref[...].astype(BF16), preferred_element_type=F32)
    for o_ref, o in zip(refs, epilogue(r)):
        if len(o_ref.shape) == 3:
            for h in range(o_ref.shape[0]):
                o_ref[h] = o[:, h * D_V:(h + 1) * D_V].astype(o_ref.dtype)
        else:
            o_ref[...] = o.astype(o_ref.dtype)


def _proj(x, w, col_off, n, epilogue, out_dtypes, tm, tn, name, gain=None):
    m, d = x.shape
    jb = col_off // tn
    norm = gain is not None
    assert not norm or n == tn
    in_specs = [pl.BlockSpec((tm, d), lambda j, i: (i, 0))]
    out_shape, out_specs = [], []
    if norm:
        in_specs.append(pl.BlockSpec((1, d), lambda j, i: (0, 0)))
        out_shape.append(jax.ShapeDtypeStruct((m, d), BF16))
        out_specs.append(pl.BlockSpec((tm, d), lambda j, i: (i, 0)))
    in_specs.append(pl.BlockSpec((d, tn), lambda j, i: (0, jb + j)))
    for dt in out_dtypes:
        if isinstance(dt, tuple):
            out_shape.append(jax.ShapeDtypeStruct((n // D_V, m, D_V), dt[0]))
            out_specs.append(pl.BlockSpec((tn // D_V, tm, D_V), lambda j, i: (j, i, 0)))
        else:
            out_shape.append(jax.ShapeDtypeStruct((m, n), dt))
            out_specs.append(pl.BlockSpec((tm, tn), lambda j, i: (i, j)))
    args = (x, gain.reshape(1, d), w) if norm else (x, w)
    return pl.pallas_call(
        functools.partial(_proj_kernel, epilogue=epilogue, norm=norm),
        out_shape=tuple(out_shape),
        grid=(n // tn, m // tm),
        in_specs=in_specs,
        out_specs=tuple(out_specs),
        compiler_params=_params(("parallel", "parallel")),
        name=name,
    )(*args)


def _conv_kernel(x_ref, wb_ref, wc_ref, wh_ref, wz_ref, st_ref, cw_ref,
                 t_ref, nc_ref, carry_ref):
    ti = pl.program_id(2)

    @pl.when(ti == 0)
    def _():
        carry_ref[:, 6:8, :] = st_ref[...]

    tb, tt, d = x_ref.shape
    x = x_ref[...].reshape(tb * tt, d)

    def mm(w_ref):
        return jnp.dot(x, w_ref[...].astype(BF16), preferred_element_type=F32).reshape(tb, tt, -1)

    u = mm(wc_ref) * mm(wh_ref)
    c0 = carry_ref[:, 6:7, :]
    c1 = carry_ref[:, 7:8, :]
    row = lax.broadcasted_iota(jnp.int32, u.shape, 1)
    u1 = jnp.where(row == 0, c1, pltpu.roll(u, 1, 1))
    u2 = jnp.where(row == 0, c0, jnp.where(row == 1, c1, pltpu.roll(u, 2, 1)))
    y = cw_ref[0:1, :] * u2 + cw_ref[1:2, :] * u1 + cw_ref[2:3, :] * u
    carry_ref[...] = u[:, tt - 8:, :]
    nc_ref[...] = carry_ref[:, 6:8, :]
    t_ref[...] = (mm(wb_ref) * y * _silu(mm(wz_ref))).astype(t_ref.dtype)


def _conv_branch(xn3, w, state, conv_w, tb, tt, tc):
    b, t, d = xn3.shape
    wspec = lambda off: pl.BlockSpec((d, tc), lambda c, bi, ti, o=off // tc: (0, o + c))
    return pl.pallas_call(
        _conv_kernel,
        out_shape=(jax.ShapeDtypeStruct((b, t, CONV_W), BF16),
                   jax.ShapeDtypeStruct((b, CONV_K - 1, CONV_W), F32)),
        grid=(CONV_W // tc, b // tb, t // tt),
        in_specs=[pl.BlockSpec((tb, tt, d), lambda c, bi, ti: (bi, ti, 0)),
                  wspec(OFF_BG), wspec(OFF_CG), wspec(OFF_H), wspec(OFF_ZC),
                  pl.BlockSpec((tb, CONV_K - 1, tc), lambda c, bi, ti: (bi, 0, c)),
                  pl.BlockSpec((CONV_K, tc), lambda c, bi, ti: (0, c))],
        out_specs=(pl.BlockSpec((tb, tt, tc), lambda c, bi, ti: (bi, ti, c)),
                   pl.BlockSpec((tb, CONV_K - 1, tc), lambda c, bi, ti: (bi, 0, c))),
        scratch_shapes=[pltpu.VMEM((tb, 8, tc), F32)],
        compiler_params=_params(("parallel", "parallel", "arbitrary")),
        name="conv_branch",
    )(xn3, w, w, w, w, state, conv_w)


_NT = (((1,), (1,)), ((), ()))


def _head_out(o, hn, sza, lam_init):
    o = o * lax.rsqrt(jnp.mean(o * o, axis=-1, keepdims=True) + EPS)
    return o * hn * (1.0 - lam_init) * sza


def _pattn_kernel(lam_ref, q_ref, k_ref, v_ref, bias_ref, sza_ref, hn_ref, o_ref,
                  s_ref, p_ref, *, lam_init):
    gi = pl.program_id(2)
    tb = ATT_BLOCK
    hw = tb // 2

    def block_tasks(n, r):
        rows = slice(r * tb, (r + 1) * tb)
        q = q_ref[0, rows, :]
        lane = lax.broadcasted_iota(jnp.int32, q.shape, 1)
        zero = jnp.zeros_like(q)
        qmaps = (jnp.where(lane < D_QK, q, zero), jnp.where(lane >= D_QK, q, zero))
        chunks = [slice(c * hw, (c + 1) * hw) for c in range(2 * (n + 1))]
        sums = [[], []]
        outs = [None, None]
        slab_of = lambda mi: (2 * r + mi) % ATT_SLABS

        def scores(mi):
            for j in range(n + 1):
                s = lax.dot_general(qmaps[mi], k_ref[0, j * tb:(j + 1) * tb, :], _NT,
                                    preferred_element_type=F32)
                if j == n:
                    s = s + bias_ref[0, 1]
                elif j == n - 1:
                    s = s + bias_ref[0, 0]
                s_ref[slab_of(mi), :, j * tb:(j + 1) * tb] = s
                yield

        def numerator(mi):
            slab = slab_of(mi)
            for st in range(tb // ATT_STRIP):
                srows = slice(st * ATT_STRIP, (st + 1) * ATT_STRIP)
                mpart = functools.reduce(jnp.maximum, [s_ref[slab, srows, c] for c in chunks])
                m = jnp.broadcast_to(jnp.max(mpart, axis=-1, keepdims=True), (ATT_STRIP, hw))
                yield
                lpart = None
                for c in chunks:
                    e = jnp.exp2(s_ref[slab, srows, c] - m)
                    p_ref[slab, srows, c] = e.astype(BF16)
                    lpart = e if lpart is None else lpart + e
                    yield
                sums[mi].append(jnp.sum(lpart, axis=-1, keepdims=True))

        def values(mi):
            o = jnp.dot(p_ref[slab_of(mi), :, :(n + 1) * tb], v_ref[0, :(n + 1) * tb, :],
                        preferred_element_type=F32)
            outs[mi] = o / jnp.concatenate(sums[mi], axis=0)
            if mi == 1:
                o = outs[0] - lam_ref[0, 0] * outs[1]
                o_ref[0, rows, :] = _head_out(o, hn_ref[...], sza_ref[0, rows, :].astype(F32),
                                              lam_init).astype(o_ref.dtype)
            yield

        return [(scores(mi), n + 1, numerator(mi), (tb // ATT_STRIP) * (1 + len(chunks)), values(mi))
                for mi in range(2)]

    def emit_group(g):
        units = [u for r in range(ATT_GROUP) for u in block_tasks(g * ATT_GROUP + r, r)]
        for t in range(len(units) + 2):
            if t >= 2:
                next(units[t - 2][4])
            a, na = (units[t][0], units[t][1]) if t < len(units) else (iter(()), 0)
            b, nb = (units[t - 1][2], units[t - 1][3]) if 1 <= t <= len(units) else (iter(()), 0)
            da = db = 0
            while da < na or db < nb:
                if db >= nb or (da < na and da * nb <= db * na):
                    next(a)
                    da += 1
                else:
                    next(b)
                    db += 1
            for rest in (a, b):
                for _ in rest:
                    pass

    for g in range(k_ref.shape[1] // (tb * ATT_GROUP)):
        pl.when(gi == g)(functools.partial(emit_group, g))


def _prompt_attention(lam, q, kb, vb, pbias, sza, head_norm, lam_init, t):
    _, m, _ = q.shape
    b = m // t
    tb = ATT_BLOCK
    tg = tb * ATT_GROUP
    assert tb >= NEAR and t % tg == 0
    blk = pl.BlockSpec((1, tg, D_V), lambda bi, h, gi: (h, bi * (t // tg) + gi, 0))
    full = pl.BlockSpec((1, t, D_V), lambda bi, h, gi: (h, bi, 0))
    return pl.pallas_call(
        functools.partial(_pattn_kernel, lam_init=lam_init),
        out_shape=jax.ShapeDtypeStruct((N_HEADS, m, D_V), BF16),
        grid=(b, N_HEADS, t // tg),
        in_specs=[pl.BlockSpec(memory_space=pltpu.SMEM),
                  blk, full, full,
                  pl.BlockSpec((1, 2, tb, tb), lambda bi, h, gi: (h, 0, 0, 0)),
                  blk,
                  pl.BlockSpec((1, D_V), lambda bi, h, gi: (0, 0))],
        out_specs=blk,
        scratch_shapes=[pltpu.VMEM((ATT_SLABS, tb, t), F32),
                        pltpu.VMEM((ATT_SLABS, tb, t), BF16)],
        compiler_params=_params(("parallel", "parallel", "arbitrary")),
        name="prompt_attention",
    )(lam, q, kb, vb, pbias, sza, head_norm)


def _sattn_kernel(lam_ref, q_ref, kc_ref, vc_ref, kn_ref, vn_ref, bc_ref, bn_ref,
                  sza_ref, hn_ref, o_ref, s_ref, vt_ref, *, lam_init):
    nq = q_ref.shape[1]
    past = kc_ref.shape[1] // N_HEADS
    far_n = past - bc_ref.shape[2]
    lam = lam_ref[0, 0]
    ck = SAMPLE_KEYS
    nchunk = past // ck

    def q2m_of(h):
        q = q_ref[h]
        lane = lax.broadcasted_iota(jnp.int32, q.shape, 1)
        zero = jnp.zeros_like(q)
        return jnp.concatenate([jnp.where(lane < D_QK, q, zero), jnp.where(lane >= D_QK, q, zero)], axis=0)

    q2m = [q2m_of(h) for h in range(N_HEADS)]

    def heads_major(ref, c):
        rows = ref[0, c * ck * N_HEADS:(c + 1) * ck * N_HEADS, :]
        return jnp.swapaxes(rows.reshape(ck, N_HEADS, D_V), 0, 1).astype(BF16)

    def scores(c):
        kt = heads_major(kc_ref, c)
        yield
        for h in range(N_HEADS):
            s_ref[h, :, c * ck:(c + 1) * ck] = lax.dot_general(q2m[h], kt[h], _NT,
                                                               preferred_element_type=F32)
            yield

    def values(c):
        vt = heads_major(vc_ref, c)
        yield
        for h in range(N_HEADS):
            vt_ref[h, c * ck:(c + 1) * ck, :] = vt[h]
        yield

    def head(h):
        s = s_ref[h]
        s = jnp.concatenate([s[:, :far_n], s[:, far_n:] + bc_ref[h]], axis=1)
        sn = lax.dot_general(q2m[h], kn_ref[h], _NT, preferred_element_type=F32) + bn_ref[h]
        mx = jnp.maximum(jnp.max(s, axis=-1, keepdims=True), jnp.max(sn, axis=-1, keepdims=True))
        e = jnp.exp2(s - mx)
        en = jnp.exp2(sn - mx)
        denom = jnp.sum(e, axis=-1, keepdims=True) + jnp.sum(en, axis=-1, keepdims=True)
        row = lax.broadcasted_iota(jnp.int32, denom.shape, 0)
        w = jnp.where(row < nq, 1.0, -lam) / denom

        def combine(x):
            p = x * w
            return (p[:nq] + p[nq:]).astype(BF16)

        a, an = combine(e), combine(en)
        yield
        o = (jnp.dot(a, vt_ref[h], preferred_element_type=F32)
             + jnp.dot(an, vn_ref[h], preferred_element_type=F32))
        o_ref[h] = _head_out(o, hn_ref[...], sza_ref[h].astype(F32), lam_init).astype(o_ref.dtype)
        yield

    def together(*gens):
        gens = list(gens)
        while gens:
            gens = [g for g in gens if next(g, StopIteration) is not StopIteration]

    def first_step(g):
        next(g, None)
        yield

    per_chunk = [scores(c) for c in range(nchunk)]
    next(per_chunk[0])
    for c in range(nchunk):
        ahead = [first_step(per_chunk[c + 1])] if c + 1 < nchunk else []
        together(per_chunk[c], values(c), *ahead)
    for h in range(0, N_HEADS, 2):
        together(head(h), head(h + 1))


def _sample_attention(lam, q, cache_k, cache_v, kb, vb, sbc, sbn, sza, head_norm, lam_init, layer):
    nq = sbn.shape[2]
    b = q.shape[1] // nq
    rows = cache_k.shape[1]
    new = pl.BlockSpec((N_HEADS, nq, D_V), lambda bi: (0, bi, 0))
    cache = pl.BlockSpec((1, rows, D_V), lambda bi: (layer * b + bi, 0, 0))
    whole = lambda a: pl.BlockSpec(a.shape, lambda bi: (0,) * a.ndim)
    return pl.pallas_call(
        functools.partial(_sattn_kernel, lam_init=lam_init),
        out_shape=jax.ShapeDtypeStruct(q.shape, BF16),
        grid=(b,),
        in_specs=[pl.BlockSpec(memory_space=pltpu.SMEM),
                  new, cache, cache, new, new, whole(sbc), whole(sbn), new, whole(head_norm)],
        out_specs=new,
        scratch_shapes=[pltpu.VMEM((N_HEADS, 2 * nq, rows // N_HEADS), F32),
                        pltpu.VMEM((N_HEADS, rows // N_HEADS, D_V), BF16)],
        compiler_params=_params(("parallel",)),
        name="sample_attention",
    )(lam, q, cache_k, cache_v, kb, vb, sbc, sbn, sza, head_norm)


def _out_kernel(og_ref, t_ref, sga_ref, sgc_ref, wpa_ref, wpc_ref, wo_ref, x_ref, g_ref,
                o_ref, m_ref):
    c = pl.program_id(1)
    nchunk, _, tc = m_ref.shape
    og = jnp.concatenate([og_ref[h] for h in range(og_ref.shape[0])], axis=1)
    ya = jnp.dot(og, wpa_ref[c], preferred_element_type=F32)
    yc = jnp.dot(t_ref[...], wpc_ref[c], preferred_element_type=F32)
    merged = sga_ref[...].astype(F32) * ya + sgc_ref[...].astype(F32) * yc
    m_ref[c] = merged.astype(BF16)

    @pl.when(c == nchunk - 1)
    def _():
        y = jnp.dot(m_ref[0], wo_ref[0], preferred_element_type=F32)
        for cc in range(1, nchunk):
            y = y + jnp.dot(m_ref[cc], wo_ref[cc], preferred_element_type=F32)
        y = y * lax.rsqrt(jnp.mean(y * y, axis=-1, keepdims=True) + EPS)
        o_ref[...] = x_ref[...] + y * g_ref[...]


def _cast_chunks_kernel(w_ref, o_ref):
    o_ref[0] = w_ref[...].astype(o_ref.dtype)


def _cast_chunks(w, tc, rows=False):
    k, n = w.shape
    blk, imap = ((tc, n), lambda j: (j, 0)) if rows else ((k, tc), lambda j: (0, j))
    nchunk = (k if rows else n) // tc
    return pl.pallas_call(
        _cast_chunks_kernel,
        out_shape=jax.ShapeDtypeStruct((nchunk,) + blk, BF16),
        grid=(nchunk,),
        in_specs=[pl.BlockSpec(blk, imap)],
        out_specs=pl.BlockSpec((1,) + blk, lambda j: (j, 0, 0)),
        compiler_params=_params(("parallel",)),
        name="cast_chunks",
    )(w)


def _merge_out(og, t, sga, sgc, wpa, wpc, wo, x, g, tm, tc):
    m, d = x.shape
    resident = lambda a: pl.BlockSpec(a.shape, lambda i, c: (0, 0, 0), pipeline_mode=pl.Buffered(1))
    return pl.pallas_call(
        _out_kernel,
        out_shape=jax.ShapeDtypeStruct((m, d), x.dtype),
        grid=(m // tm, d // tc),
        in_specs=[pl.BlockSpec((N_HEADS, tm, D_V), lambda i, c: (0, i, 0)),
                  pl.BlockSpec((tm, CONV_W), lambda i, c: (i, 0)),
                  pl.BlockSpec((tm, tc), lambda i, c: (i, c)),
                  pl.BlockSpec((tm, tc), lambda i, c: (i, c)),
                  resident(wpa), resident(wpc), resident(wo),
                  pl.BlockSpec((tm, d), lambda i, c: (i, 0)),
                  pl.BlockSpec((1, d), lambda i, c: (0, 0))],
        out_specs=pl.BlockSpec((tm, d), lambda i, c: (i, 0)),
        scratch_shapes=[pltpu.VMEM((d // tc, tm, tc), BF16)],
        compiler_params=_params(("parallel", "arbitrary")),
        name="merge_out",
    )(og, t, sga, sgc, wpa, wpc, wo, x, g)


def _proj_specs():
    scale = LOG2E / math.sqrt(D_QK)
    sigmoid = lambda r: (jax.nn.sigmoid(r),)
    heads = (BF16, "heads")
    return (("q", OFF_Q, ATT_W, lambda r: (r * scale,), (heads,)),
            ("k", OFF_K, ATT_W, lambda r: (r, r), (F32, heads)),
            ("v", OFF_V, ATT_W, lambda r: (r, r), (F32, heads)),
            ("za", OFF_ZA, ATT_W, lambda r: (_silu(r),), (heads,)),
            ("ga", OFF_GA, D_MODEL, sigmoid, (BF16,)),
            ("gc", OFF_GC, D_MODEL, sigmoid, (BF16,)))


def _layer(x, conv_state, weights, attend, tm, conv_tiles):
    (norm_pre, norm_post, w_in, conv_w, wpa, wpc, wo) = weights
    b, t, d = x.shape
    m = b * t
    x2 = x.reshape(m, d)
    res, xn = {}, None
    for name, off, n, epilogue, dtypes in _proj_specs():
        first = xn is None
        outs = list(_proj(x2 if first else xn, w_in, off, n, epilogue, dtypes,
                          min(m, 512 if first else 1024), 1024, "proj_" + name,
                          gain=norm_pre if first else None))
        if first:
            xn = outs.pop(0)
        res[name] = outs
    tcv, new_conv = _conv_branch(xn.reshape(b, t, d), w_in, conv_state, conv_w, *conv_tiles)
    (k, kb), (v, vb) = res["k"], res["v"]
    og = attend(res["q"][0], kb, vb, res["za"][0])
    y = _merge_out(og, tcv.reshape(m, CONV_W), res["ga"][0], res["gc"][0],
                   wpa, wpc, wo, x2, norm_post.reshape(1, d), tm, MERGE_CHUNK)
    return (y.reshape(b, t, d), k.reshape(b, t, N_HEADS, 2 * D_QK), v.reshape(b, t, N_HEADS, D_V),
            new_conv)


def kernel(x_prompt, x_sample, cache_k, cache_v, state_conv, norm_pre, norm_post, w_in,
           lambda_q1, lambda_k1, lambda_q2, lambda_k2, head_norm, conv_w, w_proj_attn,
           w_proj_conv, w_out, rel_bias):
    depth = w_in.shape[0]
    past = cache_k.shape[2]
    dec_b, dec_t = x_sample.shape[0], x_sample.shape[1]
    xp, xs = x_prompt, x_sample
    zero_conv = jnp.zeros((xp.shape[0], CONV_K - 1, CONV_W), xp.dtype)
    pbias = _prompt_bias(rel_bias)
    outs = [[] for _ in range(6)]
    for l in range(depth):
        lam_init = 0.8 - 0.6 * math.exp(-0.3 * l)
        sbc, sbn, lam = _sample_bias_and_lambda(
            rel_bias, lambda_q1[l][None], lambda_k1[l][None], lambda_q2[l][None], lambda_k2[l][None],
            past, dec_t, lam_init)
        hn = head_norm[l].reshape(1, D_V)
        weights = (norm_pre[l], norm_post[l], w_in[l], conv_w[l],
                   _cast_chunks(w_proj_attn[l], MERGE_CHUNK), _cast_chunks(w_proj_conv[l], MERGE_CHUNK),
                   _cast_chunks(w_out[l], MERGE_CHUNK, rows=True))

        def attend_prompt(q, kb, vb, sza):
            return _prompt_attention(lam, q, kb, vb, pbias, sza, hn, lam_init, x_prompt.shape[1])

        ck = cache_k.reshape(depth * dec_b, past * N_HEADS, D_V)
        cv = cache_v.reshape(depth * dec_b, past * N_HEADS, D_V)

        def attend_sample(q, kb, vb, sza, l=l):
            return _sample_attention(lam, q, ck, cv, kb, vb, sbc, sbn, sza, hn, lam_init, l)

        xp, kp, vp, cp = _layer(xp, zero_conv, weights, attend_prompt, 512, (1, 512, 512))
        xs, ksm, vsm, csm = _layer(xs, state_conv[l], weights, attend_sample, 512,
                                   (dec_b, dec_t, 512))
        for lst, val in zip(outs, (kp, vp, cp, ksm, vsm, csm)):
            lst.append(val)
    return (xp, xs) + tuple(jnp.stack(o, axis=0) for o in outs)
```

```python
import functools
import math

import jax
import jax.numpy as jnp
from jax import lax
from jax.experimental import pallas as pl
from jax.experimental.pallas import tpu as pltpu

D_MODEL = 2048
N_HEADS = 8
D_QK = 64
D_V = 2 * D_QK
ATT_W = N_HEADS * D_V
CONV_W = D_MODEL // 2
CONV_K = 3
CHUNK = 64
NUM_BUCKETS = 32
MAX_DISTANCE = 128
EPS = 1e-6
MASKED = -1e30
LOG2E = math.log2(math.e)

OFF_Q, OFF_K, OFF_V, OFF_ZA = 0, 1024, 2048, 3072
OFF_BG, OFF_CG, OFF_H, OFF_ZC = 4096, 5120, 6144, 7168
OFF_GA, OFF_GC = 8192, 10240

ATT_BLOCK = 256
ATT_GROUP = 8
ATT_SLABS = 4
ATT_STRIP = 64
MERGE_CHUNK = 512
SAMPLE_KEYS = 256
V7X_VMEM_LIMIT = 56 * 1024 * 1024

F32 = jnp.float32
BF16 = jnp.bfloat16


def _params(sem, vmem=V7X_VMEM_LIMIT):
    return pltpu.CompilerParams(dimension_semantics=sem, vmem_limit_bytes=vmem)


def _bucket_py(rel):
    half = NUM_BUCKETS // 2
    max_exact = half // 2
    steps = half - max_exact
    n = abs(rel)
    if n < max_exact:
        v = n
    else:
        v = max_exact + sum(n ** steps * max_exact ** j >= MAX_DISTANCE ** j * max_exact ** steps
                            for j in range(1, steps))
    return v + (half if rel > 0 else 0)


NEAR = MAX_DISTANCE
FAR_BUCKET = _bucket_py(-NEAR)
_BREAKS = tuple((rel, _bucket_py(rel)) for rel in range(-NEAR + 1, NEAR + 1)
                if _bucket_py(rel) != _bucket_py(rel - 1))


def _bias_minus_far(rel, value_of, max_rel):
    val = value_of(FAR_BUCKET)
    for first_rel, bucket in _BREAKS:
        if first_rel > max_rel:
            break
        val = jnp.where(rel >= first_rel, value_of(bucket), val)
    return (val - value_of(FAR_BUCKET)) * LOG2E


_BIAS_ROWS = 32


def _pbias_kernel(rb_ref, pb_ref):
    h = pl.program_id(0)
    tb = ATT_BLOCK
    value_of = lambda b: rb_ref[b, h]
    for t in range(2):
        def rows(i, carry, t=t):
            r0 = pl.multiple_of(i * _BIAS_ROWS, _BIAS_ROWS)
            r = r0 + lax.broadcasted_iota(jnp.int32, (_BIAS_ROWS, tb), 0)
            c = lax.broadcasted_iota(jnp.int32, (_BIAS_ROWS, tb), 1)
            val = _bias_minus_far(c - r - (1 - t) * tb, value_of, max_rel=tb - 1 if t == 1 else -1)
            if t == 1:
                val = jnp.where(c // CHUNK <= r // CHUNK, val, MASKED)
            pb_ref[0, t, pl.ds(r0, _BIAS_ROWS), :] = val
            return carry

        lax.fori_loop(0, tb // _BIAS_ROWS, rows, 0)


def _prompt_bias(rel_bias):
    tb = ATT_BLOCK
    return pl.pallas_call(
        _pbias_kernel,
        out_shape=jax.ShapeDtypeStruct((N_HEADS, 2, tb, tb), F32),
        grid=(N_HEADS,),
        in_specs=[pl.BlockSpec(memory_space=pltpu.SMEM)],
        out_specs=pl.BlockSpec((1, 2, tb, tb), lambda h: (h, 0, 0, 0)),
        compiler_params=_params(("parallel",)),
        name="prompt_bias",
    )(rel_bias)


def _sbias_kernel(rb_ref, lq1_ref, lk1_ref, lq2_ref, lk2_ref, sbc_ref, sbn_ref, lam_ref,
                  *, past, lam_init):
    nq = sbn_ref.shape[2]

    def tile(shape, k0, value_of):
        qpos = past + lax.broadcasted_iota(jnp.int32, shape, 0) % nq
        kpos = k0 + lax.broadcasted_iota(jnp.int32, shape, 1)
        val = _bias_minus_far(kpos - qpos, value_of, max_rel=k0 + shape[1] - 1 - past)
        return jnp.where(kpos // CHUNK <= qpos // CHUNK, val, MASKED)

    for h in range(N_HEADS):
        value_of = lambda b, h=h: rb_ref[b, h]
        sbc_ref[h] = tile(sbc_ref.shape[1:], past - sbc_ref.shape[2], value_of)
        sbn_ref[h] = tile(sbn_ref.shape[1:], past, value_of)

    s1 = jnp.sum(lq1_ref[...].astype(F32) * lk1_ref[...].astype(F32), axis=-1, keepdims=True)
    s2 = jnp.sum(lq2_ref[...].astype(F32) * lk2_ref[...].astype(F32), axis=-1, keepdims=True)
    lam_ref[...] = jnp.exp(s1) - jnp.exp(s2) + lam_init


def _sample_bias_and_lambda(rel_bias, lq1, lk1, lq2, lk2, past, nq, lam_init):
    vmem = pl.BlockSpec(memory_space=pltpu.VMEM)
    return pl.pallas_call(
        functools.partial(_sbias_kernel, past=past, lam_init=lam_init),
        out_shape=(jax.ShapeDtypeStruct((N_HEADS, 2 * nq, NEAR), F32),
                   jax.ShapeDtypeStruct((N_HEADS, 2 * nq, nq), F32),
                   jax.ShapeDtypeStruct((1, 1), F32)),
        in_specs=[pl.BlockSpec(memory_space=pltpu.SMEM)] + [vmem] * 4,
        out_specs=(vmem, vmem, vmem),
        name="sample_bias_lambda",
    )(rel_bias, lq1, lk1, lq2, lk2)


def _silu(x):
    return x * jax.nn.sigmoid(x)


def _proj_kernel(*refs, epilogue, n_first):
    refs = list(refs)
    if n_first is None:
        x = refs.pop(0)[...]
        w_ref = refs.pop(0)
    else:
        xa_ref, xb_ref, g_ref, w_ref, xn_ref = (refs.pop(0) for _ in range(5))

        def prenorm(x_ref):
            xf = x_ref[...].astype(F32)
            xf = xf * lax.rsqrt(jnp.mean(xf * xf, axis=-1, keepdims=True) + EPS)
            xn_ref[...] = (xf * g_ref[...]).astype(BF16)

        i = pl.program_id(1)
        pl.when(i < n_first)(functools.partial(prenorm, xa_ref))
        pl.when(i >= n_first)(functools.partial(prenorm, xb_ref))
        x = xn_ref[...]
    r = jnp.dot(x, w_ref[...].astype(BF16), preferred_element_type=F32)
    for o_ref, o in zip(refs, epilogue(r)):
        if len(o_ref.shape) == 3:
            for h in range(o_ref.shape[0]):
                o_ref[h] = o[:, h * D_V:(h + 1) * D_V].astype(o_ref.dtype)
        else:
            o_ref[...] = o.astype(o_ref.dtype)


def _proj(x, w, col_off, n, epilogue, out_dtypes, tm, tn, name, gain=None):
    jb = col_off // tn
    out_shape, out_specs = [], []
    if gain is None:
        m, d = x.shape
        n_first = None
        in_specs = [pl.BlockSpec((tm, d), lambda j, i: (i, 0))]
        args = (x, w)
    else:
        xa, xb = x
        d = xa.shape[1]
        m = xa.shape[0] + xb.shape[0]
        n_first = xa.shape[0] // tm
        assert n == tn and xa.shape[0] % tm == 0 and xb.shape[0] % tm == 0
        in_specs = [pl.BlockSpec((tm, d), lambda j, i: (jnp.minimum(i, n_first - 1), 0)),
                    pl.BlockSpec((tm, d), lambda j, i: (jnp.maximum(i - n_first, 0), 0)),
                    pl.BlockSpec((1, d), lambda j, i: (0, 0))]
        out_shape.append(jax.ShapeDtypeStruct((m, d), BF16))
        out_specs.append(pl.BlockSpec((tm, d), lambda j, i: (i, 0)))
        args = (xa, xb, gain.reshape(1, d), w)
    in_specs.append(pl.BlockSpec((d, tn), lambda j, i: (0, jb + j)))
    for dt in out_dtypes:
        if isinstance(dt, tuple):
            out_shape.append(jax.ShapeDtypeStruct((n // D_V, m, D_V), dt[0]))
            out_specs.append(pl.BlockSpec((tn // D_V, tm, D_V), lambda j, i: (j, i, 0)))
        else:
            out_shape.append(jax.ShapeDtypeStruct((m, n), dt))
            out_specs.append(pl.BlockSpec((tm, tn), lambda j, i: (i, j)))
    return pl.pallas_call(
        functools.partial(_proj_kernel, epilogue=epilogue, n_first=n_first),
        out_shape=tuple(out_shape),
        grid=(n // tn, m // tm),
        in_specs=in_specs,
        out_specs=tuple(out_specs),
        compiler_params=_params(("parallel", "parallel")),
        name=name,
    )(*args)


def _conv_kernel(x_ref, wb_ref, wc_ref, wh_ref, wz_ref, st_ref, cw_ref,
                 t_ref, nc_ref, carry_ref):
    ti = pl.program_id(2)

    @pl.when(ti == 0)
    def _():
        carry_ref[:, 6:8, :] = st_ref[...]

    tb, tt, _ = t_ref.shape
    x = x_ref[...]

    def mm(w_ref):
        return jnp.dot(x, w_ref[...].astype(BF16), preferred_element_type=F32).reshape(tb, tt, -1)

    u = mm(wc_ref) * mm(wh_ref)
    c0 = carry_ref[:, 6:7, :]
    c1 = carry_ref[:, 7:8, :]
    row = lax.broadcasted_iota(jnp.int32, u.shape, 1)
    u1 = jnp.where(row == 0, c1, pltpu.roll(u, 1, 1))
    u2 = jnp.where(row == 0, c0, jnp.where(row == 1, c1, pltpu.roll(u, 2, 1)))
    y = cw_ref[0:1, :] * u2 + cw_ref[1:2, :] * u1 + cw_ref[2:3, :] * u
    carry_ref[...] = u[:, tt - 8:, :]
    nc_ref[...] = carry_ref[:, 6:8, :]
    t_ref[...] = (mm(wb_ref) * y * _silu(mm(wz_ref))).astype(t_ref.dtype)


def _conv_branch(xn, row0, b, t, w, state, conv_w, tb, tt, tc):
    d = xn.shape[1]
    assert row0 % (tb * tt) == 0 and (tb == 1 or tt == t)
    wspec = lambda off: pl.BlockSpec((d, tc), lambda c, bi, ti, o=off // tc: (0, o + c))
    first = row0 // (tb * tt)
    return pl.pallas_call(
        _conv_kernel,
        out_shape=(jax.ShapeDtypeStruct((b, t, CONV_W), BF16),
                   jax.ShapeDtypeStruct((b, CONV_K - 1, CONV_W), F32)),
        grid=(CONV_W // tc, b // tb, t // tt),
        in_specs=[pl.BlockSpec((tb * tt, d), lambda c, bi, ti: (first + bi * (t // tt) + ti, 0)),
                  wspec(OFF_BG), wspec(OFF_CG), wspec(OFF_H), wspec(OFF_ZC),
                  pl.BlockSpec((tb, CONV_K - 1, tc), lambda c, bi, ti: (bi, 0, c)),
                  pl.BlockSpec((CONV_K, tc), lambda c, bi, ti: (0, c))],
        out_specs=(pl.BlockSpec((tb, tt, tc), lambda c, bi, ti: (bi, ti, c)),
                   pl.BlockSpec((tb, CONV_K - 1, tc), lambda c, bi, ti: (bi, 0, c))),
        scratch_shapes=[pltpu.VMEM((tb, 8, tc), F32)],
        compiler_params=_params(("parallel", "parallel", "arbitrary")),
        name="conv_branch",
    )(xn, w, w, w, w, state, conv_w)


_NT = (((1,), (1,)), ((), ()))


def _head_out(o, hn, sza, lam_init):
    o = o * lax.rsqrt(jnp.mean(o * o, axis=-1, keepdims=True) + EPS)
    return o * hn * (1.0 - lam_init) * sza


def _pattn_kernel(lam_ref, q_ref, k_ref, v_ref, bias_ref, sza_ref, hn_ref, o_ref,
                  s_ref, p_ref, *, lam_init):
    gi = pl.program_id(2)
    tb = ATT_BLOCK
    hw = tb // 2

    def block_tasks(n, r):
        rows = slice(r * tb, (r + 1) * tb)
        q = q_ref[0, rows, :]
        lane = lax.broadcasted_iota(jnp.int32, q.shape, 1)
        zero = jnp.zeros_like(q)
        qmaps = (jnp.where(lane < D_QK, q, zero), jnp.where(lane >= D_QK, q, zero))
        chunks = [slice(c * hw, (c + 1) * hw) for c in range(2 * (n + 1))]
        sums = [[], []]
        outs = [None, None]
        slab_of = lambda mi: (2 * r + mi) % ATT_SLABS

        def scores(mi):
            for j in range(n + 1):
                s = lax.dot_general(qmaps[mi], k_ref[0, j * tb:(j + 1) * tb, :], _NT,
                                    preferred_element_type=F32)
                if j == n:
                    s = s + bias_ref[0, 1]
                elif j == n - 1:
                    s = s + bias_ref[0, 0]
                s_ref[slab_of(mi), :, j * tb:(j + 1) * tb] = s
                yield

        def numerator(mi):
            slab = slab_of(mi)
            for st in range(tb // ATT_STRIP):
                srows = slice(st * ATT_STRIP, (st + 1) * ATT_STRIP)
                mpart = functools.reduce(jnp.maximum, [s_ref[slab, srows, c] for c in chunks])
                m = jnp.broadcast_to(jnp.max(mpart, axis=-1, keepdims=True), (ATT_STRIP, hw))
                yield
                lpart = None
                for c in chunks:
                    e = jnp.exp2(s_ref[slab, srows, c] - m)
                    p_ref[slab, srows, c] = e.astype(BF16)
                    lpart = e if lpart is None else lpart + e
                    yield
                sums[mi].append(jnp.sum(lpart, axis=-1, keepdims=True))

        def values(mi):
            o = jnp.dot(p_ref[slab_of(mi), :, :(n + 1) * tb], v_ref[0, :(n + 1) * tb, :],
                        preferred_element_type=F32)
            outs[mi] = o / jnp.concatenate(sums[mi], axis=0)
            if mi == 1:
                o = outs[0] - lam_ref[0, 0] * outs[1]
                o_ref[0, rows, :] = _head_out(o, hn_ref[...], sza_ref[0, rows, :].astype(F32),
                                              lam_init).astype(o_ref.dtype)
            yield

        return [(scores(mi), n + 1, numerator(mi), (tb // ATT_STRIP) * (1 + len(chunks)), values(mi))
                for mi in range(2)]

    def emit_group(g):
        units = [u for r in range(ATT_GROUP) for u in block_tasks(g * ATT_GROUP + r, r)]
        for t in range(len(units) + 2):
            if t >= 2:
                next(units[t - 2][4])
            a, na = (units[t][0], units[t][1]) if t < len(units) else (iter(()), 0)
            b, nb = (units[t - 1][2], units[t - 1][3]) if 1 <= t <= len(units) else (iter(()), 0)
            da = db = 0
            while da < na or db < nb:
                if db >= nb or (da < na and da * nb <= db * na):
                    next(a)
                    da += 1
                else:
                    next(b)
                    db += 1
            for rest in (a, b):
                for _ in rest:
                    pass

    for g in range(k_ref.shape[1] // (tb * ATT_GROUP)):
        pl.when(gi == g)(functools.partial(emit_group, g))


def _prompt_attention(lam, q, kb, vb, pbias, sza, head_norm, lam_init, b, t):
    m = b * t
    tb = ATT_BLOCK
    tg = tb * ATT_GROUP
    assert tb >= NEAR and t % tg == 0
    blk = pl.BlockSpec((1, tg, D_V), lambda bi, h, gi: (h, bi * (t // tg) + gi, 0))
    full = pl.BlockSpec((1, t, D_V), lambda bi, h, gi: (h, bi, 0))
    return pl.pallas_call(
        functools.partial(_pattn_kernel, lam_init=lam_init),
        out_shape=jax.ShapeDtypeStruct((N_HEADS, m, D_V), BF16),
        grid=(b, N_HEADS, t // tg),
        in_specs=[pl.BlockSpec(memory_space=pltpu.SMEM),
                  blk, full, full,
                  pl.BlockSpec((1, 2, tb, tb), lambda bi, h, gi: (h, 0, 0, 0)),
                  blk,
                  pl.BlockSpec((1, D_V), lambda bi, h, gi: (0, 0))],
        out_specs=blk,
        scratch_shapes=[pltpu.VMEM((ATT_SLABS, tb, t), F32),
                        pltpu.VMEM((ATT_SLABS, tb, t), BF16)],
        compiler_params=_params(("parallel", "parallel", "arbitrary")),
        name="prompt_attention",
    )(lam, q, kb, vb, pbias, sza, head_norm)


def _sattn_kernel(lam_ref, q_ref, kc_ref, vc_ref, kn_ref, vn_ref, bc_ref, bn_ref,
                  sza_ref, hn_ref, o_ref, s_ref, vt_ref, *, lam_init):
    nq = q_ref.shape[1]
    past = kc_ref.shape[1] // N_HEADS
    far_n = past - bc_ref.shape[2]
    lam = lam_ref[0, 0]
    ck = SAMPLE_KEYS
    nchunk = past // ck

    def q2m_of(h):
        q = q_ref[h]
        lane = lax.broadcasted_iota(jnp.int32, q.shape, 1)
        zero = jnp.zeros_like(q)
        return jnp.concatenate([jnp.where(lane < D_QK, q, zero), jnp.where(lane >= D_QK, q, zero)], axis=0)

    q2m = [q2m_of(h) for h in range(N_HEADS)]

    def heads_major(ref, c):
        rows = ref[0, c * ck * N_HEADS:(c + 1) * ck * N_HEADS, :]
        return jnp.swapaxes(rows.reshape(ck, N_HEADS, D_V), 0, 1).astype(BF16)

    def scores(c):
        kt = heads_major(kc_ref, c)
        yield
        for h in range(N_HEADS):
            s_ref[h, :, c * ck:(c + 1) * ck] = lax.dot_general(q2m[h], kt[h], _NT,
                                                               preferred_element_type=F32)
            yield

    def values(c):
        vt = heads_major(vc_ref, c)
        yield
        for h in range(N_HEADS):
            vt_ref[h, c * ck:(c + 1) * ck, :] = vt[h]
        yield

    def head(h):
        s = s_ref[h]
        s = jnp.concatenate([s[:, :far_n], s[:, far_n:] + bc_ref[h]], axis=1)
        sn = lax.dot_general(q2m[h], kn_ref[h], _NT, preferred_element_type=F32) + bn_ref[h]
        mx = jnp.maximum(jnp.max(s, axis=-1, keepdims=True), jnp.max(sn, axis=-1, keepdims=True))
        e = jnp.exp2(s - mx)
        en = jnp.exp2(sn - mx)
        denom = jnp.sum(e, axis=-1, keepdims=True) + jnp.sum(en, axis=-1, keepdims=True)
        row = lax.broadcasted_iota(jnp.int32, denom.shape, 0)
        w = jnp.where(row < nq, 1.0, -lam) / denom

        def combine(x):
            p = x * w
            return (p[:nq] + p[nq:]).astype(BF16)

        a, an = combine(e), combine(en)
        yield
        o = (jnp.dot(a, vt_ref[h], preferred_element_type=F32)
             + jnp.dot(an, vn_ref[h], preferred_element_type=F32))
        o_ref[h] = _head_out(o, hn_ref[...], sza_ref[h].astype(F32), lam_init).astype(o_ref.dtype)
        yield

    def together(*gens):
        gens = list(gens)
        while gens:
            gens = [g for g in gens if next(g, StopIteration) is not StopIteration]

    def first_step(g):
        next(g, None)
        yield

    per_chunk = [scores(c) for c in range(nchunk)]
    next(per_chunk[0])
    for c in range(nchunk):
        ahead = [first_step(per_chunk[c + 1])] if c + 1 < nchunk else []
        together(per_chunk[c], values(c), *ahead)
    for h in range(0, N_HEADS, 2):
        together(head(h), head(h + 1))


def _sample_attention(lam, q, cache_k, cache_v, kb, vb, sbc, sbn, sza, head_norm, lam_init, layer,
                      row0, b):
    nq = sbn.shape[2]
    rows = cache_k.shape[1]
    new = pl.BlockSpec((N_HEADS, nq, D_V), lambda bi: (0, row0 // nq + bi, 0))
    cache = pl.BlockSpec((1, rows, D_V), lambda bi: (layer * b + bi, 0, 0))
    whole = lambda a: pl.BlockSpec(a.shape, lambda bi: (0,) * a.ndim)
    return pl.pallas_call(
        functools.partial(_sattn_kernel, lam_init=lam_init),
        out_shape=jax.ShapeDtypeStruct((N_HEADS, b * nq, D_V), BF16),
        grid=(b,),
        in_specs=[pl.BlockSpec(memory_space=pltpu.SMEM),
                  new, cache, cache, new, new, whole(sbc), whole(sbn), new, whole(head_norm)],
        out_specs=pl.BlockSpec((N_HEADS, nq, D_V), lambda bi: (0, bi, 0)),
        scratch_shapes=[pltpu.VMEM((N_HEADS, 2 * nq, rows // N_HEADS), F32),
                        pltpu.VMEM((N_HEADS, rows // N_HEADS, D_V), BF16)],
        compiler_params=_params(("parallel",)),
        name="sample_attention",
    )(lam, q, cache_k, cache_v, kb, vb, sbc, sbn, sza, head_norm)


def _out_kernel(og_ref, t_ref, sga_ref, sgc_ref, wpa_ref, wpc_ref, wo_ref, x_ref, g_ref,
                o_ref, m_ref):
    c = pl.program_id(1)
    nchunk, _, tc = m_ref.shape
    og = jnp.concatenate([og_ref[h] for h in range(og_ref.shape[0])], axis=1)
    ya = jnp.dot(og, wpa_ref[c], preferred_element_type=F32)
    yc = jnp.dot(t_ref[...], wpc_ref[c], preferred_element_type=F32)
    merged = sga_ref[...].astype(F32) * ya + sgc_ref[...].astype(F32) * yc
    m_ref[c] = merged.astype(BF16)

    @pl.when(c == nchunk - 1)
    def _():
        y = jnp.dot(m_ref[0], wo_ref[0], preferred_element_type=F32)
        for cc in range(1, nchunk):
            y = y + jnp.dot(m_ref[cc], wo_ref[cc], preferred_element_type=F32)
        y = y * lax.rsqrt(jnp.mean(y * y, axis=-1, keepdims=True) + EPS)
        o_ref[...] = x_ref[...] + y * g_ref[...]


def _cast_chunks_kernel(w_ref, o_ref):
    o_ref[0] = w_ref[...].astype(o_ref.dtype)


def _cast_chunks(w, tc, rows=False):
    k, n = w.shape
    blk, imap = ((tc, n), lambda j: (j, 0)) if rows else ((k, tc), lambda j: (0, j))
    nchunk = (k if rows else n) // tc
    return pl.pallas_call(
        _cast_chunks_kernel,
        out_shape=jax.ShapeDtypeStruct((nchunk,) + blk, BF16),
        grid=(nchunk,),
        in_specs=[pl.BlockSpec(blk, imap)],
        out_specs=pl.BlockSpec((1,) + blk, lambda j: (j, 0, 0)),
        compiler_params=_params(("parallel",)),
        name="cast_chunks",
    )(w)


def _merge_out(og, t, gates, row0, wpa, wpc, wo, x, g, tm, tc):
    m, d = x.shape
    assert row0 % tm == 0
    first = row0 // tm
    resident = lambda a: pl.BlockSpec(a.shape, lambda i, c: (0, 0, 0), pipeline_mode=pl.Buffered(1))
    return pl.pallas_call(
        _out_kernel,
        out_shape=jax.ShapeDtypeStruct((m, d), x.dtype),
        grid=(m // tm, d // tc),
        in_specs=[pl.BlockSpec((N_HEADS, tm, D_V), lambda i, c: (0, i, 0)),
                  pl.BlockSpec((tm, CONV_W), lambda i, c: (i, 0)),
                  pl.BlockSpec((tm, tc), lambda i, c: (first + i, c)),
                  pl.BlockSpec((tm, tc), lambda i, c: (first + i, d // tc + c)),
                  resident(wpa), resident(wpc), resident(wo),
                  pl.BlockSpec((tm, d), lambda i, c: (i, 0)),
                  pl.BlockSpec((1, d), lambda i, c: (0, 0))],
        out_specs=pl.BlockSpec((tm, d), lambda i, c: (i, 0)),
        scratch_shapes=[pltpu.VMEM((d // tc, tm, tc), BF16)],
        compiler_params=_params(("parallel", "arbitrary")),
        name="merge_out",
    )(og, t, gates, gates, wpa, wpc, wo, x, g)


def _proj_specs():
    scale = LOG2E / math.sqrt(D_QK)
    heads = (BF16, "heads")
    return (("q", OFF_Q, ATT_W, 1024, lambda r: (r * scale,), (heads,)),
            ("k", OFF_K, ATT_W, 1024, lambda r: (r, r), (F32, heads)),
            ("v", OFF_V, ATT_W, 1024, lambda r: (r, r), (F32, heads)),
            ("za", OFF_ZA, ATT_W, 1024, lambda r: (_silu(r),), (heads,)),
            ("gates", OFF_GA, 2 * D_MODEL, 1024, lambda r: (jax.nn.sigmoid(r),), (BF16,)))


def _layer(xp, xs, conv_p, conv_s, weights, attend_p, attend_s):
    (norm_pre, norm_post, w_in, conv_w, wpa, wpc, wo) = weights
    (bp, tp, d), (bs, ts, _) = xp.shape, xs.shape
    mp, ms = bp * tp, bs * ts
    xp2, xs2 = xp.reshape(mp, d), xs.reshape(ms, d)
    first_tm = 512
    tm = (mp + ms) // 8
    assert tm * 8 == mp + ms and tm % 16 == 0
    res, xn = {}, None
    for name, off, n, tn, epilogue, dtypes in _proj_specs():
        if xn is None:
            xn, *res[name] = _proj((xp2, xs2), w_in, off, n, epilogue, dtypes, first_tm, tn,
                                   "proj_" + name, gain=norm_pre)
        else:
            res[name] = _proj(xn, w_in, off, n, epilogue, dtypes, tm, tn, "proj_" + name)
    (q,), (k, kb), (v, vb), (sza,), (gates,) = (res[n] for n in ("q", "k", "v", "za", "gates"))
    out = []
    for row0, x2, b, t, conv_state, attend, conv_tiles in (
            (0, xp2, bp, tp, conv_p, attend_p, (1, 512, 512)),
            (mp, xs2, bs, ts, conv_s, attend_s, (bs, ts, 512))):
        m = b * t
        tcv, new_conv = _conv_branch(xn, row0, b, t, w_in, conv_state, conv_w, *conv_tiles)
        og = attend(q, kb, vb, sza)
        y = _merge_out(og, tcv.reshape(m, CONV_W), gates, row0, wpa, wpc, wo, x2,
                       norm_post.reshape(1, d), 512, MERGE_CHUNK)
        rows = slice(row0, row0 + m)
        out.append((y.reshape(b, t, d), k[rows].reshape(b, t, N_HEADS, 2 * D_QK),
                    v[rows].reshape(b, t, N_HEADS, D_V), new_conv))
    return out


def kernel(x_prompt, x_sample, cache_k, cache_v, state_conv, norm_pre, norm_post, w_in,
           lambda_q1, lambda_k1, lambda_q2, lambda_k2, head_norm, conv_w, w_proj_attn,
           w_proj_conv, w_out, rel_bias):
    depth = w_in.shape[0]
    past = cache_k.shape[2]
    dec_b, dec_t = x_sample.shape[0], x_sample.shape[1]
    xp, xs = x_prompt, x_sample
    zero_conv = jnp.zeros((xp.shape[0], CONV_K - 1, CONV_W), xp.dtype)
    pbias = _prompt_bias(rel_bias)
    outs = [[] for _ in range(6)]
    for l in range(depth):
        lam_init = 0.8 - 0.6 * math.exp(-0.3 * l)
        sbc, sbn, lam = _sample_bias_and_lambda(
            rel_bias, lambda_q1[l][None], lambda_k1[l][None], lambda_q2[l][None], lambda_k2[l][None],
            past, dec_t, lam_init)
        hn = head_norm[l].reshape(1, D_V)
        weights = (norm_pre[l], norm_post[l], w_in[l], conv_w[l],
                   _cast_chunks(w_proj_attn[l], MERGE_CHUNK), _cast_chunks(w_proj_conv[l], MERGE_CHUNK),
                   _cast_chunks(w_out[l], MERGE_CHUNK, rows=True))

        pb, pt = x_prompt.shape[0], x_prompt.shape[1]

        def attend_prompt(q, kb, vb, sza):
            return _prompt_attention(lam, q, kb, vb, pbias, sza, hn, lam_init, pb, pt)

        ck = cache_k.reshape(depth * dec_b, past * N_HEADS, D_V)
        cv = cache_v.reshape(depth * dec_b, past * N_HEADS, D_V)

        def attend_sample(q, kb, vb, sza, l=l):
            return _sample_attention(lam, q, ck, cv, kb, vb, sbc, sbn, sza, hn, lam_init, l,
                                     pb * pt, dec_b)

        (xp, kp, vp, cp), (xs, ksm, vsm, csm) = _layer(xp, xs, zero_conv, state_conv[l], weights,
                                                       attend_prompt, attend_sample)
        for lst, val in zip(outs, (kp, vp, cp, ksm, vsm, csm)):
            lst.append(val)
    return (xp, xs) + tuple(jnp.stack(o, axis=0) for o in outs)
```

```python
import functools
import math

import jax
import jax.numpy as jnp
from jax import lax
from jax.experimental import pallas as pl
from jax.experimental.pallas import tpu as pltpu

D_MODEL = 2048
N_HEADS = 8
D_QK = 64
D_V = 2 * D_QK
ATT_W = N_HEADS * D_V
CONV_W = D_MODEL // 2
CONV_K = 3
CHUNK = 64
NUM_BUCKETS = 32
MAX_DISTANCE = 128
EPS = 1e-6
MASKED = -1e30
LOG2E = math.log2(math.e)

OFF_Q, OFF_K, OFF_V, OFF_ZA = 0, 1024, 2048, 3072
OFF_BG, OFF_CG, OFF_H, OFF_ZC = 4096, 5120, 6144, 7168
OFF_GA, OFF_GC = 8192, 10240

ATT_BLOCK = 256
ATT_GROUP = 8
ATT_SLABS = 4
ATT_STRIP = 64
MERGE_CHUNK = 512
SAMPLE_KEYS = 256
V7X_VMEM_LIMIT = 56 * 1024 * 1024

F32 = jnp.float32
BF16 = jnp.bfloat16


def _params(sem, vmem=V7X_VMEM_LIMIT):
    return pltpu.CompilerParams(dimension_semantics=sem, vmem_limit_bytes=vmem)


def _bucket_py(rel):
    half = NUM_BUCKETS // 2
    max_exact = half // 2
    steps = half - max_exact
    n = abs(rel)
    if n < max_exact:
        v = n
    else:
        v = max_exact + sum(n ** steps * max_exact ** j >= MAX_DISTANCE ** j * max_exact ** steps
                            for j in range(1, steps))
    return v + (half if rel > 0 else 0)


NEAR = MAX_DISTANCE
FAR_BUCKET = _bucket_py(-NEAR)
_BREAKS = tuple((rel, _bucket_py(rel)) for rel in range(-NEAR + 1, NEAR + 1)
                if _bucket_py(rel) != _bucket_py(rel - 1))


def _bias_minus_far(rel, value_of, max_rel):
    val = value_of(FAR_BUCKET)
    for first_rel, bucket in _BREAKS:
        if first_rel > max_rel:
            break
        val = jnp.where(rel >= first_rel, value_of(bucket), val)
    return (val - value_of(FAR_BUCKET)) * LOG2E


_BIAS_ROWS = 32


def _pbias_kernel(rb_ref, pb_ref):
    h = pl.program_id(0)
    tb = ATT_BLOCK
    value_of = lambda b: rb_ref[b, h]
    for t in range(2):
        def rows(i, carry, t=t):
            r0 = pl.multiple_of(i * _BIAS_ROWS, _BIAS_ROWS)
            r = r0 + lax.broadcasted_iota(jnp.int32, (_BIAS_ROWS, tb), 0)
            c = lax.broadcasted_iota(jnp.int32, (_BIAS_ROWS, tb), 1)
            val = _bias_minus_far(c - r - (1 - t) * tb, value_of, max_rel=tb - 1 if t == 1 else -1)
            if t == 1:
                val = jnp.where(c // CHUNK <= r // CHUNK, val, MASKED)
            pb_ref[0, t, pl.ds(r0, _BIAS_ROWS), :] = val
            return carry

        lax.fori_loop(0, tb // _BIAS_ROWS, rows, 0)


def _prompt_bias(rel_bias):
    tb = ATT_BLOCK
    return pl.pallas_call(
        _pbias_kernel,
        out_shape=jax.ShapeDtypeStruct((N_HEADS, 2, tb, tb), F32),
        grid=(N_HEADS,),
        in_specs=[pl.BlockSpec(memory_space=pltpu.SMEM)],
        out_specs=pl.BlockSpec((1, 2, tb, tb), lambda h: (h, 0, 0, 0)),
        compiler_params=_params(("parallel",)),
        name="prompt_bias",
    )(rel_bias)


def _sbias_kernel(rb_ref, lq1_ref, lk1_ref, lq2_ref, lk2_ref, sbc_ref, sbn_ref, lam_ref,
                  *, past, lam_init):
    nq = sbn_ref.shape[2]

    def tile(shape, k0, value_of):
        qpos = past + lax.broadcasted_iota(jnp.int32, shape, 0) % nq
        kpos = k0 + lax.broadcasted_iota(jnp.int32, shape, 1)
        val = _bias_minus_far(kpos - qpos, value_of, max_rel=k0 + shape[1] - 1 - past)
        return jnp.where(kpos // CHUNK <= qpos // CHUNK, val, MASKED)

    for h in range(N_HEADS):
        value_of = lambda b, h=h: rb_ref[b, h]
        sbc_ref[h] = tile(sbc_ref.shape[1:], past - sbc_ref.shape[2], value_of)
        sbn_ref[h] = tile(sbn_ref.shape[1:], past, value_of)

    s1 = jnp.sum(lq1_ref[...].astype(F32) * lk1_ref[...].astype(F32), axis=-1, keepdims=True)
    s2 = jnp.sum(lq2_ref[...].astype(F32) * lk2_ref[...].astype(F32), axis=-1, keepdims=True)
    lam_ref[...] = jnp.exp(s1) - jnp.exp(s2) + lam_init


def _sample_bias_and_lambda(rel_bias, lq1, lk1, lq2, lk2, past, nq, lam_init):
    vmem = pl.BlockSpec(memory_space=pltpu.VMEM)
    return pl.pallas_call(
        functools.partial(_sbias_kernel, past=past, lam_init=lam_init),
        out_shape=(jax.ShapeDtypeStruct((N_HEADS, 2 * nq, NEAR), F32),
                   jax.ShapeDtypeStruct((N_HEADS, 2 * nq, nq), F32),
                   jax.ShapeDtypeStruct((1, 1), F32)),
        in_specs=[pl.BlockSpec(memory_space=pltpu.SMEM)] + [vmem] * 4,
        out_specs=(vmem, vmem, vmem),
        name="sample_bias_lambda",
    )(rel_bias, lq1, lk1, lq2, lk2)


def _silu(x):
    return x * jax.nn.sigmoid(x)


def _proj_kernel(*refs, epilogue, n_first):
    refs = list(refs)
    if n_first is None:
        x = refs.pop(0)[...]
        w_ref = refs.pop(0)
    else:
        xa_ref, xb_ref, g_ref, w_ref, xn_ref = (refs.pop(0) for _ in range(5))

        def prenorm(x_ref):
            xf = x_ref[...].astype(F32)
            xf = xf * lax.rsqrt(jnp.mean(xf * xf, axis=-1, keepdims=True) + EPS)
            xn_ref[...] = (xf * g_ref[...]).astype(BF16)

        i = pl.program_id(1)
        pl.when(i < n_first)(functools.partial(prenorm, xa_ref))
        pl.when(i >= n_first)(functools.partial(prenorm, xb_ref))
        x = xn_ref[...]
    r = jnp.dot(x, w_ref[...].astype(BF16), preferred_element_type=F32)
    for o_ref, o in zip(refs, epilogue(r)):
        if len(o_ref.shape) == 3:
            for h in range(o_ref.shape[0]):
                o_ref[h] = o[:, h * D_V:(h + 1) * D_V].astype(o_ref.dtype)
        else:
            o_ref[...] = o.astype(o_ref.dtype)


def _proj(x, w, col_off, n, epilogue, out_dtypes, tm, tn, name, gain=None, rows=None):
    jb = col_off // tn
    out_shape, out_specs = [], []
    if gain is None:
        row0, m = (0, x.shape[0]) if rows is None else rows
        d = x.shape[1]
        assert row0 % tm == 0 and m % tm == 0
        n_first = None
        in_specs = [pl.BlockSpec((tm, d), lambda j, i: (row0 // tm + i, 0))]
        args = (x, w)
    else:
        xa, xb = x
        d = xa.shape[1]
        m = xa.shape[0] + xb.shape[0]
        n_first = xa.shape[0] // tm
        assert n == tn and xa.shape[0] % tm == 0 and xb.shape[0] % tm == 0
        in_specs = [pl.BlockSpec((tm, d), lambda j, i: (jnp.minimum(i, n_first - 1), 0)),
                    pl.BlockSpec((tm, d), lambda j, i: (jnp.maximum(i - n_first, 0), 0)),
                    pl.BlockSpec((1, d), lambda j, i: (0, 0))]
        out_shape.append(jax.ShapeDtypeStruct((m, d), BF16))
        out_specs.append(pl.BlockSpec((tm, d), lambda j, i: (i, 0)))
        args = (xa, xb, gain.reshape(1, d), w)
    in_specs.append(pl.BlockSpec((d, tn), lambda j, i: (0, jb + j)))
    for dt in out_dtypes:
        if isinstance(dt, tuple):
            out_shape.append(jax.ShapeDtypeStruct((n // D_V, m, D_V), dt[0]))
            out_specs.append(pl.BlockSpec((tn // D_V, tm, D_V), lambda j, i: (j, i, 0)))
        else:
            out_shape.append(jax.ShapeDtypeStruct((m, n), dt))
            out_specs.append(pl.BlockSpec((tm, tn), lambda j, i: (i, j)))
    return pl.pallas_call(
        functools.partial(_proj_kernel, epilogue=epilogue, n_first=n_first),
        out_shape=tuple(out_shape),
        grid=(n // tn, m // tm),
        in_specs=in_specs,
        out_specs=tuple(out_specs),
        compiler_params=_params(("parallel", "parallel")),
        name=name,
    )(*args)


def _conv_kernel(x_ref, wb_ref, wc_ref, wh_ref, wz_ref, st_ref, cw_ref,
                 t_ref, nc_ref, carry_ref):
    ti = pl.program_id(2)

    @pl.when(ti == 0)
    def _():
        carry_ref[:, 6:8, :] = st_ref[...]

    tb, tt, _ = t_ref.shape
    x = x_ref[...]

    def mm(w_ref):
        return jnp.dot(x, w_ref[...].astype(BF16), preferred_element_type=F32).reshape(tb, tt, -1)

    u = mm(wc_ref) * mm(wh_ref)
    c0 = carry_ref[:, 6:7, :]
    c1 = carry_ref[:, 7:8, :]
    row = lax.broadcasted_iota(jnp.int32, u.shape, 1)
    u1 = jnp.where(row == 0, c1, pltpu.roll(u, 1, 1))
    u2 = jnp.where(row == 0, c0, jnp.where(row == 1, c1, pltpu.roll(u, 2, 1)))
    y = cw_ref[0:1, :] * u2 + cw_ref[1:2, :] * u1 + cw_ref[2:3, :] * u
    carry_ref[...] = u[:, tt - 8:, :]
    nc_ref[...] = carry_ref[:, 6:8, :]
    t_ref[...] = (mm(wb_ref) * y * _silu(mm(wz_ref))).astype(t_ref.dtype)


def _conv_branch(xn, row0, b, t, w, state, conv_w, tb, tt, tc):
    d = xn.shape[1]
    assert row0 % (tb * tt) == 0 and (tb == 1 or tt == t)
    wspec = lambda off: pl.BlockSpec((d, tc), lambda c, bi, ti, o=off // tc: (0, o + c))
    first = row0 // (tb * tt)
    return pl.pallas_call(
        _conv_kernel,
        out_shape=(jax.ShapeDtypeStruct((b, t, CONV_W), BF16),
                   jax.ShapeDtypeStruct((b, CONV_K - 1, CONV_W), F32)),
        grid=(CONV_W // tc, b // tb, t // tt),
        in_specs=[pl.BlockSpec((tb * tt, d), lambda c, bi, ti: (first + bi * (t // tt) + ti, 0)),
                  wspec(OFF_BG), wspec(OFF_CG), wspec(OFF_H), wspec(OFF_ZC),
                  pl.BlockSpec((tb, CONV_K - 1, tc), lambda c, bi, ti: (bi, 0, c)),
                  pl.BlockSpec((CONV_K, tc), lambda c, bi, ti: (0, c))],
        out_specs=(pl.BlockSpec((tb, tt, tc), lambda c, bi, ti: (bi, ti, c)),
                   pl.BlockSpec((tb, CONV_K - 1, tc), lambda c, bi, ti: (bi, 0, c))),
        scratch_shapes=[pltpu.VMEM((tb, 8, tc), F32)],
        compiler_params=_params(("parallel", "parallel", "arbitrary")),
        name="conv_branch",
    )(xn, w, w, w, w, state, conv_w)


_NT = (((1,), (1,)), ((), ()))


def _head_out(o, hn, sza, lam_init):
    o = o * lax.rsqrt(jnp.mean(o * o, axis=-1, keepdims=True) + EPS)
    return o * hn * (1.0 - lam_init) * sza


def _pattn_kernel(lam_ref, q_ref, k_ref, v_ref, bias_ref, sza_ref, hn_ref, o_ref,
                  s_ref, p_ref, *, lam_init):
    gi = pl.program_id(2)
    tb = ATT_BLOCK
    hw = tb // 2

    def block_tasks(n, r):
        rows = slice(r * tb, (r + 1) * tb)
        q = q_ref[0, rows, :]
        lane = lax.broadcasted_iota(jnp.int32, q.shape, 1)
        zero = jnp.zeros_like(q)
        qmaps = (jnp.where(lane < D_QK, q, zero), jnp.where(lane >= D_QK, q, zero))
        chunks = [slice(c * hw, (c + 1) * hw) for c in range(2 * (n + 1))]
        sums = [[], []]
        outs = [None, None]
        slab_of = lambda mi: (2 * r + mi) % ATT_SLABS

        def scores(mi):
            for j in range(n + 1):
                s = lax.dot_general(qmaps[mi], k_ref[0, j * tb:(j + 1) * tb, :], _NT,
                                    preferred_element_type=F32)
                if j == n:
                    s = s + bias_ref[0, 1]
                elif j == n - 1:
                    s = s + bias_ref[0, 0]
                s_ref[slab_of(mi), :, j * tb:(j + 1) * tb] = s
                yield

        def numerator(mi):
            slab = slab_of(mi)
            for st in range(tb // ATT_STRIP):
                srows = slice(st * ATT_STRIP, (st + 1) * ATT_STRIP)
                mpart = functools.reduce(jnp.maximum, [s_ref[slab, srows, c] for c in chunks])
                m = jnp.broadcast_to(jnp.max(mpart, axis=-1, keepdims=True), (ATT_STRIP, hw))
                yield
                lpart = None
                for c in chunks:
                    e = jnp.exp2(s_ref[slab, srows, c] - m)
                    p_ref[slab, srows, c] = e.astype(BF16)
                    lpart = e if lpart is None else lpart + e
                    yield
                sums[mi].append(jnp.sum(lpart, axis=-1, keepdims=True))

        def values(mi):
            o = jnp.dot(p_ref[slab_of(mi), :, :(n + 1) * tb], v_ref[0, :(n + 1) * tb, :],
                        preferred_element_type=F32)
            outs[mi] = o / jnp.concatenate(sums[mi], axis=0)
            if mi == 1:
                o = outs[0] - lam_ref[0, 0] * outs[1]
                o_ref[0, rows, :] = _head_out(o, hn_ref[...], sza_ref[0, rows, :].astype(F32),
                                              lam_init).astype(o_ref.dtype)
            yield

        return [(scores(mi), n + 1, numerator(mi), (tb // ATT_STRIP) * (1 + len(chunks)), values(mi))
                for mi in range(2)]

    def emit_group(g):
        units = [u for r in range(ATT_GROUP) for u in block_tasks(g * ATT_GROUP + r, r)]
        for t in range(len(units) + 2):
            if t >= 2:
                next(units[t - 2][4])
            a, na = (units[t][0], units[t][1]) if t < len(units) else (iter(()), 0)
            b, nb = (units[t - 1][2], units[t - 1][3]) if 1 <= t <= len(units) else (iter(()), 0)
            da = db = 0
            while da < na or db < nb:
                if db >= nb or (da < na and da * nb <= db * na):
                    next(a)
                    da += 1
                else:
                    next(b)
                    db += 1
            for rest in (a, b):
                for _ in rest:
                    pass

    for g in range(k_ref.shape[1] // (tb * ATT_GROUP)):
        pl.when(gi == g)(functools.partial(emit_group, g))


def _prompt_attention(lam, q, kb, vb, pbias, sza, head_norm, lam_init, b, t):
    m = b * t
    tb = ATT_BLOCK
    tg = tb * ATT_GROUP
    assert tb >= NEAR and t % tg == 0
    blk = pl.BlockSpec((1, tg, D_V), lambda bi, h, gi: (h, bi * (t // tg) + gi, 0))
    full = pl.BlockSpec((1, t, D_V), lambda bi, h, gi: (h, bi, 0))
    return pl.pallas_call(
        functools.partial(_pattn_kernel, lam_init=lam_init),
        out_shape=jax.ShapeDtypeStruct((N_HEADS, m, D_V), BF16),
        grid=(b, N_HEADS, t // tg),
        in_specs=[pl.BlockSpec(memory_space=pltpu.SMEM),
                  blk, full, full,
                  pl.BlockSpec((1, 2, tb, tb), lambda bi, h, gi: (h, 0, 0, 0)),
                  blk,
                  pl.BlockSpec((1, D_V), lambda bi, h, gi: (0, 0))],
        out_specs=blk,
        scratch_shapes=[pltpu.VMEM((ATT_SLABS, tb, t), F32),
                        pltpu.VMEM((ATT_SLABS, tb, t), BF16)],
        compiler_params=_params(("parallel", "parallel", "arbitrary")),
        name="prompt_attention",
    )(lam, q, kb, vb, pbias, sza, head_norm)


def _sattn_kernel(lam_ref, q_ref, kc_ref, vc_ref, kn_ref, vn_ref, bc_ref, bn_ref,
                  sza_ref, hn_ref, o_ref, s_ref, vt_ref, *, lam_init):
    nq = q_ref.shape[1]
    past = kc_ref.shape[1] // N_HEADS
    far_n = past - bc_ref.shape[2]
    lam = lam_ref[0, 0]
    ck = SAMPLE_KEYS
    nchunk = past // ck

    def q2m_of(h):
        q = q_ref[h]
        lane = lax.broadcasted_iota(jnp.int32, q.shape, 1)
        zero = jnp.zeros_like(q)
        return jnp.concatenate([jnp.where(lane < D_QK, q, zero), jnp.where(lane >= D_QK, q, zero)], axis=0)

    q2m = [q2m_of(h) for h in range(N_HEADS)]

    def heads_major(ref, c):
        rows = ref[0, c * ck * N_HEADS:(c + 1) * ck * N_HEADS, :]
        return jnp.swapaxes(rows.reshape(ck, N_HEADS, D_V), 0, 1).astype(BF16)

    def scores(c):
        kt = heads_major(kc_ref, c)
        yield
        for h in range(N_HEADS):
            s_ref[h, :, c * ck:(c + 1) * ck] = lax.dot_general(q2m[h], kt[h], _NT,
                                                               preferred_element_type=F32)
            yield

    def values(c):
        vt = heads_major(vc_ref, c)
        yield
        for h in range(N_HEADS):
            vt_ref[h, c * ck:(c + 1) * ck, :] = vt[h]
        yield

    def head(h):
        s = s_ref[h]
        s = jnp.concatenate([s[:, :far_n], s[:, far_n:] + bc_ref[h]], axis=1)
        sn = lax.dot_general(q2m[h], kn_ref[h], _NT, preferred_element_type=F32) + bn_ref[h]
        mx = jnp.maximum(jnp.max(s, axis=-1, keepdims=True), jnp.max(sn, axis=-1, keepdims=True))
        e = jnp.exp2(s - mx)
        en = jnp.exp2(sn - mx)
        denom = jnp.sum(e, axis=-1, keepdims=True) + jnp.sum(en, axis=-1, keepdims=True)
        row = lax.broadcasted_iota(jnp.int32, denom.shape, 0)
        w = jnp.where(row < nq, 1.0, -lam) / denom

        def combine(x):
            p = x * w
            return (p[:nq] + p[nq:]).astype(BF16)

        a, an = combine(e), combine(en)
        yield
        o = (jnp.dot(a, vt_ref[h], preferred_element_type=F32)
             + jnp.dot(an, vn_ref[h], preferred_element_type=F32))
        o_ref[h] = _head_out(o, hn_ref[...], sza_ref[h].astype(F32), lam_init).astype(o_ref.dtype)
        yield

    def together(*gens):
        gens = list(gens)
        while gens:
            gens = [g for g in gens if next(g, StopIteration) is not StopIteration]

    def first_step(g):
        next(g, None)
        yield

    per_chunk = [scores(c) for c in range(nchunk)]
    next(per_chunk[0])
    for c in range(nchunk):
        ahead = [first_step(per_chunk[c + 1])] if c + 1 < nchunk else []
        together(per_chunk[c], values(c), *ahead)
    for h in range(0, N_HEADS, 2):
        together(head(h), head(h + 1))


def _sample_attention(lam, q, cache_k, cache_v, kb, vb, sbc, sbn, sza, head_norm, lam_init, layer,
                      row0, b):
    nq = sbn.shape[2]
    rows = cache_k.shape[1]
    stacked = pl.BlockSpec((N_HEADS, nq, D_V), lambda bi: (0, row0 // nq + bi, 0))
    own = pl.BlockSpec((N_HEADS, nq, D_V), lambda bi: (0, bi, 0))
    cache = pl.BlockSpec((1, rows, D_V), lambda bi: (layer * b + bi, 0, 0))
    whole = lambda a: pl.BlockSpec(a.shape, lambda bi: (0,) * a.ndim)
    return pl.pallas_call(
        functools.partial(_sattn_kernel, lam_init=lam_init),
        out_shape=jax.ShapeDtypeStruct((N_HEADS, b * nq, D_V), BF16),
        grid=(b,),
        in_specs=[pl.BlockSpec(memory_space=pltpu.SMEM),
                  stacked, cache, cache, own, own, whole(sbc), whole(sbn), stacked, whole(head_norm)],
        out_specs=own,
        scratch_shapes=[pltpu.VMEM((N_HEADS, 2 * nq, rows // N_HEADS), F32),
                        pltpu.VMEM((N_HEADS, rows // N_HEADS, D_V), BF16)],
        compiler_params=_params(("parallel",)),
        name="sample_attention",
    )(lam, q, cache_k, cache_v, kb, vb, sbc, sbn, sza, head_norm)


def _out_kernel(og_ref, t_ref, sga_ref, sgc_ref, wpa_ref, wpc_ref, wo_ref, x_ref, g_ref,
                o_ref, m_ref):
    c = pl.program_id(1)
    nchunk, _, tc = m_ref.shape
    og = jnp.concatenate([og_ref[h] for h in range(og_ref.shape[0])], axis=1)
    ya = jnp.dot(og, wpa_ref[c], preferred_element_type=F32)
    yc = jnp.dot(t_ref[...], wpc_ref[c], preferred_element_type=F32)
    merged = sga_ref[...].astype(F32) * ya + sgc_ref[...].astype(F32) * yc
    m_ref[c] = merged.astype(BF16)

    @pl.when(c == nchunk - 1)
    def _():
        y = jnp.dot(m_ref[0], wo_ref[0], preferred_element_type=F32)
        for cc in range(1, nchunk):
            y = y + jnp.dot(m_ref[cc], wo_ref[cc], preferred_element_type=F32)
        y = y * lax.rsqrt(jnp.mean(y * y, axis=-1, keepdims=True) + EPS)
        o_ref[...] = x_ref[...] + y * g_ref[...]


def _cast_chunks_kernel(w_ref, o_ref):
    o_ref[0] = w_ref[...].astype(o_ref.dtype)


def _cast_chunks(w, tc, rows=False):
    k, n = w.shape
    blk, imap = ((tc, n), lambda j: (j, 0)) if rows else ((k, tc), lambda j: (0, j))
    nchunk = (k if rows else n) // tc
    return pl.pallas_call(
        _cast_chunks_kernel,
        out_shape=jax.ShapeDtypeStruct((nchunk,) + blk, BF16),
        grid=(nchunk,),
        in_specs=[pl.BlockSpec(blk, imap)],
        out_specs=pl.BlockSpec((1,) + blk, lambda j: (j, 0, 0)),
        compiler_params=_params(("parallel",)),
        name="cast_chunks",
    )(w)


def _merge_out(og, t, gates, row0, wpa, wpc, wo, x, g, tm, tc):
    m, d = x.shape
    assert row0 % tm == 0
    first = row0 // tm
    resident = lambda a: pl.BlockSpec(a.shape, lambda i, c: (0, 0, 0), pipeline_mode=pl.Buffered(1))
    return pl.pallas_call(
        _out_kernel,
        out_shape=jax.ShapeDtypeStruct((m, d), x.dtype),
        grid=(m // tm, d // tc),
        in_specs=[pl.BlockSpec((N_HEADS, tm, D_V), lambda i, c: (0, i, 0)),
                  pl.BlockSpec((tm, CONV_W), lambda i, c: (i, 0)),
                  pl.BlockSpec((tm, tc), lambda i, c: (first + i, c)),
                  pl.BlockSpec((tm, tc), lambda i, c: (first + i, d // tc + c)),
                  resident(wpa), resident(wpc), resident(wo),
                  pl.BlockSpec((tm, d), lambda i, c: (i, 0)),
                  pl.BlockSpec((1, d), lambda i, c: (0, 0))],
        out_specs=pl.BlockSpec((tm, d), lambda i, c: (i, 0)),
        scratch_shapes=[pltpu.VMEM((d // tc, tm, tc), BF16)],
        compiler_params=_params(("parallel", "arbitrary")),
        name="merge_out",
    )(og, t, gates, gates, wpa, wpc, wo, x, g)


def _proj_specs():
    scale = LOG2E / math.sqrt(D_QK)
    heads = (BF16, "heads")
    return (("q", OFF_Q, ATT_W, 1024, lambda r: (r * scale,), (heads,)),
            ("k", OFF_K, ATT_W, 1024, lambda r: (r, r), (F32, heads)),
            ("v", OFF_V, ATT_W, 1024, lambda r: (r, r), (F32, heads)),
            ("za", OFF_ZA, ATT_W, 1024, lambda r: (_silu(r),), (heads,)),
            ("gates", OFF_GA, 2 * D_MODEL, 1024, lambda r: (jax.nn.sigmoid(r),), (BF16,)))


def _layer(xp, xs, conv_p, conv_s, weights, attend_p, attend_s):
    (norm_pre, norm_post, w_in, conv_w, wpa, wpc, wo) = weights
    (bp, tp, d), (bs, ts, _) = xp.shape, xs.shape
    mp, ms = bp * tp, bs * ts
    xp2, xs2 = xp.reshape(mp, d), xs.reshape(ms, d)
    first_tm = 512
    tm = (mp + ms) // 8
    assert tm * 8 == mp + ms and tm % 16 == 0
    res, xn = {}, None
    for name, off, n, tn, epilogue, dtypes in _proj_specs():
        if xn is None:
            xn, *res[name] = _proj((xp2, xs2), w_in, off, n, epilogue, dtypes, first_tm, tn,
                                   "proj_" + name, gain=norm_pre)
        elif F32 in dtypes:
            res[name] = [_proj(xn, w_in, off, n, epilogue, dtypes, min(m, 1024), tn, "proj_" + name,
                               rows=(row0, m)) for row0, m in ((0, mp), (mp, ms))]
        else:
            res[name] = _proj(xn, w_in, off, n, epilogue, dtypes, tm, tn, "proj_" + name)
    (q,), (sza,), (gates,) = (res[n] for n in ("q", "za", "gates"))
    out = []
    for g, (row0, x2, b, t, conv_state, attend, conv_tiles) in enumerate((
            (0, xp2, bp, tp, conv_p, attend_p, (1, 512, 512)),
            (mp, xs2, bs, ts, conv_s, attend_s, (bs, ts, 512)))):
        m = b * t
        (k, kb), (v, vb) = res["k"][g], res["v"][g]
        tcv, new_conv = _conv_branch(xn, row0, b, t, w_in, conv_state, conv_w, *conv_tiles)
        og = attend(q, kb, vb, sza)
        y = _merge_out(og, tcv.reshape(m, CONV_W), gates, row0, wpa, wpc, wo, x2,
                       norm_post.reshape(1, d), 512, MERGE_CHUNK)
        out.append((y.reshape(b, t, d), k.reshape(b, t, N_HEADS, 2 * D_QK),
                    v.reshape(b, t, N_HEADS, D_V), new_conv))
    return out


def kernel(x_prompt, x_sample, cache_k, cache_v, state_conv, norm_pre, norm_post, w_in,
           lambda_q1, lambda_k1, lambda_q2, lambda_k2, head_norm, conv_w, w_proj_attn,
           w_proj_conv, w_out, rel_bias):
    depth = w_in.shape[0]
    past = cache_k.shape[2]
    dec_b, dec_t = x_sample.shape[0], x_sample.shape[1]
    xp, xs = x_prompt, x_sample
    zero_conv = jnp.zeros((xp.shape[0], CONV_K - 1, CONV_W), xp.dtype)
    pbias = _prompt_bias(rel_bias)
    outs = [[] for _ in range(6)]
    for l in range(depth):
        lam_init = 0.8 - 0.6 * math.exp(-0.3 * l)
        sbc, sbn, lam = _sample_bias_and_lambda(
            rel_bias, lambda_q1[l][None], lambda_k1[l][None], lambda_q2[l][None], lambda_k2[l][None],
            past, dec_t, lam_init)
        hn = head_norm[l].reshape(1, D_V)
        weights = (norm_pre[l], norm_post[l], w_in[l], conv_w[l],
                   _cast_chunks(w_proj_attn[l], MERGE_CHUNK), _cast_chunks(w_proj_conv[l], MERGE_CHUNK),
                   _cast_chunks(w_out[l], MERGE_CHUNK, rows=True))

        pb, pt = x_prompt.shape[0], x_prompt.shape[1]

        def attend_prompt(q, kb, vb, sza):
            return _prompt_attention(lam, q, kb, vb, pbias, sza, hn, lam_init, pb, pt)

        ck = cache_k.reshape(depth * dec_b, past * N_HEADS, D_V)
        cv = cache_v.reshape(depth * dec_b, past * N_HEADS, D_V)

        def attend_sample(q, kb, vb, sza, l=l):
            return _sample_attention(lam, q, ck, cv, kb, vb, sbc, sbn, sza, hn, lam_init, l,
                                     pb * pt, dec_b)

        (xp, kp, vp, cp), (xs, ksm, vsm, csm) = _layer(xp, xs, zero_conv, state_conv[l], weights,
                                                       attend_prompt, attend_sample)
        for lst, val in zip(outs, (kp, vp, cp, ksm, vsm, csm)):
            lst.append(val)
    return (xp, xs) + tuple(jnp.stack(o, axis=0) for o in outs)
```

```python
import functools
import math

import jax
import jax.numpy as jnp
from jax import lax
from jax.experimental import pallas as pl
from jax.experimental.pallas import tpu as pltpu

D_MODEL = 2048
N_HEADS = 8
D_QK = 64
D_V = 2 * D_QK
ATT_W = N_HEADS * D_V
CONV_W = D_MODEL // 2
CONV_K = 3
CHUNK = 64
NUM_BUCKETS = 32
MAX_DISTANCE = 128
EPS = 1e-6
MASKED = -1e30
LOG2E = math.log2(math.e)

OFF_Q, OFF_K, OFF_V, OFF_ZA = 0, 1024, 2048, 3072
OFF_BG, OFF_CG, OFF_H, OFF_ZC = 4096, 5120, 6144, 7168
OFF_GA, OFF_GC = 8192, 10240

ATT_BLOCK = 256
ATT_GROUP = 8
ATT_SLABS = 4
ATT_STRIP = 64
MERGE_CHUNK = 512
SAMPLE_KEYS = 256
V7X_VMEM_LIMIT = 56 * 1024 * 1024

F32 = jnp.float32
BF16 = jnp.bfloat16


def _params(sem, vmem=V7X_VMEM_LIMIT):
    return pltpu.CompilerParams(dimension_semantics=sem, vmem_limit_bytes=vmem)


def _bucket_py(rel):
    half = NUM_BUCKETS // 2
    max_exact = half // 2
    steps = half - max_exact
    n = abs(rel)
    if n < max_exact:
        v = n
    else:
        v = max_exact + sum(n ** steps * max_exact ** j >= MAX_DISTANCE ** j * max_exact ** steps
                            for j in range(1, steps))
    return v + (half if rel > 0 else 0)


NEAR = MAX_DISTANCE
FAR_BUCKET = _bucket_py(-NEAR)
_BREAKS = tuple((rel, _bucket_py(rel)) for rel in range(-NEAR + 1, NEAR + 1)
                if _bucket_py(rel) != _bucket_py(rel - 1))


def _bias_minus_far(rel, value_of, max_rel):
    val = value_of(FAR_BUCKET)
    for first_rel, bucket in _BREAKS:
        if first_rel > max_rel:
            break
        val = jnp.where(rel >= first_rel, value_of(bucket), val)
    return (val - value_of(FAR_BUCKET)) * LOG2E


_BIAS_ROWS = 32


def _pbias_kernel(rb_ref, pb_ref):
    h = pl.program_id(0)
    tb = ATT_BLOCK
    value_of = lambda b: rb_ref[b, h]
    for t in range(2):
        def rows(i, carry, t=t):
            r0 = pl.multiple_of(i * _BIAS_ROWS, _BIAS_ROWS)
            r = r0 + lax.broadcasted_iota(jnp.int32, (_BIAS_ROWS, tb), 0)
            c = lax.broadcasted_iota(jnp.int32, (_BIAS_ROWS, tb), 1)
            val = _bias_minus_far(c - r - (1 - t) * tb, value_of, max_rel=tb - 1 if t == 1 else -1)
            if t == 1:
                val = jnp.where(c // CHUNK <= r // CHUNK, val, MASKED)
            pb_ref[0, t, pl.ds(r0, _BIAS_ROWS), :] = val
            return carry

        lax.fori_loop(0, tb // _BIAS_ROWS, rows, 0)


def _prompt_bias(rel_bias):
    tb = ATT_BLOCK
    return pl.pallas_call(
        _pbias_kernel,
        out_shape=jax.ShapeDtypeStruct((N_HEADS, 2, tb, tb), F32),
        grid=(N_HEADS,),
        in_specs=[pl.BlockSpec(memory_space=pltpu.SMEM)],
        out_specs=pl.BlockSpec((1, 2, tb, tb), lambda h: (h, 0, 0, 0)),
        compiler_params=_params(("parallel",)),
        name="prompt_bias",
    )(rel_bias)


def _sbias_kernel(rb_ref, lq1_ref, lk1_ref, lq2_ref, lk2_ref, sbc_ref, sbn_ref, lam_ref,
                  *, past, lam_init):
    nq = sbn_ref.shape[2]

    def tile(shape, k0, value_of):
        qpos = past + lax.broadcasted_iota(jnp.int32, shape, 0) % nq
        kpos = k0 + lax.broadcasted_iota(jnp.int32, shape, 1)
        val = _bias_minus_far(kpos - qpos, value_of, max_rel=k0 + shape[1] - 1 - past)
        return jnp.where(kpos // CHUNK <= qpos // CHUNK, val, MASKED)

    for h in range(N_HEADS):
        value_of = lambda b, h=h: rb_ref[b, h]
        sbc_ref[h] = tile(sbc_ref.shape[1:], past - sbc_ref.shape[2], value_of)
        sbn_ref[h] = tile(sbn_ref.shape[1:], past, value_of)

    s1 = jnp.sum(lq1_ref[...].astype(F32) * lk1_ref[...].astype(F32), axis=-1, keepdims=True)
    s2 = jnp.sum(lq2_ref[...].astype(F32) * lk2_ref[...].astype(F32), axis=-1, keepdims=True)
    lam_ref[...] = jnp.exp(s1) - jnp.exp(s2) + lam_init


def _sample_bias_and_lambda(rel_bias, lq1, lk1, lq2, lk2, past, nq, lam_init):
    vmem = pl.BlockSpec(memory_space=pltpu.VMEM)
    return pl.pallas_call(
        functools.partial(_sbias_kernel, past=past, lam_init=lam_init),
        out_shape=(jax.ShapeDtypeStruct((N_HEADS, 2 * nq, NEAR), F32),
                   jax.ShapeDtypeStruct((N_HEADS, 2 * nq, nq), F32),
                   jax.ShapeDtypeStruct((1, 1), F32)),
        in_specs=[pl.BlockSpec(memory_space=pltpu.SMEM)] + [vmem] * 4,
        out_specs=(vmem, vmem, vmem),
        name="sample_bias_lambda",
    )(rel_bias, lq1, lk1, lq2, lk2)


def _silu(x):
    return x * jax.nn.sigmoid(x)


def _proj_kernel(*refs, epilogue, n_first, n_split):
    refs = list(refs)
    if n_first is None:
        x = refs.pop(0)[...]
        w_ref = refs.pop(0)
    else:
        xa_ref, xb_ref, g_ref, w_ref, xn_ref = (refs.pop(0) for _ in range(5))

        def prenorm(x_ref):
            xf = x_ref[...].astype(F32)
            xf = xf * lax.rsqrt(jnp.mean(xf * xf, axis=-1, keepdims=True) + EPS)
            xn_ref[...] = (xf * g_ref[...]).astype(BF16)

        i = pl.program_id(1)
        pl.when(i < n_first)(functools.partial(prenorm, xa_ref))
        pl.when(i >= n_first)(functools.partial(prenorm, xb_ref))
        x = xn_ref[...]
    r = jnp.dot(x, w_ref[...].astype(BF16), preferred_element_type=F32)
    results = epilogue(r)

    def store(o_refs):
        for o_ref, o in zip(o_refs, results):
            if len(o_ref.shape) == 3:
                for h in range(o_ref.shape[0]):
                    o_ref[h] = o[:, h * D_V:(h + 1) * D_V].astype(o_ref.dtype)
            else:
                o_ref[...] = o.astype(o_ref.dtype)

    if n_split is None:
        store(refs)
    else:
        i = pl.program_id(1)
        pl.when(i < n_split)(functools.partial(store, refs[0::2]))
        pl.when(i >= n_split)(functools.partial(store, refs[1::2]))


def _proj(x, w, col_off, n, epilogue, out_dtypes, tm, tn, name, gain=None, split=None):
    jb = col_off // tn
    out_shape, out_specs = [], []
    if gain is None:
        m, d = x.shape
        n_first = None
        in_specs = [pl.BlockSpec((tm, d), lambda j, i: (i, 0))]
        args = (x, w)
    else:
        xa, xb = x
        d = xa.shape[1]
        m = xa.shape[0] + xb.shape[0]
        n_first = xa.shape[0] // tm
        assert n == tn and xa.shape[0] % tm == 0 and xb.shape[0] % tm == 0
        in_specs = [pl.BlockSpec((tm, d), lambda j, i: (jnp.minimum(i, n_first - 1), 0)),
                    pl.BlockSpec((tm, d), lambda j, i: (jnp.maximum(i - n_first, 0), 0)),
                    pl.BlockSpec((1, d), lambda j, i: (0, 0))]
        out_shape.append(jax.ShapeDtypeStruct((m, d), BF16))
        out_specs.append(pl.BlockSpec((tm, d), lambda j, i: (i, 0)))
        args = (xa, xb, gain.reshape(1, d), w)
    in_specs.append(pl.BlockSpec((d, tn), lambda j, i: (0, jb + j)))
    if split is None:
        n_split = None
        groups = ((m, lambda i: i),)
    else:
        assert split % tm == 0 and (m - split) % tm == 0
        n_split = split // tm
        groups = ((split, lambda i: jnp.minimum(i, n_split - 1)),
                  (m - split, lambda i: jnp.maximum(i - n_split, 0)))
    for dt in out_dtypes:
        for rows, row_block in groups:
            if isinstance(dt, tuple):
                out_shape.append(jax.ShapeDtypeStruct((n // D_V, rows, D_V), dt[0]))
                out_specs.append(pl.BlockSpec((tn // D_V, tm, D_V),
                                              lambda j, i, rb=row_block: (j, rb(i), 0)))
            else:
                out_shape.append(jax.ShapeDtypeStruct((rows, n), dt))
                out_specs.append(pl.BlockSpec((tm, tn), lambda j, i, rb=row_block: (rb(i), j)))
    return pl.pallas_call(
        functools.partial(_proj_kernel, epilogue=epilogue, n_first=n_first, n_split=n_split),
        out_shape=tuple(out_shape),
        grid=(n // tn, m // tm),
        in_specs=in_specs,
        out_specs=tuple(out_specs),
        compiler_params=_params(("parallel", "arbitrary" if split else "parallel")),
        name=name,
    )(*args)


def _conv_kernel(x_ref, wb_ref, wc_ref, wh_ref, wz_ref, st_ref, cw_ref,
                 t_ref, nc_ref, carry_ref):
    ti = pl.program_id(2)

    @pl.when(ti == 0)
    def _():
        carry_ref[:, 6:8, :] = st_ref[...]

    tb, tt, _ = t_ref.shape
    x = x_ref[...]

    def mm(w_ref):
        return jnp.dot(x, w_ref[...].astype(BF16), preferred_element_type=F32).reshape(tb, tt, -1)

    u = mm(wc_ref) * mm(wh_ref)
    c0 = carry_ref[:, 6:7, :]
    c1 = carry_ref[:, 7:8, :]
    row = lax.broadcasted_iota(jnp.int32, u.shape, 1)
    u1 = jnp.where(row == 0, c1, pltpu.roll(u, 1, 1))
    u2 = jnp.where(row == 0, c0, jnp.where(row == 1, c1, pltpu.roll(u, 2, 1)))
    y = cw_ref[0:1, :] * u2 + cw_ref[1:2, :] * u1 + cw_ref[2:3, :] * u
    carry_ref[...] = u[:, tt - 8:, :]
    nc_ref[...] = carry_ref[:, 6:8, :]
    t_ref[...] = (mm(wb_ref) * y * _silu(mm(wz_ref))).astype(t_ref.dtype)


def _conv_branch(xn, row0, b, t, w, state, conv_w, tb, tt, tc):
    d = xn.shape[1]
    assert row0 % (tb * tt) == 0 and (tb == 1 or tt == t)
    wspec = lambda off: pl.BlockSpec((d, tc), lambda c, bi, ti, o=off // tc: (0, o + c))
    first = row0 // (tb * tt)
    return pl.pallas_call(
        _conv_kernel,
        out_shape=(jax.ShapeDtypeStruct((b, t, CONV_W), BF16),
                   jax.ShapeDtypeStruct((b, CONV_K - 1, CONV_W), F32)),
        grid=(CONV_W // tc, b // tb, t // tt),
        in_specs=[pl.BlockSpec((tb * tt, d), lambda c, bi, ti: (first + bi * (t // tt) + ti, 0)),
                  wspec(OFF_BG), wspec(OFF_CG), wspec(OFF_H), wspec(OFF_ZC),
                  pl.BlockSpec((tb, CONV_K - 1, tc), lambda c, bi, ti: (bi, 0, c)),
                  pl.BlockSpec((CONV_K, tc), lambda c, bi, ti: (0, c))],
        out_specs=(pl.BlockSpec((tb, tt, tc), lambda c, bi, ti: (bi, ti, c)),
                   pl.BlockSpec((tb, CONV_K - 1, tc), lambda c, bi, ti: (bi, 0, c))),
        scratch_shapes=[pltpu.VMEM((tb, 8, tc), F32)],
        compiler_params=_params(("parallel", "parallel", "arbitrary")),
        name="conv_branch",
    )(xn, w, w, w, w, state, conv_w)


_NT = (((1,), (1,)), ((), ()))


def _head_out(o, hn, sza, lam_init):
    o = o * lax.rsqrt(jnp.mean(o * o, axis=-1, keepdims=True) + EPS)
    return o * hn * (1.0 - lam_init) * sza


def _pattn_kernel(lam_ref, q_ref, k_ref, v_ref, bias_ref, sza_ref, hn_ref, o_ref,
                  s_ref, p_ref, *, lam_init):
    gi = pl.program_id(2)
    tb = ATT_BLOCK
    hw = tb // 2

    def block_tasks(n, r):
        rows = slice(r * tb, (r + 1) * tb)
        q = q_ref[0, rows, :]
        lane = lax.broadcasted_iota(jnp.int32, q.shape, 1)
        zero = jnp.zeros_like(q)
        qmaps = (jnp.where(lane < D_QK, q, zero), jnp.where(lane >= D_QK, q, zero))
        chunks = [slice(c * hw, (c + 1) * hw) for c in range(2 * (n + 1))]
        sums = [[], []]
        maxes = [None, None]
        outs = [None, None]
        slab_of = lambda mi: (2 * r + mi) % ATT_SLABS

        def scores(mi):
            mpart = None
            for j in range(n + 1):
                s = lax.dot_general(qmaps[mi], k_ref[0, j * tb:(j + 1) * tb, :], _NT,
                                    preferred_element_type=F32)
                if j == n:
                    s = s + bias_ref[0, 1]
                elif j == n - 1:
                    s = s + bias_ref[0, 0]
                s_ref[slab_of(mi), :, j * tb:(j + 1) * tb] = s
                part = jnp.maximum(s[:, :hw], s[:, hw:])
                mpart = part if mpart is None else jnp.maximum(mpart, part)
                if j == n:
                    maxes[mi] = jnp.max(mpart, axis=-1, keepdims=True)
                yield

        def numerator(mi):
            slab = slab_of(mi)
            for st in range(tb // ATT_STRIP):
                srows = slice(st * ATT_STRIP, (st + 1) * ATT_STRIP)
                m = jnp.broadcast_to(maxes[mi][srows], (ATT_STRIP, hw))
                yield
                lpart = None
                for c in chunks:
                    e = jnp.exp2(s_ref[slab, srows, c] - m)
                    p_ref[slab, srows, c] = e.astype(BF16)
                    lpart = e if lpart is None else lpart + e
                    yield
                sums[mi].append(jnp.sum(lpart, axis=-1, keepdims=True))

        def values(mi):
            o = jnp.dot(p_ref[slab_of(mi), :, :(n + 1) * tb], v_ref[0, :(n + 1) * tb, :],
                        preferred_element_type=F32)
            outs[mi] = o / jnp.concatenate(sums[mi], axis=0)
            if mi == 1:
                o = outs[0] - lam_ref[0, 0] * outs[1]
                o_ref[0, rows, :] = _head_out(o, hn_ref[...], sza_ref[0, rows, :].astype(F32),
                                              lam_init).astype(o_ref.dtype)
            yield

        return [(scores(mi), n + 1, numerator(mi), (tb // ATT_STRIP) * (1 + len(chunks)), values(mi))
                for mi in range(2)]

    def emit_group(g):
        units = [u for r in range(ATT_GROUP) for u in block_tasks(g * ATT_GROUP + r, r)]
        for t in range(len(units) + 2):
            if t >= 2:
                next(units[t - 2][4])
            a, na = (units[t][0], units[t][1]) if t < len(units) else (iter(()), 0)
            b, nb = (units[t - 1][2], units[t - 1][3]) if 1 <= t <= len(units) else (iter(()), 0)
            da = db = 0
            while da < na or db < nb:
                if db >= nb or (da < na and da * nb <= db * na):
                    next(a)
                    da += 1
                else:
                    next(b)
                    db += 1
            for rest in (a, b):
                for _ in rest:
                    pass

    for g in range(k_ref.shape[1] // (tb * ATT_GROUP)):
        pl.when(gi == g)(functools.partial(emit_group, g))


def _prompt_attention(lam, q, kb, vb, pbias, sza, head_norm, lam_init, b, t):
    m = b * t
    tb = ATT_BLOCK
    tg = tb * ATT_GROUP
    assert tb >= NEAR and t % tg == 0
    blk = pl.BlockSpec((1, tg, D_V), lambda bi, h, gi: (h, bi * (t // tg) + gi, 0))
    full = pl.BlockSpec((1, t, D_V), lambda bi, h, gi: (h, bi, 0))
    return pl.pallas_call(
        functools.partial(_pattn_kernel, lam_init=lam_init),
        out_shape=jax.ShapeDtypeStruct((N_HEADS, m, D_V), BF16),
        grid=(b, N_HEADS, t // tg),
        in_specs=[pl.BlockSpec(memory_space=pltpu.SMEM),
                  blk, full, full,
                  pl.BlockSpec((1, 2, tb, tb), lambda bi, h, gi: (h, 0, 0, 0)),
                  blk,
                  pl.BlockSpec((1, D_V), lambda bi, h, gi: (0, 0))],
        out_specs=blk,
        scratch_shapes=[pltpu.VMEM((ATT_SLABS, tb, t), F32),
                        pltpu.VMEM((ATT_SLABS, tb, t), BF16)],
        compiler_params=_params(("parallel", "parallel", "arbitrary")),
        name="prompt_attention",
    )(lam, q, kb, vb, pbias, sza, head_norm)


def _sattn_kernel(lam_ref, q_ref, kc_ref, vc_ref, kn_ref, vn_ref, bc_ref, bn_ref,
                  sza_ref, hn_ref, o_ref, s_ref, vt_ref, *, lam_init):
    nq = q_ref.shape[1]
    past = kc_ref.shape[1] // N_HEADS
    far_n = past - bc_ref.shape[2]
    lam = lam_ref[0, 0]
    ck = SAMPLE_KEYS
    nchunk = past // ck

    def q2m_of(h):
        q = q_ref[h]
        lane = lax.broadcasted_iota(jnp.int32, q.shape, 1)
        zero = jnp.zeros_like(q)
        return jnp.concatenate([jnp.where(lane < D_QK, q, zero), jnp.where(lane >= D_QK, q, zero)], axis=0)

    q2m = [q2m_of(h) for h in range(N_HEADS)]

    def heads_major(ref, c):
        rows = ref[0, c * ck * N_HEADS:(c + 1) * ck * N_HEADS, :]
        return jnp.swapaxes(rows.reshape(ck, N_HEADS, D_V), 0, 1).astype(BF16)

    def scores(c):
        kt = heads_major(kc_ref, c)
        yield
        for h in range(N_HEADS):
            s_ref[h, :, c * ck:(c + 1) * ck] = lax.dot_general(q2m[h], kt[h], _NT,
                                                               preferred_element_type=F32)
            yield

    def values(c):
        vt = heads_major(vc_ref, c)
        yield
        for h in range(N_HEADS):
            vt_ref[h, c * ck:(c + 1) * ck, :] = vt[h]
        yield

    def head(h):
        s = s_ref[h]
        s = jnp.concatenate([s[:, :far_n], s[:, far_n:] + bc_ref[h]], axis=1)
        sn = lax.dot_general(q2m[h], kn_ref[h], _NT, preferred_element_type=F32) + bn_ref[h]
        mx = jnp.maximum(jnp.max(s, axis=-1, keepdims=True), jnp.max(sn, axis=-1, keepdims=True))
        e = jnp.exp2(s - mx)
        en = jnp.exp2(sn - mx)
        denom = jnp.sum(e, axis=-1, keepdims=True) + jnp.sum(en, axis=-1, keepdims=True)
        row = lax.broadcasted_iota(jnp.int32, denom.shape, 0)
        w = jnp.where(row < nq, 1.0, -lam) / denom

        def combine(x):
            p = x * w
            return (p[:nq] + p[nq:]).astype(BF16)

        a, an = combine(e), combine(en)
        yield
        o = (jnp.dot(a, vt_ref[h], preferred_element_type=F32)
             + jnp.dot(an, vn_ref[h], preferred_element_type=F32))
        o_ref[h] = _head_out(o, hn_ref[...], sza_ref[h].astype(F32), lam_init).astype(o_ref.dtype)
        yield

    def together(*gens):
        gens = list(gens)
        while gens:
            gens = [g for g in gens if next(g, StopIteration) is not StopIteration]

    def first_step(g):
        next(g, None)
        yield

    per_chunk = [scores(c) for c in range(nchunk)]
    next(per_chunk[0])
    for c in range(nchunk):
        ahead = [first_step(per_chunk[c + 1])] if c + 1 < nchunk else []
        together(per_chunk[c], values(c), *ahead)
    for h in range(0, N_HEADS, 2):
        together(head(h), head(h + 1))


def _sample_attention(lam, q, cache_k, cache_v, kb, vb, sbc, sbn, sza, head_norm, lam_init, layer,
                      row0, b):
    nq = sbn.shape[2]
    rows = cache_k.shape[1]
    stacked = pl.BlockSpec((N_HEADS, nq, D_V), lambda bi: (0, row0 // nq + bi, 0))
    own = pl.BlockSpec((N_HEADS, nq, D_V), lambda bi: (0, bi, 0))
    cache = pl.BlockSpec((1, rows, D_V), lambda bi: (layer * b + bi, 0, 0))
    whole = lambda a: pl.BlockSpec(a.shape, lambda bi: (0,) * a.ndim)
    return pl.pallas_call(
        functools.partial(_sattn_kernel, lam_init=lam_init),
        out_shape=jax.ShapeDtypeStruct((N_HEADS, b * nq, D_V), BF16),
        grid=(b,),
        in_specs=[pl.BlockSpec(memory_space=pltpu.SMEM),
                  stacked, cache, cache, own, own, whole(sbc), whole(sbn), stacked, whole(head_norm)],
        out_specs=own,
        scratch_shapes=[pltpu.VMEM((N_HEADS, 2 * nq, rows // N_HEADS), F32),
                        pltpu.VMEM((N_HEADS, rows // N_HEADS, D_V), BF16)],
        compiler_params=_params(("parallel",)),
        name="sample_attention",
    )(lam, q, cache_k, cache_v, kb, vb, sbc, sbn, sza, head_norm)


def _out_kernel(og_ref, t_ref, sga_ref, sgc_ref, wpa_ref, wpc_ref, wo_ref, x_ref, g_ref,
                o_ref, m_ref):
    c = pl.program_id(1)
    nchunk, _, tc = m_ref.shape
    og = jnp.concatenate([og_ref[h] for h in range(og_ref.shape[0])], axis=1)
    ya = jnp.dot(og, wpa_ref[c], preferred_element_type=F32)
    yc = jnp.dot(t_ref[...], wpc_ref[c], preferred_element_type=F32)
    merged = sga_ref[...].astype(F32) * ya + sgc_ref[...].astype(F32) * yc
    m_ref[c] = merged.astype(BF16)

    @pl.when(c == nchunk - 1)
    def _():
        y = jnp.dot(m_ref[0], wo_ref[0], preferred_element_type=F32)
        for cc in range(1, nchunk):
            y = y + jnp.dot(m_ref[cc], wo_ref[cc], preferred_element_type=F32)
        y = y * lax.rsqrt(jnp.mean(y * y, axis=-1, keepdims=True) + EPS)
        o_ref[...] = x_ref[...] + y * g_ref[...]


def _cast_chunks_kernel(w_ref, o_ref):
    o_ref[0] = w_ref[...].astype(o_ref.dtype)


def _cast_chunks(w, tc, rows=False):
    k, n = w.shape
    blk, imap = ((tc, n), lambda j: (j, 0)) if rows else ((k, tc), lambda j: (0, j))
    nchunk = (k if rows else n) // tc
    return pl.pallas_call(
        _cast_chunks_kernel,
        out_shape=jax.ShapeDtypeStruct((nchunk,) + blk, BF16),
        grid=(nchunk,),
        in_specs=[pl.BlockSpec(blk, imap)],
        out_specs=pl.BlockSpec((1,) + blk, lambda j: (j, 0, 0)),
        compiler_params=_params(("parallel",)),
        name="cast_chunks",
    )(w)


def _merge_out(og, t, gates, row0, wpa, wpc, wo, x, g, tm, tc):
    m, d = x.shape
    assert row0 % tm == 0
    first = row0 // tm
    resident = lambda a: pl.BlockSpec(a.shape, lambda i, c: (0, 0, 0), pipeline_mode=pl.Buffered(1))
    return pl.pallas_call(
        _out_kernel,
        out_shape=jax.ShapeDtypeStruct((m, d), x.dtype),
        grid=(m // tm, d // tc),
        in_specs=[pl.BlockSpec((N_HEADS, tm, D_V), lambda i, c: (0, i, 0)),
                  pl.BlockSpec((tm, CONV_W), lambda i, c: (i, 0)),
                  pl.BlockSpec((tm, tc), lambda i, c: (first + i, c)),
                  pl.BlockSpec((tm, tc), lambda i, c: (first + i, d // tc + c)),
                  resident(wpa), resident(wpc), resident(wo),
                  pl.BlockSpec((tm, d), lambda i, c: (i, 0)),
                  pl.BlockSpec((1, d), lambda i, c: (0, 0))],
        out_specs=pl.BlockSpec((tm, d), lambda i, c: (i, 0)),
        scratch_shapes=[pltpu.VMEM((d // tc, tm, tc), BF16)],
        compiler_params=_params(("parallel", "arbitrary")),
        name="merge_out",
    )(og, t, gates, gates, wpa, wpc, wo, x, g)


def _proj_specs():
    scale = LOG2E / math.sqrt(D_QK)
    heads = (BF16, "heads")
    return (("q", OFF_Q, ATT_W, 1024, lambda r: (r * scale,), (heads,)),
            ("k", OFF_K, ATT_W, 1024, lambda r: (r, r), (F32, heads)),
            ("v", OFF_V, ATT_W, 1024, lambda r: (r, r), (F32, heads)),
            ("za", OFF_ZA, ATT_W, 1024, lambda r: (_silu(r),), (heads,)),
            ("gates", OFF_GA, 2 * D_MODEL, 1024, lambda r: (jax.nn.sigmoid(r),), (BF16,)))


def _layer(xp, xs, conv_p, conv_s, weights, attend_p, attend_s):
    (norm_pre, norm_post, w_in, conv_w, wpa, wpc, wo) = weights
    (bp, tp, d), (bs, ts, _) = xp.shape, xs.shape
    mp, ms = bp * tp, bs * ts
    xp2, xs2 = xp.reshape(mp, d), xs.reshape(ms, d)
    first_tm = 512
    tm = (mp + ms) // 8
    assert tm * 8 == mp + ms and tm % 16 == 0
    res, xn = {}, None
    for name, off, n, tn, epilogue, dtypes in _proj_specs():
        if xn is None:
            xn, *res[name] = _proj((xp2, xs2), w_in, off, n, epilogue, dtypes, first_tm, tn,
                                   "proj_" + name, gain=norm_pre)
        elif F32 in dtypes:
            k_p, k_s, kb_p, kb_s = _proj(xn, w_in, off, n, epilogue, dtypes, first_tm, tn,
                                         "proj_" + name, split=mp)
            res[name] = [(k_p, kb_p), (k_s, kb_s)]
        else:
            res[name] = _proj(xn, w_in, off, n, epilogue, dtypes, tm, tn, "proj_" + name)
    (q,), (sza,), (gates,) = (res[n] for n in ("q", "za", "gates"))
    out = []
    for g, (row0, x2, b, t, conv_state, attend, conv_tiles) in enumerate((
            (0, xp2, bp, tp, conv_p, attend_p, (1, 512, 512)),
            (mp, xs2, bs, ts, conv_s, attend_s, (bs, ts, 512)))):
        m = b * t
        (k, kb), (v, vb) = res["k"][g], res["v"][g]
        tcv, new_conv = _conv_branch(xn, row0, b, t, w_in, conv_state, conv_w, *conv_tiles)
        og = attend(q, kb, vb, sza)
        y = _merge_out(og, tcv.reshape(m, CONV_W), gates, row0, wpa, wpc, wo, x2,
                       norm_post.reshape(1, d), 512, MERGE_CHUNK)
        out.append((y.reshape(b, t, d), k.reshape(b, t, N_HEADS, 2 * D_QK),
                    v.reshape(b, t, N_HEADS, D_V), new_conv))
    return out


def kernel(x_prompt, x_sample, cache_k, cache_v, state_conv, norm_pre, norm_post, w_in,
           lambda_q1, lambda_k1, lambda_q2, lambda_k2, head_norm, conv_w, w_proj_attn,
           w_proj_conv, w_out, rel_bias):
    depth = w_in.shape[0]
    past = cache_k.shape[2]
    dec_b, dec_t = x_sample.shape[0], x_sample.shape[1]
    xp, xs = x_prompt, x_sample
    zero_conv = jnp.zeros((xp.shape[0], CONV_K - 1, CONV_W), xp.dtype)
    pbias = _prompt_bias(rel_bias)
    outs = [[] for _ in range(6)]
    for l in range(depth):
        lam_init = 0.8 - 0.6 * math.exp(-0.3 * l)
        sbc, sbn, lam = _sample_bias_and_lambda(
            rel_bias, lambda_q1[l][None], lambda_k1[l][None], lambda_q2[l][None], lambda_k2[l][None],
            past, dec_t, lam_init)
        hn = head_norm[l].reshape(1, D_V)
        weights = (norm_pre[l], norm_post[l], w_in[l], conv_w[l],
                   _cast_chunks(w_proj_attn[l], MERGE_CHUNK), _cast_chunks(w_proj_conv[l], MERGE_CHUNK),
                   _cast_chunks(w_out[l], MERGE_CHUNK, rows=True))

        pb, pt = x_prompt.shape[0], x_prompt.shape[1]

        def attend_prompt(q, kb, vb, sza):
            return _prompt_attention(lam, q, kb, vb, pbias, sza, hn, lam_init, pb, pt)

        ck = cache_k.reshape(depth * dec_b, past * N_HEADS, D_V)
        cv = cache_v.reshape(depth * dec_b, past * N_HEADS, D_V)

        def attend_sample(q, kb, vb, sza, l=l):
            return _sample_attention(lam, q, ck, cv, kb, vb, sbc, sbn, sza, hn, lam_init, l,
                                     pb * pt, dec_b)

        (xp, kp, vp, cp), (xs, ksm, vsm, csm) = _layer(xp, xs, zero_conv, state_conv[l], weights,
                                                       attend_prompt, attend_sample)
        for lst, val in zip(outs, (kp, vp, cp, ksm, vsm, csm)):
            lst.append(val)
    return (xp, xs) + tuple(jnp.stack(o, axis=0) for o in outs)
```

```python
import functools
import math

import jax
import jax.numpy as jnp
from jax import lax
from jax.experimental import pallas as pl
from jax.experimental.pallas import tpu as pltpu

D_MODEL = 2048
N_HEADS = 8
D_QK = 64
D_V = 2 * D_QK
ATT_W = N_HEADS * D_V
CONV_W = D_MODEL // 2
CONV_K = 3
CHUNK = 64
NUM_BUCKETS = 32
MAX_DISTANCE = 128
EPS = 1e-6
MASKED = -1e30
LOG2E = math.log2(math.e)

OFF_Q, OFF_K, OFF_V, OFF_ZA = 0, 1024, 2048, 3072
OFF_BG, OFF_CG, OFF_H, OFF_ZC = 4096, 5120, 6144, 7168
OFF_GA, OFF_GC = 8192, 10240

ATT_BLOCK = 256
ATT_GROUP = 8
ATT_SLABS = 4
ATT_STRIP = 64
MERGE_CHUNK = 1024
SAMPLE_KEYS = 256
V7X_VMEM_LIMIT = 56 * 1024 * 1024

F32 = jnp.float32
BF16 = jnp.bfloat16


def _params(sem, vmem=V7X_VMEM_LIMIT):
    return pltpu.CompilerParams(dimension_semantics=sem, vmem_limit_bytes=vmem)


def _bucket_py(rel):
    half = NUM_BUCKETS // 2
    max_exact = half // 2
    steps = half - max_exact
    n = abs(rel)
    if n < max_exact:
        v = n
    else:
        v = max_exact + sum(n ** steps * max_exact ** j >= MAX_DISTANCE ** j * max_exact ** steps
                            for j in range(1, steps))
    return v + (half if rel > 0 else 0)


NEAR = MAX_DISTANCE
FAR_BUCKET = _bucket_py(-NEAR)
_BREAKS = tuple((rel, _bucket_py(rel)) for rel in range(-NEAR + 1, NEAR + 1)
                if _bucket_py(rel) != _bucket_py(rel - 1))


def _bias_minus_far(rel, value_of, max_rel):
    val = value_of(FAR_BUCKET)
    for first_rel, bucket in _BREAKS:
        if first_rel > max_rel:
            break
        val = jnp.where(rel >= first_rel, value_of(bucket), val)
    return (val - value_of(FAR_BUCKET)) * LOG2E


_BIAS_ROWS = 32


def _pbias_kernel(rb_ref, pb_ref):
    h = pl.program_id(0)
    tb = ATT_BLOCK
    value_of = lambda b: rb_ref[b, h]
    for t in range(2):
        def rows(i, carry, t=t):
            r0 = pl.multiple_of(i * _BIAS_ROWS, _BIAS_ROWS)
            r = r0 + lax.broadcasted_iota(jnp.int32, (_BIAS_ROWS, tb), 0)
            c = lax.broadcasted_iota(jnp.int32, (_BIAS_ROWS, tb), 1)
            val = _bias_minus_far(c - r - (1 - t) * tb, value_of, max_rel=tb - 1 if t == 1 else -1)
            if t == 1:
                val = jnp.where(c // CHUNK <= r // CHUNK, val, MASKED)
            pb_ref[0, t, pl.ds(r0, _BIAS_ROWS), :] = val
            return carry

        lax.fori_loop(0, tb // _BIAS_ROWS, rows, 0)


def _prompt_bias(rel_bias):
    tb = ATT_BLOCK
    return pl.pallas_call(
        _pbias_kernel,
        out_shape=jax.ShapeDtypeStruct((N_HEADS, 2, tb, tb), F32),
        grid=(N_HEADS,),
        in_specs=[pl.BlockSpec(memory_space=pltpu.SMEM)],
        out_specs=pl.BlockSpec((1, 2, tb, tb), lambda h: (h, 0, 0, 0)),
        compiler_params=_params(("parallel",)),
        name="prompt_bias",
    )(rel_bias)


def _sbias_kernel(rb_ref, lq1_ref, lk1_ref, lq2_ref, lk2_ref, sbc_ref, sbn_ref, lam_ref,
                  *, past, lam_init):
    nq = sbn_ref.shape[2]

    def tile(shape, k0, value_of):
        qpos = past + lax.broadcasted_iota(jnp.int32, shape, 0) % nq
        kpos = k0 + lax.broadcasted_iota(jnp.int32, shape, 1)
        val = _bias_minus_far(kpos - qpos, value_of, max_rel=k0 + shape[1] - 1 - past)
        return jnp.where(kpos // CHUNK <= qpos // CHUNK, val, MASKED)

    for h in range(N_HEADS):
        value_of = lambda b, h=h: rb_ref[b, h]
        sbc_ref[h] = tile(sbc_ref.shape[1:], past - sbc_ref.shape[2], value_of)
        sbn_ref[h] = tile(sbn_ref.shape[1:], past, value_of)

    s1 = jnp.sum(lq1_ref[...].astype(F32) * lk1_ref[...].astype(F32), axis=-1, keepdims=True)
    s2 = jnp.sum(lq2_ref[...].astype(F32) * lk2_ref[...].astype(F32), axis=-1, keepdims=True)
    lam_ref[...] = jnp.exp(s1) - jnp.exp(s2) + lam_init


def _sample_bias_and_lambda(rel_bias, lq1, lk1, lq2, lk2, past, nq, lam_init):
    vmem = pl.BlockSpec(memory_space=pltpu.VMEM)
    return pl.pallas_call(
        functools.partial(_sbias_kernel, past=past, lam_init=lam_init),
        out_shape=(jax.ShapeDtypeStruct((N_HEADS, 2 * nq, NEAR), F32),
                   jax.ShapeDtypeStruct((N_HEADS, 2 * nq, nq), F32),
                   jax.ShapeDtypeStruct((1, 1), F32)),
        in_specs=[pl.BlockSpec(memory_space=pltpu.SMEM)] + [vmem] * 4,
        out_specs=(vmem, vmem, vmem),
        name="sample_bias_lambda",
    )(rel_bias, lq1, lk1, lq2, lk2)


def _silu(x):
    return x * jax.nn.sigmoid(x)


def _proj_kernel(*refs, epilogue, n_first, n_split):
    refs = list(refs)
    if n_first is None:
        x = refs.pop(0)[...]
        w_ref = refs.pop(0)
    else:
        xa_ref, xb_ref, g_ref, w_ref, xn_ref = (refs.pop(0) for _ in range(5))

        def prenorm(x_ref):
            xf = x_ref[...].astype(F32)
            xf = xf * lax.rsqrt(jnp.mean(xf * xf, axis=-1, keepdims=True) + EPS)
            xn_ref[...] = (xf * g_ref[...]).astype(BF16)

        i = pl.program_id(1)
        pl.when(i < n_first)(functools.partial(prenorm, xa_ref))
        pl.when(i >= n_first)(functools.partial(prenorm, xb_ref))
        x = xn_ref[...]
    r = jnp.dot(x, w_ref[...].astype(BF16), preferred_element_type=F32)
    results = epilogue(r)

    def store(o_refs):
        for o_ref, o in zip(o_refs, results):
            if len(o_ref.shape) == 3:
                for h in range(o_ref.shape[0]):
                    o_ref[h] = o[:, h * D_V:(h + 1) * D_V].astype(o_ref.dtype)
            else:
                o_ref[...] = o.astype(o_ref.dtype)

    if n_split is None:
        store(refs)
    else:
        i = pl.program_id(1)
        pl.when(i < n_split)(functools.partial(store, refs[0::2]))
        pl.when(i >= n_split)(functools.partial(store, refs[1::2]))


def _proj(x, w, col_off, n, epilogue, out_dtypes, tm, tn, name, gain=None, split=None):
    jb = col_off // tn
    out_shape, out_specs = [], []
    if gain is None:
        m, d = x.shape
        n_first = None
        in_specs = [pl.BlockSpec((tm, d), lambda j, i: (i, 0))]
        args = (x, w)
    else:
        xa, xb = x
        d = xa.shape[1]
        m = xa.shape[0] + xb.shape[0]
        n_first = xa.shape[0] // tm
        assert n == tn and xa.shape[0] % tm == 0 and xb.shape[0] % tm == 0
        in_specs = [pl.BlockSpec((tm, d), lambda j, i: (jnp.minimum(i, n_first - 1), 0)),
                    pl.BlockSpec((tm, d), lambda j, i: (jnp.maximum(i - n_first, 0), 0)),
                    pl.BlockSpec((1, d), lambda j, i: (0, 0))]
        out_shape.append(jax.ShapeDtypeStruct((m, d), BF16))
        out_specs.append(pl.BlockSpec((tm, d), lambda j, i: (i, 0)))
        args = (xa, xb, gain.reshape(1, d), w)
    in_specs.append(pl.BlockSpec((d, tn), lambda j, i: (0, jb + j)))
    if split is None:
        n_split = None
        groups = ((m, lambda i: i),)
    else:
        assert split % tm == 0 and (m - split) % tm == 0
        n_split = split // tm
        groups = ((split, lambda i: jnp.minimum(i, n_split - 1)),
                  (m - split, lambda i: jnp.maximum(i - n_split, 0)))
    for dt in out_dtypes:
        for rows, row_block in groups:
            if isinstance(dt, tuple):
                out_shape.append(jax.ShapeDtypeStruct((n // D_V, rows, D_V), dt[0]))
                out_specs.append(pl.BlockSpec((tn // D_V, tm, D_V),
                                              lambda j, i, rb=row_block: (j, rb(i), 0)))
            else:
                out_shape.append(jax.ShapeDtypeStruct((rows, n), dt))
                out_specs.append(pl.BlockSpec((tm, tn), lambda j, i, rb=row_block: (rb(i), j)))
    return pl.pallas_call(
        functools.partial(_proj_kernel, epilogue=epilogue, n_first=n_first, n_split=n_split),
        out_shape=tuple(out_shape),
        grid=(n // tn, m // tm),
        in_specs=in_specs,
        out_specs=tuple(out_specs),
        compiler_params=_params(("parallel", "arbitrary" if split else "parallel")),
        name=name,
    )(*args)


def _conv_kernel(x_ref, wb_ref, wc_ref, wh_ref, wz_ref, st_ref, cw_ref,
                 t_ref, nc_ref, carry_ref):
    ti = pl.program_id(2)

    @pl.when(ti == 0)
    def _():
        carry_ref[:, 6:8, :] = st_ref[...]

    tb, tt, _ = t_ref.shape
    x = x_ref[...]

    def mm(w_ref):
        return jnp.dot(x, w_ref[...].astype(BF16), preferred_element_type=F32).reshape(tb, tt, -1)

    u = mm(wc_ref) * mm(wh_ref)
    c0 = carry_ref[:, 6:7, :]
    c1 = carry_ref[:, 7:8, :]
    row = lax.broadcasted_iota(jnp.int32, u.shape, 1)
    u1 = jnp.where(row == 0, c1, pltpu.roll(u, 1, 1))
    u2 = jnp.where(row == 0, c0, jnp.where(row == 1, c1, pltpu.roll(u, 2, 1)))
    y = cw_ref[0:1, :] * u2 + cw_ref[1:2, :] * u1 + cw_ref[2:3, :] * u
    carry_ref[...] = u[:, tt - 8:, :]
    nc_ref[...] = carry_ref[:, 6:8, :]
    t_ref[...] = (mm(wb_ref) * y * _silu(mm(wz_ref))).astype(t_ref.dtype)


def _conv_branch(xn, row0, b, t, w, state, conv_w, tb, tt, tc):
    d = xn.shape[1]
    assert row0 % (tb * tt) == 0 and (tb == 1 or tt == t)
    wspec = lambda off: pl.BlockSpec((d, tc), lambda c, bi, ti, o=off // tc: (0, o + c))
    first = row0 // (tb * tt)
    return pl.pallas_call(
        _conv_kernel,
        out_shape=(jax.ShapeDtypeStruct((b, t, CONV_W), BF16),
                   jax.ShapeDtypeStruct((b, CONV_K - 1, CONV_W), F32)),
        grid=(CONV_W // tc, b // tb, t // tt),
        in_specs=[pl.BlockSpec((tb * tt, d), lambda c, bi, ti: (first + bi * (t // tt) + ti, 0)),
                  wspec(OFF_BG), wspec(OFF_CG), wspec(OFF_H), wspec(OFF_ZC),
                  pl.BlockSpec((tb, CONV_K - 1, tc), lambda c, bi, ti: (bi, 0, c)),
                  pl.BlockSpec((CONV_K, tc), lambda c, bi, ti: (0, c))],
        out_specs=(pl.BlockSpec((tb, tt, tc), lambda c, bi, ti: (bi, ti, c)),
                   pl.BlockSpec((tb, CONV_K - 1, tc), lambda c, bi, ti: (bi, 0, c))),
        scratch_shapes=[pltpu.VMEM((tb, 8, tc), F32)],
        compiler_params=_params(("parallel", "parallel", "arbitrary")),
        name="conv_branch",
    )(xn, w, w, w, w, state, conv_w)


_NT = (((1,), (1,)), ((), ()))


def _head_out(o, hn, sza, lam_init):
    o = o * lax.rsqrt(jnp.mean(o * o, axis=-1, keepdims=True) + EPS)
    return o * hn * (1.0 - lam_init) * sza


def _pattn_kernel(lam_ref, q_ref, k_ref, v_ref, bias_ref, sza_ref, hn_ref, o_ref,
                  s_ref, p_ref, *, lam_init):
    gi = pl.program_id(2)
    tb = ATT_BLOCK
    hw = tb // 2

    def block_tasks(n, r):
        rows = slice(r * tb, (r + 1) * tb)
        q = q_ref[0, rows, :]
        lane = lax.broadcasted_iota(jnp.int32, q.shape, 1)
        zero = jnp.zeros_like(q)
        qmaps = (jnp.where(lane < D_QK, q, zero), jnp.where(lane >= D_QK, q, zero))
        chunks = [slice(c * hw, (c + 1) * hw) for c in range(2 * (n + 1))]
        sums = [[], []]
        maxes = [None, None]
        outs = [None, None]
        slab_of = lambda mi: (2 * r + mi) % ATT_SLABS

        def scores(mi):
            mpart = None
            for j in range(n + 1):
                s = lax.dot_general(qmaps[mi], k_ref[0, j * tb:(j + 1) * tb, :], _NT,
                                    preferred_element_type=F32)
                if j == n:
                    s = s + bias_ref[0, 1]
                elif j == n - 1:
                    s = s + bias_ref[0, 0]
                s_ref[slab_of(mi), :, j * tb:(j + 1) * tb] = s
                part = jnp.maximum(s[:, :hw], s[:, hw:])
                mpart = part if mpart is None else jnp.maximum(mpart, part)
                if j == n:
                    maxes[mi] = jnp.max(mpart, axis=-1, keepdims=True)
                yield

        def numerator(mi):
            slab = slab_of(mi)
            for st in range(tb // ATT_STRIP):
                srows = slice(st * ATT_STRIP, (st + 1) * ATT_STRIP)
                m = jnp.broadcast_to(maxes[mi][srows], (ATT_STRIP, hw))
                yield
                lpart = None
                for c in chunks:
                    e = jnp.exp2(s_ref[slab, srows, c] - m)
                    p_ref[slab, srows, c] = e.astype(BF16)
                    lpart = e if lpart is None else lpart + e
                    yield
                sums[mi].append(jnp.sum(lpart, axis=-1, keepdims=True))

        def values(mi):
            o = jnp.dot(p_ref[slab_of(mi), :, :(n + 1) * tb], v_ref[0, :(n + 1) * tb, :],
                        preferred_element_type=F32)
            outs[mi] = o / jnp.concatenate(sums[mi], axis=0)
            if mi == 1:
                o = outs[0] - lam_ref[0, 0] * outs[1]
                o_ref[0, rows, :] = _head_out(o, hn_ref[...], sza_ref[0, rows, :].astype(F32),
                                              lam_init).astype(o_ref.dtype)
            yield

        return [(scores(mi), n + 1, numerator(mi), (tb // ATT_STRIP) * (1 + len(chunks)), values(mi))
                for mi in range(2)]

    def emit_group(g):
        units = [u for r in range(ATT_GROUP) for u in block_tasks(g * ATT_GROUP + r, r)]
        for t in range(len(units) + 2):
            if t >= 2:
                next(units[t - 2][4])
            a, na = (units[t][0], units[t][1]) if t < len(units) else (iter(()), 0)
            b, nb = (units[t - 1][2], units[t - 1][3]) if 1 <= t <= len(units) else (iter(()), 0)
            da = db = 0
            while da < na or db < nb:
                if db >= nb or (da < na and da * nb <= db * na):
                    next(a)
                    da += 1
                else:
                    next(b)
                    db += 1
            for rest in (a, b):
                for _ in rest:
                    pass

    for g in range(k_ref.shape[1] // (tb * ATT_GROUP)):
        pl.when(gi == g)(functools.partial(emit_group, g))


def _prompt_attention(lam, q, kb, vb, pbias, sza, head_norm, lam_init, b, t):
    m = b * t
    tb = ATT_BLOCK
    tg = tb * ATT_GROUP
    assert tb >= NEAR and t % tg == 0
    blk = pl.BlockSpec((1, tg, D_V), lambda bi, h, gi: (h, bi * (t // tg) + gi, 0))
    full = pl.BlockSpec((1, t, D_V), lambda bi, h, gi: (h, bi, 0))
    return pl.pallas_call(
        functools.partial(_pattn_kernel, lam_init=lam_init),
        out_shape=jax.ShapeDtypeStruct((N_HEADS, m, D_V), BF16),
        grid=(b, N_HEADS, t // tg),
        in_specs=[pl.BlockSpec(memory_space=pltpu.SMEM),
                  blk, full, full,
                  pl.BlockSpec((1, 2, tb, tb), lambda bi, h, gi: (h, 0, 0, 0)),
                  blk,
                  pl.BlockSpec((1, D_V), lambda bi, h, gi: (0, 0))],
        out_specs=blk,
        scratch_shapes=[pltpu.VMEM((ATT_SLABS, tb, t), F32),
                        pltpu.VMEM((ATT_SLABS, tb, t), BF16)],
        compiler_params=_params(("parallel", "parallel", "arbitrary")),
        name="prompt_attention",
    )(lam, q, kb, vb, pbias, sza, head_norm)


def _sattn_kernel(lam_ref, q_ref, kc_ref, vc_ref, kn_ref, vn_ref, bc_ref, bn_ref,
                  sza_ref, hn_ref, o_ref, s_ref, vt_ref, *, lam_init):
    nq = q_ref.shape[1]
    past = kc_ref.shape[1] // N_HEADS
    far_n = past - bc_ref.shape[2]
    lam = lam_ref[0, 0]
    ck = SAMPLE_KEYS
    nchunk = past // ck

    def q2m_of(h):
        q = q_ref[h]
        lane = lax.broadcasted_iota(jnp.int32, q.shape, 1)
        zero = jnp.zeros_like(q)
        return jnp.concatenate([jnp.where(lane < D_QK, q, zero), jnp.where(lane >= D_QK, q, zero)], axis=0)

    q2m = [q2m_of(h) for h in range(N_HEADS)]

    def heads_major(ref, c):
        rows = ref[0, c * ck * N_HEADS:(c + 1) * ck * N_HEADS, :]
        return jnp.swapaxes(rows.reshape(ck, N_HEADS, D_V), 0, 1).astype(BF16)

    def scores(c):
        kt = heads_major(kc_ref, c)
        yield
        for h in range(N_HEADS):
            s_ref[h, :, c * ck:(c + 1) * ck] = lax.dot_general(q2m[h], kt[h], _NT,
                                                               preferred_element_type=F32)
            yield

    def values(c):
        vt = heads_major(vc_ref, c)
        yield
        for h in range(N_HEADS):
            vt_ref[h, c * ck:(c + 1) * ck, :] = vt[h]
        yield

    def head(h):
        s = s_ref[h]
        s = jnp.concatenate([s[:, :far_n], s[:, far_n:] + bc_ref[h]], axis=1)
        sn = lax.dot_general(q2m[h], kn_ref[h], _NT, preferred_element_type=F32) + bn_ref[h]
        mx = jnp.maximum(jnp.max(s, axis=-1, keepdims=True), jnp.max(sn, axis=-1, keepdims=True))
        e = jnp.exp2(s - mx)
        en = jnp.exp2(sn - mx)
        denom = jnp.sum(e, axis=-1, keepdims=True) + jnp.sum(en, axis=-1, keepdims=True)
        row = lax.broadcasted_iota(jnp.int32, denom.shape, 0)
        w = jnp.where(row < nq, 1.0, -lam) / denom

        def combine(x):
            p = x * w
            return (p[:nq] + p[nq:]).astype(BF16)

        a, an = combine(e), combine(en)
        yield
        o = (jnp.dot(a, vt_ref[h], preferred_element_type=F32)
             + jnp.dot(an, vn_ref[h], preferred_element_type=F32))
        o_ref[h] = _head_out(o, hn_ref[...], sza_ref[h].astype(F32), lam_init).astype(o_ref.dtype)
        yield

    def together(*gens):
        gens = list(gens)
        while gens:
            gens = [g for g in gens if next(g, StopIteration) is not StopIteration]

    def first_step(g):
        next(g, None)
        yield

    per_chunk = [scores(c) for c in range(nchunk)]
    next(per_chunk[0])
    for c in range(nchunk):
        ahead = [first_step(per_chunk[c + 1])] if c + 1 < nchunk else []
        together(per_chunk[c], values(c), *ahead)
    for h in range(0, N_HEADS, 2):
        together(head(h), head(h + 1))


def _sample_attention(lam, q, cache_k, cache_v, kb, vb, sbc, sbn, sza, head_norm, lam_init, layer,
                      row0, b):
    nq = sbn.shape[2]
    rows = cache_k.shape[1]
    stacked = pl.BlockSpec((N_HEADS, nq, D_V), lambda bi: (0, row0 // nq + bi, 0))
    own = pl.BlockSpec((N_HEADS, nq, D_V), lambda bi: (0, bi, 0))
    cache = pl.BlockSpec((1, rows, D_V), lambda bi: (layer * b + bi, 0, 0))
    whole = lambda a: pl.BlockSpec(a.shape, lambda bi: (0,) * a.ndim)
    return pl.pallas_call(
        functools.partial(_sattn_kernel, lam_init=lam_init),
        out_shape=jax.ShapeDtypeStruct((N_HEADS, b * nq, D_V), BF16),
        grid=(b,),
        in_specs=[pl.BlockSpec(memory_space=pltpu.SMEM),
                  stacked, cache, cache, own, own, whole(sbc), whole(sbn), stacked, whole(head_norm)],
        out_specs=own,
        scratch_shapes=[pltpu.VMEM((N_HEADS, 2 * nq, rows // N_HEADS), F32),
                        pltpu.VMEM((N_HEADS, rows // N_HEADS, D_V), BF16)],
        compiler_params=_params(("parallel",)),
        name="sample_attention",
    )(lam, q, cache_k, cache_v, kb, vb, sbc, sbn, sza, head_norm)


def _out_kernel(og_ref, t_ref, sga_ref, sgc_ref, wpa_ref, wpc_ref, wo_ref, x_ref, g_ref,
                o_ref, m_ref):
    c = pl.program_id(1)
    nchunk, _, tc = m_ref.shape
    og = jnp.concatenate([og_ref[h] for h in range(og_ref.shape[0])], axis=1)
    ya = jnp.dot(og, wpa_ref[c], preferred_element_type=F32)
    yc = jnp.dot(t_ref[...], wpc_ref[c], preferred_element_type=F32)
    merged = sga_ref[...].astype(F32) * ya + sgc_ref[...].astype(F32) * yc
    m_ref[c] = merged.astype(BF16)

    @pl.when(c == nchunk - 1)
    def _():
        y = jnp.dot(m_ref[0], wo_ref[0], preferred_element_type=F32)
        for cc in range(1, nchunk):
            y = y + jnp.dot(m_ref[cc], wo_ref[cc], preferred_element_type=F32)
        y = y * lax.rsqrt(jnp.mean(y * y, axis=-1, keepdims=True) + EPS)
        o_ref[...] = x_ref[...] + y * g_ref[...]


def _cast_chunks_kernel(w_ref, o_ref):
    o_ref[0] = w_ref[...].astype(o_ref.dtype)


def _cast_chunks(w, tc, rows=False):
    k, n = w.shape
    blk, imap = ((tc, n), lambda j: (j, 0)) if rows else ((k, tc), lambda j: (0, j))
    nchunk = (k if rows else n) // tc
    return pl.pallas_call(
        _cast_chunks_kernel,
        out_shape=jax.ShapeDtypeStruct((nchunk,) + blk, BF16),
        grid=(nchunk,),
        in_specs=[pl.BlockSpec(blk, imap)],
        out_specs=pl.BlockSpec((1,) + blk, lambda j: (j, 0, 0)),
        compiler_params=_params(("parallel",)),
        name="cast_chunks",
    )(w)


def _merge_out(og, t, gates, row0, wpa, wpc, wo, x, g, tm, tc):
    m, d = x.shape
    assert row0 % tm == 0
    first = row0 // tm
    resident = lambda a: pl.BlockSpec(a.shape, lambda i, c: (0, 0, 0), pipeline_mode=pl.Buffered(1))
    return pl.pallas_call(
        _out_kernel,
        out_shape=jax.ShapeDtypeStruct((m, d), x.dtype),
        grid=(m // tm, d // tc),
        in_specs=[pl.BlockSpec((N_HEADS, tm, D_V), lambda i, c: (0, i, 0)),
                  pl.BlockSpec((tm, CONV_W), lambda i, c: (i, 0)),
                  pl.BlockSpec((tm, tc), lambda i, c: (first + i, c)),
                  pl.BlockSpec((tm, tc), lambda i, c: (first + i, d // tc + c)),
                  resident(wpa), resident(wpc), resident(wo),
                  pl.BlockSpec((tm, d), lambda i, c: (i, 0)),
                  pl.BlockSpec((1, d), lambda i, c: (0, 0))],
        out_specs=pl.BlockSpec((tm, d), lambda i, c: (i, 0)),
        scratch_shapes=[pltpu.VMEM((d // tc, tm, tc), BF16)],
        compiler_params=_params(("parallel", "arbitrary")),
        name="merge_out",
    )(og, t, gates, gates, wpa, wpc, wo, x, g)


def _proj_specs():
    scale = LOG2E / math.sqrt(D_QK)
    heads = (BF16, "heads")
    return (("q", OFF_Q, ATT_W, 1024, lambda r: (r * scale,), (heads,)),
            ("k", OFF_K, ATT_W, 1024, lambda r: (r, r), (F32, heads)),
            ("v", OFF_V, ATT_W, 1024, lambda r: (r, r), (F32, heads)),
            ("za", OFF_ZA, ATT_W, 1024, lambda r: (_silu(r),), (heads,)),
            ("gates", OFF_GA, 2 * D_MODEL, 1024, lambda r: (jax.nn.sigmoid(r),), (BF16,)))


def _layer(xp, xs, conv_p, conv_s, weights, attend_p, attend_s):
    (norm_pre, norm_post, w_in, conv_w, wpa, wpc, wo) = weights
    (bp, tp, d), (bs, ts, _) = xp.shape, xs.shape
    mp, ms = bp * tp, bs * ts
    xp2, xs2 = xp.reshape(mp, d), xs.reshape(ms, d)
    first_tm = 512
    tm = (mp + ms) // 8
    assert tm * 8 == mp + ms and tm % 16 == 0
    res, xn = {}, None
    for name, off, n, tn, epilogue, dtypes in _proj_specs():
        if xn is None:
            xn, *res[name] = _proj((xp2, xs2), w_in, off, n, epilogue, dtypes, first_tm, tn,
                                   "proj_" + name, gain=norm_pre)
        elif F32 in dtypes:
            k_p, k_s, kb_p, kb_s = _proj(xn, w_in, off, n, epilogue, dtypes, first_tm, tn,
                                         "proj_" + name, split=mp)
            res[name] = [(k_p, kb_p), (k_s, kb_s)]
        else:
            res[name] = _proj(xn, w_in, off, n, epilogue, dtypes, tm, tn, "proj_" + name)
    (q,), (sza,), (gates,) = (res[n] for n in ("q", "za", "gates"))
    out = []
    for g, (row0, x2, b, t, conv_state, attend, conv_tiles) in enumerate((
            (0, xp2, bp, tp, conv_p, attend_p, (1, 512, 512)),
            (mp, xs2, bs, ts, conv_s, attend_s, (bs, ts, 512)))):
        m = b * t
        (k, kb), (v, vb) = res["k"][g], res["v"][g]
        tcv, new_conv = _conv_branch(xn, row0, b, t, w_in, conv_state, conv_w, *conv_tiles)
        og = attend(q, kb, vb, sza)
        y = _merge_out(og, tcv.reshape(m, CONV_W), gates, row0, wpa, wpc, wo, x2,
                       norm_post.reshape(1, d), 512, MERGE_CHUNK)
        out.append((y.reshape(b, t, d), k.reshape(b, t, N_HEADS, 2 * D_QK),
                    v.reshape(b, t, N_HEADS, D_V), new_conv))
    return out


def kernel(x_prompt, x_sample, cache_k, cache_v, state_conv, norm_pre, norm_post, w_in,
           lambda_q1, lambda_k1, lambda_q2, lambda_k2, head_norm, conv_w, w_proj_attn,
           w_proj_conv, w_out, rel_bias):
    depth = w_in.shape[0]
    past = cache_k.shape[2]
    dec_b, dec_t = x_sample.shape[0], x_sample.shape[1]
    xp, xs = x_prompt, x_sample
    zero_conv = jnp.zeros((xp.shape[0], CONV_K - 1, CONV_W), xp.dtype)
    pbias = _prompt_bias(rel_bias)
    outs = [[] for _ in range(6)]
    for l in range(depth):
        lam_init = 0.8 - 0.6 * math.exp(-0.3 * l)
        sbc, sbn, lam = _sample_bias_and_lambda(
            rel_bias, lambda_q1[l][None], lambda_k1[l][None], lambda_q2[l][None], lambda_k2[l][None],
            past, dec_t, lam_init)
        hn = head_norm[l].reshape(1, D_V)
        weights = (norm_pre[l], norm_post[l], w_in[l], conv_w[l],
                   _cast_chunks(w_proj_attn[l], MERGE_CHUNK), _cast_chunks(w_proj_conv[l], MERGE_CHUNK),
                   _cast_chunks(w_out[l], MERGE_CHUNK, rows=True))

        pb, pt = x_prompt.shape[0], x_prompt.shape[1]

        def attend_prompt(q, kb, vb, sza):
            return _prompt_attention(lam, q, kb, vb, pbias, sza, hn, lam_init, pb, pt)

        ck = cache_k.reshape(depth * dec_b, past * N_HEADS, D_V)
        cv = cache_v.reshape(depth * dec_b, past * N_HEADS, D_V)

        def attend_sample(q, kb, vb, sza, l=l):
            return _sample_attention(lam, q, ck, cv, kb, vb, sbc, sbn, sza, hn, lam_init, l,
                                     pb * pt, dec_b)

        (xp, kp, vp, cp), (xs, ksm, vsm, csm) = _layer(xp, xs, zero_conv, state_conv[l], weights,
                                                       attend_prompt, attend_sample)
        for lst, val in zip(outs, (kp, vp, cp, ksm, vsm, csm)):
            lst.append(val)
    return (xp, xs) + tuple(jnp.stack(o, axis=0) for o in outs)
```

```python
import functools
import math

import jax
import jax.numpy as jnp
from jax import lax
from jax.experimental import pallas as pl
from jax.experimental.pallas import tpu as pltpu

D_MODEL = 2048
N_HEADS = 8
D_QK = 64
D_V = 2 * D_QK
ATT_W = N_HEADS * D_V
CONV_W = D_MODEL // 2
CONV_K = 3
CHUNK = 64
NUM_BUCKETS = 32
MAX_DISTANCE = 128
EPS = 1e-6
MASKED = -1e30
LOG2E = math.log2(math.e)

OFF_Q, OFF_K, OFF_V, OFF_ZA = 0, 1024, 2048, 3072
OFF_BG, OFF_CG, OFF_H, OFF_ZC = 4096, 5120, 6144, 7168
OFF_GA, OFF_GC = 8192, 10240

ATT_BLOCK = 256
ATT_GROUP = 8
ATT_SLABS = 4
ATT_STRIP = 64
MERGE_CHUNK = 1024
SAMPLE_KEYS = 256
V7X_VMEM_LIMIT = 56 * 1024 * 1024
V7X_LANES = 128

F32 = jnp.float32
BF16 = jnp.bfloat16


def _params(sem, vmem=V7X_VMEM_LIMIT):
    return pltpu.CompilerParams(dimension_semantics=sem, vmem_limit_bytes=vmem)


def _bucket_py(rel):
    half = NUM_BUCKETS // 2
    max_exact = half // 2
    steps = half - max_exact
    n = abs(rel)
    if n < max_exact:
        v = n
    else:
        v = max_exact + sum(n ** steps * max_exact ** j >= MAX_DISTANCE ** j * max_exact ** steps
                            for j in range(1, steps))
    return v + (half if rel > 0 else 0)


NEAR = MAX_DISTANCE
FAR_BUCKET = _bucket_py(-NEAR)
_BREAKS = tuple((rel, _bucket_py(rel)) for rel in range(-NEAR + 1, NEAR + 1)
                if _bucket_py(rel) != _bucket_py(rel - 1))


def _bias_minus_far(rel, value_of, max_rel):
    val = value_of(FAR_BUCKET)
    for first_rel, bucket in _BREAKS:
        if first_rel > max_rel:
            break
        val = jnp.where(rel >= first_rel, value_of(bucket), val)
    return (val - value_of(FAR_BUCKET)) * LOG2E


_BIAS_ROWS = 32


def _pbias_kernel(rb_ref, pb_ref):
    h = pl.program_id(0)
    tb = ATT_BLOCK
    value_of = lambda b: rb_ref[b, h]
    for t in range(2):
        def rows(i, carry, t=t):
            r0 = pl.multiple_of(i * _BIAS_ROWS, _BIAS_ROWS)
            r = r0 + lax.broadcasted_iota(jnp.int32, (_BIAS_ROWS, tb), 0)
            c = lax.broadcasted_iota(jnp.int32, (_BIAS_ROWS, tb), 1)
            val = _bias_minus_far(c - r - (1 - t) * tb, value_of, max_rel=tb - 1 if t == 1 else -1)
            if t == 1:
                val = jnp.where(c // CHUNK <= r // CHUNK, val, MASKED)
            pb_ref[0, t, pl.ds(r0, _BIAS_ROWS), :] = val
            return carry

        lax.fori_loop(0, tb // _BIAS_ROWS, rows, 0)


def _prompt_bias(rel_bias):
    tb = ATT_BLOCK
    return pl.pallas_call(
        _pbias_kernel,
        out_shape=jax.ShapeDtypeStruct((N_HEADS, 2, tb, tb), F32),
        grid=(N_HEADS,),
        in_specs=[pl.BlockSpec(memory_space=pltpu.SMEM)],
        out_specs=pl.BlockSpec((1, 2, tb, tb), lambda h: (h, 0, 0, 0)),
        compiler_params=_params(("parallel",)),
        name="prompt_bias",
    )(rel_bias)


def _sbias_kernel(rb_ref, lq1_ref, lk1_ref, lq2_ref, lk2_ref, sbc_ref, sbn_ref, lam_ref,
                  *, past, lam_init):
    nq = sbn_ref.shape[2]

    def tile(shape, k0, value_of):
        qpos = past + lax.broadcasted_iota(jnp.int32, shape, 0) % nq
        kpos = k0 + lax.broadcasted_iota(jnp.int32, shape, 1)
        val = _bias_minus_far(kpos - qpos, value_of, max_rel=k0 + shape[1] - 1 - past)
        return jnp.where(kpos // CHUNK <= qpos // CHUNK, val, MASKED)

    for h in range(N_HEADS):
        value_of = lambda b, h=h: rb_ref[b, h]
        sbc_ref[h] = tile(sbc_ref.shape[1:], past - sbc_ref.shape[2], value_of)
        sbn_ref[h] = tile(sbn_ref.shape[1:], past, value_of)

    s1 = jnp.sum(lq1_ref[...].astype(F32) * lk1_ref[...].astype(F32), axis=-1, keepdims=True)
    s2 = jnp.sum(lq2_ref[...].astype(F32) * lk2_ref[...].astype(F32), axis=-1, keepdims=True)
    lam_ref[...] = jnp.exp(s1) - jnp.exp(s2) + lam_init


def _sample_bias_and_lambda(rel_bias, lq1, lk1, lq2, lk2, past, nq, lam_init):
    vmem = pl.BlockSpec(memory_space=pltpu.VMEM)
    return pl.pallas_call(
        functools.partial(_sbias_kernel, past=past, lam_init=lam_init),
        out_shape=(jax.ShapeDtypeStruct((N_HEADS, 2 * nq, NEAR), F32),
                   jax.ShapeDtypeStruct((N_HEADS, 2 * nq, nq), F32),
                   jax.ShapeDtypeStruct((1, 1), F32)),
        in_specs=[pl.BlockSpec(memory_space=pltpu.SMEM)] + [vmem] * 4,
        out_specs=(vmem, vmem, vmem),
        name="sample_bias_lambda",
    )(rel_bias, lq1, lk1, lq2, lk2)


def _silu(x):
    return x * jax.nn.sigmoid(x)


def _proj_kernel(*refs, epilogue, n_first, n_split):
    refs = list(refs)
    if n_first is None:
        x = refs.pop(0)[...]
        w_ref = refs.pop(0)
    else:
        xa_ref, xb_ref, g_ref, w_ref, xn_ref = (refs.pop(0) for _ in range(5))

        def prenorm(x_ref):
            xf = x_ref[...].astype(F32)
            xf = xf * lax.rsqrt(jnp.mean(xf * xf, axis=-1, keepdims=True) + EPS)
            xn_ref[...] = (xf * g_ref[...]).astype(BF16)

        i = pl.program_id(1)
        pl.when(i < n_first)(functools.partial(prenorm, xa_ref))
        pl.when(i >= n_first)(functools.partial(prenorm, xb_ref))
        x = xn_ref[...]
    r = jnp.dot(x, w_ref[...].astype(BF16), preferred_element_type=F32)
    results = epilogue(r)

    def store(o_refs):
        for o_ref, o in zip(o_refs, results):
            if len(o_ref.shape) == 3:
                for h in range(o_ref.shape[0]):
                    o_ref[h] = o[:, h * D_V:(h + 1) * D_V].astype(o_ref.dtype)
            else:
                o_ref[...] = o.astype(o_ref.dtype)

    if n_split is None:
        store(refs)
    else:
        i = pl.program_id(1)
        pl.when(i < n_split)(functools.partial(store, refs[0::2]))
        pl.when(i >= n_split)(functools.partial(store, refs[1::2]))


def _proj(x, w, col_off, n, epilogue, out_dtypes, tm, tn, name, gain=None, split=None):
    jb = col_off // tn
    out_shape, out_specs = [], []
    if gain is None:
        m, d = x.shape
        n_first = None
        in_specs = [pl.BlockSpec((tm, d), lambda j, i: (i, 0))]
        args = (x, w)
    else:
        xa, xb = x
        d = xa.shape[1]
        m = xa.shape[0] + xb.shape[0]
        n_first = xa.shape[0] // tm
        assert n == tn and xa.shape[0] % tm == 0 and xb.shape[0] % tm == 0
        in_specs = [pl.BlockSpec((tm, d), lambda j, i: (jnp.minimum(i, n_first - 1), 0)),
                    pl.BlockSpec((tm, d), lambda j, i: (jnp.maximum(i - n_first, 0), 0)),
                    pl.BlockSpec((1, d), lambda j, i: (0, 0))]
        out_shape.append(jax.ShapeDtypeStruct((m, d), BF16))
        out_specs.append(pl.BlockSpec((tm, d), lambda j, i: (i, 0)))
        args = (xa, xb, gain.reshape(1, d), w)
    in_specs.append(pl.BlockSpec((d, tn), lambda j, i: (0, jb + j)))
    if split is None:
        n_split = None
        groups = ((m, lambda i: i),)
    else:
        assert split % tm == 0 and (m - split) % tm == 0
        n_split = split // tm
        groups = ((split, lambda i: jnp.minimum(i, n_split - 1)),
                  (m - split, lambda i: jnp.maximum(i - n_split, 0)))
    for dt in out_dtypes:
        for rows, row_block in groups:
            if isinstance(dt, tuple):
                out_shape.append(jax.ShapeDtypeStruct((n // D_V, rows, D_V), dt[0]))
                out_specs.append(pl.BlockSpec((tn // D_V, tm, D_V),
                                              lambda j, i, rb=row_block: (j, rb(i), 0)))
            else:
                out_shape.append(jax.ShapeDtypeStruct((rows, n), dt))
                out_specs.append(pl.BlockSpec((tm, tn), lambda j, i, rb=row_block: (rb(i), j)))
    return pl.pallas_call(
        functools.partial(_proj_kernel, epilogue=epilogue, n_first=n_first, n_split=n_split),
        out_shape=tuple(out_shape),
        grid=(n // tn, m // tm),
        in_specs=in_specs,
        out_specs=tuple(out_specs),
        compiler_params=_params(("parallel", "arbitrary" if split else "parallel")),
        name=name,
    )(*args)


def _conv_kernel(x_ref, wb_ref, wc_ref, wh_ref, wz_ref, st_ref, cw_ref,
                 t_ref, nc_ref, carry_ref):
    ti = pl.program_id(2)

    @pl.when(ti == 0)
    def _():
        carry_ref[:, 6:8, :] = st_ref[...]

    tb, tt, _ = t_ref.shape
    x = x_ref[...]

    def mm(w_ref):
        return jnp.dot(x, w_ref[...].astype(BF16), preferred_element_type=F32).reshape(tb, tt, -1)

    u = mm(wc_ref) * mm(wh_ref)
    c0 = carry_ref[:, 6:7, :]
    c1 = carry_ref[:, 7:8, :]
    row = lax.broadcasted_iota(jnp.int32, u.shape, 1)
    u1 = jnp.where(row == 0, c1, pltpu.roll(u, 1, 1))
    u2 = jnp.where(row == 0, c0, jnp.where(row == 1, c1, pltpu.roll(u, 2, 1)))
    y = cw_ref[0:1, :] * u2 + cw_ref[1:2, :] * u1 + cw_ref[2:3, :] * u
    carry_ref[...] = u[:, tt - 8:, :]
    nc_ref[...] = carry_ref[:, 6:8, :]
    t_ref[...] = (mm(wb_ref) * y * _silu(mm(wz_ref))).astype(t_ref.dtype)


def _conv_branch(xn, row0, b, t, w, state, conv_w, tb, tt, tc):
    d = xn.shape[1]
    assert row0 % (tb * tt) == 0 and (tb == 1 or tt == t)
    wspec = lambda off: pl.BlockSpec((d, tc), lambda c, bi, ti, o=off // tc: (0, o + c))
    first = row0 // (tb * tt)
    return pl.pallas_call(
        _conv_kernel,
        out_shape=(jax.ShapeDtypeStruct((b, t, CONV_W), BF16),
                   jax.ShapeDtypeStruct((b, CONV_K - 1, CONV_W), F32)),
        grid=(CONV_W // tc, b // tb, t // tt),
        in_specs=[pl.BlockSpec((tb * tt, d), lambda c, bi, ti: (first + bi * (t // tt) + ti, 0)),
                  wspec(OFF_BG), wspec(OFF_CG), wspec(OFF_H), wspec(OFF_ZC),
                  pl.BlockSpec((tb, CONV_K - 1, tc), lambda c, bi, ti: (bi, 0, c)),
                  pl.BlockSpec((CONV_K, tc), lambda c, bi, ti: (0, c))],
        out_specs=(pl.BlockSpec((tb, tt, tc), lambda c, bi, ti: (bi, ti, c)),
                   pl.BlockSpec((tb, CONV_K - 1, tc), lambda c, bi, ti: (bi, 0, c))),
        scratch_shapes=[pltpu.VMEM((tb, 8, tc), F32)],
        compiler_params=_params(("parallel", "parallel", "arbitrary")),
        name="conv_branch",
    )(xn, w, w, w, w, state, conv_w)


_NT = (((1,), (1,)), ((), ()))


def _head_out(o, hn, sza, lam_init):
    o = o * lax.rsqrt(jnp.mean(o * o, axis=-1, keepdims=True) + EPS)
    return o * hn * (1.0 - lam_init) * sza


def _pattn_kernel(lam_ref, q_ref, k_ref, v_ref, bias_ref, sza_ref, hn_ref, o_ref,
                  s_ref, p_ref, *, lam_init):
    gi = pl.program_id(2)
    tb = ATT_BLOCK
    hw = tb // 2

    def block_tasks(n, r):
        rows = slice(r * tb, (r + 1) * tb)
        q = q_ref[0, rows, :]
        lane = lax.broadcasted_iota(jnp.int32, q.shape, 1)
        zero = jnp.zeros_like(q)
        qmaps = (jnp.where(lane < D_QK, q, zero), jnp.where(lane >= D_QK, q, zero))
        chunks = [slice(c * hw, (c + 1) * hw) for c in range(2 * (n + 1))]
        sums = [[], []]
        maxes = [None, None]
        outs = [None, None]
        slab_of = lambda mi: (2 * r + mi) % ATT_SLABS

        def scores(mi):
            mpart = None
            for j in range(n + 1):
                s = lax.dot_general(qmaps[mi], k_ref[0, j * tb:(j + 1) * tb, :], _NT,
                                    preferred_element_type=F32)
                if j == n:
                    s = s + bias_ref[0, 1]
                elif j == n - 1:
                    s = s + bias_ref[0, 0]
                s_ref[slab_of(mi), :, j * tb:(j + 1) * tb] = s
                part = jnp.maximum(s[:, :hw], s[:, hw:])
                mpart = part if mpart is None else jnp.maximum(mpart, part)
                if j == n:
                    maxes[mi] = jnp.max(mpart, axis=-1, keepdims=True)
                yield

        def numerator(mi):
            slab = slab_of(mi)
            for st in range(tb // ATT_STRIP):
                srows = slice(st * ATT_STRIP, (st + 1) * ATT_STRIP)
                m = jnp.broadcast_to(maxes[mi][srows], (ATT_STRIP, hw))
                yield
                lpart = None
                for c in chunks:
                    e = jnp.exp2(s_ref[slab, srows, c] - m)
                    p_ref[slab, srows, c] = e.astype(BF16)
                    lpart = e if lpart is None else lpart + e
                    yield
                sums[mi].append(jnp.sum(lpart, axis=-1, keepdims=True))

        def values(mi):
            o = jnp.dot(p_ref[slab_of(mi), :, :(n + 1) * tb], v_ref[0, :(n + 1) * tb, :],
                        preferred_element_type=F32)
            outs[mi] = o / jnp.concatenate(sums[mi], axis=0)
            if mi == 1:
                o = outs[0] - lam_ref[0, 0] * outs[1]
                o_ref[0, rows, :] = _head_out(o, hn_ref[...], sza_ref[0, rows, :].astype(F32),
                                              lam_init).astype(o_ref.dtype)
            yield

        return [(scores(mi), n + 1, numerator(mi), (tb // ATT_STRIP) * (1 + len(chunks)), values(mi))
                for mi in range(2)]

    def emit_group(g):
        units = [u for r in range(ATT_GROUP) for u in block_tasks(g * ATT_GROUP + r, r)]
        for t in range(len(units) + 2):
            if t >= 2:
                next(units[t - 2][4])
            a, na = (units[t][0], units[t][1]) if t < len(units) else (iter(()), 0)
            b, nb = (units[t - 1][2], units[t - 1][3]) if 1 <= t <= len(units) else (iter(()), 0)
            da = db = 0
            while da < na or db < nb:
                if db >= nb or (da < na and da * nb <= db * na):
                    next(a)
                    da += 1
                else:
                    next(b)
                    db += 1
            for rest in (a, b):
                for _ in rest:
                    pass

    for g in range(k_ref.shape[1] // (tb * ATT_GROUP)):
        pl.when(gi == g)(functools.partial(emit_group, g))


def _prompt_attention(lam, q, kb, vb, pbias, sza, head_norm, lam_init, b, t):
    m = b * t
    tb = ATT_BLOCK
    tg = tb * ATT_GROUP
    assert tb >= NEAR and t % tg == 0
    blk = pl.BlockSpec((1, tg, D_V), lambda bi, h, gi: (h, bi * (t // tg) + gi, 0))
    full = pl.BlockSpec((1, t, D_V), lambda bi, h, gi: (h, bi, 0))
    return pl.pallas_call(
        functools.partial(_pattn_kernel, lam_init=lam_init),
        out_shape=jax.ShapeDtypeStruct((N_HEADS, m, D_V), BF16),
        grid=(b, N_HEADS, t // tg),
        in_specs=[pl.BlockSpec(memory_space=pltpu.SMEM),
                  blk, full, full,
                  pl.BlockSpec((1, 2, tb, tb), lambda bi, h, gi: (h, 0, 0, 0)),
                  blk,
                  pl.BlockSpec((1, D_V), lambda bi, h, gi: (0, 0))],
        out_specs=blk,
        scratch_shapes=[pltpu.VMEM((ATT_SLABS, tb, t + V7X_LANES), F32),
                        pltpu.VMEM((ATT_SLABS, tb, t + V7X_LANES), BF16)],
        compiler_params=_params(("parallel", "parallel", "arbitrary")),
        name="prompt_attention",
    )(lam, q, kb, vb, pbias, sza, head_norm)


def _sattn_kernel(lam_ref, q_ref, kc_ref, vc_ref, kn_ref, vn_ref, bc_ref, bn_ref,
                  sza_ref, hn_ref, o_ref, s_ref, vt_ref, *, lam_init):
    nq = q_ref.shape[1]
    past = kc_ref.shape[1] // N_HEADS
    far_n = past - bc_ref.shape[2]
    lam = lam_ref[0, 0]
    ck = SAMPLE_KEYS
    nchunk = past // ck

    def q2m_of(h):
        q = q_ref[h]
        lane = lax.broadcasted_iota(jnp.int32, q.shape, 1)
        zero = jnp.zeros_like(q)
        return jnp.concatenate([jnp.where(lane < D_QK, q, zero), jnp.where(lane >= D_QK, q, zero)], axis=0)

    q2m = [q2m_of(h) for h in range(N_HEADS)]

    def heads_major(ref, c):
        rows = ref[0, c * ck * N_HEADS:(c + 1) * ck * N_HEADS, :]
        return jnp.swapaxes(rows.reshape(ck, N_HEADS, D_V), 0, 1).astype(BF16)

    def scores(c):
        kt = heads_major(kc_ref, c)
        yield
        for h in range(N_HEADS):
            s_ref[h, :, c * ck:(c + 1) * ck] = lax.dot_general(q2m[h], kt[h], _NT,
                                                               preferred_element_type=F32)
            yield

    def values(c):
        vt = heads_major(vc_ref, c)
        yield
        for h in range(N_HEADS):
            vt_ref[h, c * ck:(c + 1) * ck, :] = vt[h]
        yield

    def head(h):
        s = s_ref[h]
        s = jnp.concatenate([s[:, :far_n], s[:, far_n:] + bc_ref[h]], axis=1)
        sn = lax.dot_general(q2m[h], kn_ref[h], _NT, preferred_element_type=F32) + bn_ref[h]
        mx = jnp.maximum(jnp.max(s, axis=-1, keepdims=True), jnp.max(sn, axis=-1, keepdims=True))
        e = jnp.exp2(s - mx)
        en = jnp.exp2(sn - mx)
        denom = jnp.sum(e, axis=-1, keepdims=True) + jnp.sum(en, axis=-1, keepdims=True)
        row = lax.broadcasted_iota(jnp.int32, denom.shape, 0)
        w = jnp.where(row < nq, 1.0, -lam) / denom

        def combine(x):
            p = x * w
            return (p[:nq] + p[nq:]).astype(BF16)

        a, an = combine(e), combine(en)
        yield
        o = (jnp.dot(a, vt_ref[h], preferred_element_type=F32)
             + jnp.dot(an, vn_ref[h], preferred_element_type=F32))
        o_ref[h] = _head_out(o, hn_ref[...], sza_ref[h].astype(F32), lam_init).astype(o_ref.dtype)
        yield

    def together(*gens):
        gens = list(gens)
        while gens:
            gens = [g for g in gens if next(g, StopIteration) is not StopIteration]

    def first_step(g):
        next(g, None)
        yield

    per_chunk = [scores(c) for c in range(nchunk)]
    next(per_chunk[0])
    for c in range(nchunk):
        ahead = [first_step(per_chunk[c + 1])] if c + 1 < nchunk else []
        together(per_chunk[c], values(c), *ahead)
    for h in range(0, N_HEADS, 2):
        together(head(h), head(h + 1))


def _sample_attention(lam, q, cache_k, cache_v, kb, vb, sbc, sbn, sza, head_norm, lam_init, layer,
                      row0, b):
    nq = sbn.shape[2]
    rows = cache_k.shape[1]
    stacked = pl.BlockSpec((N_HEADS, nq, D_V), lambda bi: (0, row0 // nq + bi, 0))
    own = pl.BlockSpec((N_HEADS, nq, D_V), lambda bi: (0, bi, 0))
    cache = pl.BlockSpec((1, rows, D_V), lambda bi: (layer * b + bi, 0, 0))
    whole = lambda a: pl.BlockSpec(a.shape, lambda bi: (0,) * a.ndim)
    return pl.pallas_call(
        functools.partial(_sattn_kernel, lam_init=lam_init),
        out_shape=jax.ShapeDtypeStruct((N_HEADS, b * nq, D_V), BF16),
        grid=(b,),
        in_specs=[pl.BlockSpec(memory_space=pltpu.SMEM),
                  stacked, cache, cache, own, own, whole(sbc), whole(sbn), stacked, whole(head_norm)],
        out_specs=own,
        scratch_shapes=[pltpu.VMEM((N_HEADS, 2 * nq, rows // N_HEADS), F32),
                        pltpu.VMEM((N_HEADS, rows // N_HEADS, D_V), BF16)],
        compiler_params=_params(("parallel",)),
        name="sample_attention",
    )(lam, q, cache_k, cache_v, kb, vb, sbc, sbn, sza, head_norm)


def _out_kernel(og_ref, t_ref, sga_ref, sgc_ref, wpa_ref, wpc_ref, wo_ref, x_ref, g_ref,
                o_ref, m_ref):
    c = pl.program_id(1)
    nchunk, _, tc = m_ref.shape
    og = jnp.concatenate([og_ref[h] for h in range(og_ref.shape[0])], axis=1)
    ya = jnp.dot(og, wpa_ref[c], preferred_element_type=F32)
    yc = jnp.dot(t_ref[...], wpc_ref[c], preferred_element_type=F32)
    merged = sga_ref[...].astype(F32) * ya + sgc_ref[...].astype(F32) * yc
    m_ref[c] = merged.astype(BF16)

    @pl.when(c == nchunk - 1)
    def _():
        y = jnp.dot(m_ref[0], wo_ref[0], preferred_element_type=F32)
        for cc in range(1, nchunk):
            y = y + jnp.dot(m_ref[cc], wo_ref[cc], preferred_element_type=F32)
        y = y * lax.rsqrt(jnp.mean(y * y, axis=-1, keepdims=True) + EPS)
        o_ref[...] = x_ref[...] + y * g_ref[...]


def _cast_chunks_kernel(w_ref, o_ref):
    o_ref[0] = w_ref[...].astype(o_ref.dtype)


def _cast_chunks(w, tc, rows=False):
    k, n = w.shape
    blk, imap = ((tc, n), lambda j: (j, 0)) if rows else ((k, tc), lambda j: (0, j))
    nchunk = (k if rows else n) // tc
    return pl.pallas_call(
        _cast_chunks_kernel,
        out_shape=jax.ShapeDtypeStruct((nchunk,) + blk, BF16),
        grid=(nchunk,),
        in_specs=[pl.BlockSpec(blk, imap)],
        out_specs=pl.BlockSpec((1,) + blk, lambda j: (j, 0, 0)),
        compiler_params=_params(("parallel",)),
        name="cast_chunks",
    )(w)


def _merge_out(og, t, gates, row0, wpa, wpc, wo, x, g, tm, tc):
    m, d = x.shape
    assert row0 % tm == 0
    first = row0 // tm
    resident = lambda a: pl.BlockSpec(a.shape, lambda i, c: (0, 0, 0), pipeline_mode=pl.Buffered(1))
    return pl.pallas_call(
        _out_kernel,
        out_shape=jax.ShapeDtypeStruct((m, d), x.dtype),
        grid=(m // tm, d // tc),
        in_specs=[pl.BlockSpec((N_HEADS, tm, D_V), lambda i, c: (0, i, 0)),
                  pl.BlockSpec((tm, CONV_W), lambda i, c: (i, 0)),
                  pl.BlockSpec((tm, tc), lambda i, c: (first + i, c)),
                  pl.BlockSpec((tm, tc), lambda i, c: (first + i, d // tc + c)),
                  resident(wpa), resident(wpc), resident(wo),
                  pl.BlockSpec((tm, d), lambda i, c: (i, 0)),
                  pl.BlockSpec((1, d), lambda i, c: (0, 0))],
        out_specs=pl.BlockSpec((tm, d), lambda i, c: (i, 0)),
        scratch_shapes=[pltpu.VMEM((d // tc, tm, tc), BF16)],
        compiler_params=_params(("parallel", "arbitrary")),
        name="merge_out",
    )(og, t, gates, gates, wpa, wpc, wo, x, g)


def _proj_specs():
    scale = LOG2E / math.sqrt(D_QK)
    heads = (BF16, "heads")
    return (("q", OFF_Q, ATT_W, 1024, lambda r: (r * scale,), (heads,)),
            ("k", OFF_K, ATT_W, 1024, lambda r: (r, r), (F32, heads)),
            ("v", OFF_V, ATT_W, 1024, lambda r: (r, r), (F32, heads)),
            ("za", OFF_ZA, ATT_W, 1024, lambda r: (_silu(r),), (heads,)),
            ("gates", OFF_GA, 2 * D_MODEL, 1024, lambda r: (jax.nn.sigmoid(r),), (BF16,)))


def _layer(xp, xs, conv_p, conv_s, weights, attend_p, attend_s):
    (norm_pre, norm_post, w_in, conv_w, wpa, wpc, wo) = weights
    (bp, tp, d), (bs, ts, _) = xp.shape, xs.shape
    mp, ms = bp * tp, bs * ts
    xp2, xs2 = xp.reshape(mp, d), xs.reshape(ms, d)
    first_tm = 512
    tm = (mp + ms) // 8
    assert tm * 8 == mp + ms and tm % 16 == 0
    res, xn = {}, None
    for name, off, n, tn, epilogue, dtypes in _proj_specs():
        if xn is None:
            xn, *res[name] = _proj((xp2, xs2), w_in, off, n, epilogue, dtypes, first_tm, tn,
                                   "proj_" + name, gain=norm_pre)
        elif F32 in dtypes:
            k_p, k_s, kb_p, kb_s = _proj(xn, w_in, off, n, epilogue, dtypes, first_tm, tn,
                                         "proj_" + name, split=mp)
            res[name] = [(k_p, kb_p), (k_s, kb_s)]
        else:
            res[name] = _proj(xn, w_in, off, n, epilogue, dtypes, tm, tn, "proj_" + name)
    (q,), (sza,), (gates,) = (res[n] for n in ("q", "za", "gates"))
    out = []
    for g, (row0, x2, b, t, conv_state, attend, conv_tiles) in enumerate((
            (0, xp2, bp, tp, conv_p, attend_p, (1, 512, 512)),
            (mp, xs2, bs, ts, conv_s, attend_s, (bs, ts, 512)))):
        m = b * t
        (k, kb), (v, vb) = res["k"][g], res["v"][g]
        tcv, new_conv = _conv_branch(xn, row0, b, t, w_in, conv_state, conv_w, *conv_tiles)
        og = attend(q, kb, vb, sza)
        y = _merge_out(og, tcv.reshape(m, CONV_W), gates, row0, wpa, wpc, wo, x2,
                       norm_post.reshape(1, d), 512, MERGE_CHUNK)
        out.append((y.reshape(b, t, d), k.reshape(b, t, N_HEADS, 2 * D_QK),
                    v.reshape(b, t, N_HEADS, D_V), new_conv))
    return out


def kernel(x_prompt, x_sample, cache_k, cache_v, state_conv, norm_pre, norm_post, w_in,
           lambda_q1, lambda_k1, lambda_q2, lambda_k2, head_norm, conv_w, w_proj_attn,
           w_proj_conv, w_out, rel_bias):
    depth = w_in.shape[0]
    past = cache_k.shape[2]
    dec_b, dec_t = x_sample.shape[0], x_sample.shape[1]
    xp, xs = x_prompt, x_sample
    zero_conv = jnp.zeros((xp.shape[0], CONV_K - 1, CONV_W), xp.dtype)
    pbias = _prompt_bias(rel_bias)
    outs = [[] for _ in range(6)]
    for l in range(depth):
        lam_init = 0.8 - 0.6 * math.exp(-0.3 * l)
        sbc, sbn, lam = _sample_bias_and_lambda(
            rel_bias, lambda_q1[l][None], lambda_k1[l][None], lambda_q2[l][None], lambda_k2[l][None],
            past, dec_t, lam_init)
        hn = head_norm[l].reshape(1, D_V)
        weights = (norm_pre[l], norm_post[l], w_in[l], conv_w[l],
                   _cast_chunks(w_proj_attn[l], MERGE_CHUNK), _cast_chunks(w_proj_conv[l], MERGE_CHUNK),
                   _cast_chunks(w_out[l], MERGE_CHUNK, rows=True))

        pb, pt = x_prompt.shape[0], x_prompt.shape[1]

        def attend_prompt(q, kb, vb, sza):
            return _prompt_attention(lam, q, kb, vb, pbias, sza, hn, lam_init, pb, pt)

        ck = cache_k.reshape(depth * dec_b, past * N_HEADS, D_V)
        cv = cache_v.reshape(depth * dec_b, past * N_HEADS, D_V)

        def attend_sample(q, kb, vb, sza, l=l):
            return _sample_attention(lam, q, ck, cv, kb, vb, sbc, sbn, sza, hn, lam_init, l,
                                     pb * pt, dec_b)

        (xp, kp, vp, cp), (xs, ksm, vsm, csm) = _layer(xp, xs, zero_conv, state_conv[l], weights,
                                                       attend_prompt, attend_sample)
        for lst, val in zip(outs, (kp, vp, cp, ksm, vsm, csm)):
            lst.append(val)
    return (xp, xs) + tuple(jnp.stack(o, axis=0) for o in outs)
```

```python
import functools
import math

import jax
import jax.numpy as jnp
from jax import lax
from jax.experimental import pallas as pl
from jax.experimental.pallas import tpu as pltpu

D_MODEL = 2048
N_HEADS = 8
D_QK = 64
D_V = 2 * D_QK
ATT_W = N_HEADS * D_V
CONV_W = D_MODEL // 2
CONV_K = 3
CHUNK = 64
NUM_BUCKETS = 32
MAX_DISTANCE = 128
EPS = 1e-6
MASKED = -1e30
LOG2E = math.log2(math.e)

OFF_Q, OFF_K, OFF_V, OFF_ZA = 0, 1024, 2048, 3072
OFF_BG, OFF_CG, OFF_H, OFF_ZC = 4096, 5120, 6144, 7168
OFF_GA, OFF_GC = 8192, 10240

ATT_BLOCK = 256
ATT_GROUP = 8
ATT_SLABS = 4
ATT_STRIP = 64
MERGE_CHUNK = 1024
SAMPLE_KEYS = 256
V7X_VMEM_LIMIT = 56 * 1024 * 1024
V7X_LANES = 128

F32 = jnp.float32
BF16 = jnp.bfloat16


def _params(sem, vmem=V7X_VMEM_LIMIT):
    return pltpu.CompilerParams(dimension_semantics=sem, vmem_limit_bytes=vmem)


def _bucket_py(rel):
    half = NUM_BUCKETS // 2
    max_exact = half // 2
    steps = half - max_exact
    n = abs(rel)
    if n < max_exact:
        v = n
    else:
        v = max_exact + sum(n ** steps * max_exact ** j >= MAX_DISTANCE ** j * max_exact ** steps
                            for j in range(1, steps))
    return v + (half if rel > 0 else 0)


NEAR = MAX_DISTANCE
FAR_BUCKET = _bucket_py(-NEAR)
_BREAKS = tuple((rel, _bucket_py(rel)) for rel in range(-NEAR + 1, NEAR + 1)
                if _bucket_py(rel) != _bucket_py(rel - 1))


def _bias_minus_far(rel, value_of, max_rel):
    val = value_of(FAR_BUCKET)
    for first_rel, bucket in _BREAKS:
        if first_rel > max_rel:
            break
        val = jnp.where(rel >= first_rel, value_of(bucket), val)
    return (val - value_of(FAR_BUCKET)) * LOG2E


_BIAS_ROWS = 32


def _pbias_kernel(rb_ref, pb_ref):
    h = pl.program_id(0)
    tb = ATT_BLOCK
    value_of = lambda b: rb_ref[b, h]
    for t in range(2):
        def rows(i, carry, t=t):
            r0 = pl.multiple_of(i * _BIAS_ROWS, _BIAS_ROWS)
            r = r0 + lax.broadcasted_iota(jnp.int32, (_BIAS_ROWS, tb), 0)
            c = lax.broadcasted_iota(jnp.int32, (_BIAS_ROWS, tb), 1)
            val = _bias_minus_far(c - r - (1 - t) * tb, value_of, max_rel=tb - 1 if t == 1 else -1)
            if t == 1:
                val = jnp.where(c // CHUNK <= r // CHUNK, val, MASKED)
            pb_ref[0, t, pl.ds(r0, _BIAS_ROWS), :] = val
            return carry

        lax.fori_loop(0, tb // _BIAS_ROWS, rows, 0)


def _prompt_bias(rel_bias):
    tb = ATT_BLOCK
    return pl.pallas_call(
        _pbias_kernel,
        out_shape=jax.ShapeDtypeStruct((N_HEADS, 2, tb, tb), F32),
        grid=(N_HEADS,),
        in_specs=[pl.BlockSpec(memory_space=pltpu.SMEM)],
        out_specs=pl.BlockSpec((1, 2, tb, tb), lambda h: (h, 0, 0, 0)),
        compiler_params=_params(("parallel",)),
        name="prompt_bias",
    )(rel_bias)


def _sbias_kernel(rb_ref, lq1_ref, lk1_ref, lq2_ref, lk2_ref, sbc_ref, sbn_ref, lam_ref,
                  *, past, lam_init):
    nq = sbn_ref.shape[2]

    def tile(shape, k0, value_of):
        qpos = past + lax.broadcasted_iota(jnp.int32, shape, 0) % nq
        kpos = k0 + lax.broadcasted_iota(jnp.int32, shape, 1)
        val = _bias_minus_far(kpos - qpos, value_of, max_rel=k0 + shape[1] - 1 - past)
        return jnp.where(kpos // CHUNK <= qpos // CHUNK, val, MASKED)

    for h in range(N_HEADS):
        value_of = lambda b, h=h: rb_ref[b, h]
        sbc_ref[h] = tile(sbc_ref.shape[1:], past - sbc_ref.shape[2], value_of)
        sbn_ref[h] = tile(sbn_ref.shape[1:], past, value_of)

    s1 = jnp.sum(lq1_ref[...].astype(F32) * lk1_ref[...].astype(F32), axis=-1, keepdims=True)
    s2 = jnp.sum(lq2_ref[...].astype(F32) * lk2_ref[...].astype(F32), axis=-1, keepdims=True)
    lam_ref[...] = jnp.exp(s1) - jnp.exp(s2) + lam_init


def _sample_bias_and_lambda(rel_bias, lq1, lk1, lq2, lk2, past, nq, lam_init):
    vmem = pl.BlockSpec(memory_space=pltpu.VMEM)
    return pl.pallas_call(
        functools.partial(_sbias_kernel, past=past, lam_init=lam_init),
        out_shape=(jax.ShapeDtypeStruct((N_HEADS, 2 * nq, NEAR), F32),
                   jax.ShapeDtypeStruct((N_HEADS, 2 * nq, nq), F32),
                   jax.ShapeDtypeStruct((1, 1), F32)),
        in_specs=[pl.BlockSpec(memory_space=pltpu.SMEM)] + [vmem] * 4,
        out_specs=(vmem, vmem, vmem),
        name="sample_bias_lambda",
    )(rel_bias, lq1, lk1, lq2, lk2)


def _silu(x):
    return x * jax.nn.sigmoid(x)


def _proj_kernel(*refs, epilogue, n_first, n_split):
    refs = list(refs)
    if n_first is None:
        x = refs.pop(0)[...]
        w_ref = refs.pop(0)
    else:
        xa_ref, xb_ref, g_ref, w_ref, xn_ref = (refs.pop(0) for _ in range(5))

        def prenorm(x_ref):
            xf = x_ref[...].astype(F32)
            xf = xf * lax.rsqrt(jnp.mean(xf * xf, axis=-1, keepdims=True) + EPS)
            xn_ref[...] = (xf * g_ref[...]).astype(BF16)

        i = pl.program_id(1)
        pl.when(i < n_first)(functools.partial(prenorm, xa_ref))
        pl.when(i >= n_first)(functools.partial(prenorm, xb_ref))
        x = xn_ref[...]
    r = jnp.dot(x, w_ref[...].astype(BF16), preferred_element_type=F32)
    results = epilogue(r)

    def store(o_refs):
        for o_ref, o in zip(o_refs, results):
            if len(o_ref.shape) == 3:
                for h in range(o_ref.shape[0]):
                    o_ref[h] = o[:, h * D_V:(h + 1) * D_V].astype(o_ref.dtype)
            else:
                o_ref[...] = o.astype(o_ref.dtype)

    if n_split is None:
        store(refs)
    else:
        i = pl.program_id(1)
        pl.when(i < n_split)(functools.partial(store, refs[0::2]))
        pl.when(i >= n_split)(functools.partial(store, refs[1::2]))


def _proj(x, w, col_off, n, epilogue, out_dtypes, tm, tn, name, gain=None, split=None):
    jb = col_off // tn
    out_shape, out_specs = [], []
    if gain is None:
        m, d = x.shape
        n_first = None
        in_specs = [pl.BlockSpec((tm, d), lambda j, i: (i, 0))]
        args = (x, w)
    else:
        xa, xb = x
        d = xa.shape[1]
        m = xa.shape[0] + xb.shape[0]
        n_first = xa.shape[0] // tm
        assert n == tn and xa.shape[0] % tm == 0 and xb.shape[0] % tm == 0
        in_specs = [pl.BlockSpec((tm, d), lambda j, i: (jnp.minimum(i, n_first - 1), 0)),
                    pl.BlockSpec((tm, d), lambda j, i: (jnp.maximum(i - n_first, 0), 0)),
                    pl.BlockSpec((1, d), lambda j, i: (0, 0))]
        out_shape.append(jax.ShapeDtypeStruct((m, d), BF16))
        out_specs.append(pl.BlockSpec((tm, d), lambda j, i: (i, 0)))
        args = (xa, xb, gain.reshape(1, d), w)
    in_specs.append(pl.BlockSpec((d, tn), lambda j, i: (0, jb + j)))
    if split is None:
        n_split = None
        groups = ((m, lambda i: i),)
    else:
        assert split % tm == 0 and (m - split) % tm == 0
        n_split = split // tm
        groups = ((split, lambda i: jnp.minimum(i, n_split - 1)),
                  (m - split, lambda i: jnp.maximum(i - n_split, 0)))
    for dt in out_dtypes:
        for rows, row_block in groups:
            if isinstance(dt, tuple):
                out_shape.append(jax.ShapeDtypeStruct((n // D_V, rows, D_V), dt[0]))
                out_specs.append(pl.BlockSpec((tn // D_V, tm, D_V),
                                              lambda j, i, rb=row_block: (j, rb(i), 0)))
            else:
                out_shape.append(jax.ShapeDtypeStruct((rows, n), dt))
                out_specs.append(pl.BlockSpec((tm, tn), lambda j, i, rb=row_block: (rb(i), j)))
    return pl.pallas_call(
        functools.partial(_proj_kernel, epilogue=epilogue, n_first=n_first, n_split=n_split),
        out_shape=tuple(out_shape),
        grid=(n // tn, m // tm),
        in_specs=in_specs,
        out_specs=tuple(out_specs),
        compiler_params=_params(("parallel", "arbitrary" if split else "parallel")),
        name=name,
    )(*args)


def _conv_kernel(x_ref, wb_ref, wc_ref, wh_ref, wz_ref, st_ref, cw_ref,
                 t_ref, nc_ref, carry_ref):
    ti = pl.program_id(2)

    @pl.when(ti == 0)
    def _():
        carry_ref[:, 6:8, :] = st_ref[...]

    tb, tt, _ = t_ref.shape
    x = x_ref[...]

    def mm(w_ref):
        return jnp.dot(x, w_ref[...].astype(BF16), preferred_element_type=F32).reshape(tb, tt, -1)

    u = mm(wc_ref) * mm(wh_ref)
    c0 = carry_ref[:, 6:7, :]
    c1 = carry_ref[:, 7:8, :]
    row = lax.broadcasted_iota(jnp.int32, u.shape, 1)
    u1 = jnp.where(row == 0, c1, pltpu.roll(u, 1, 1))
    u2 = jnp.where(row == 0, c0, jnp.where(row == 1, c1, pltpu.roll(u, 2, 1)))
    y = cw_ref[0:1, :] * u2 + cw_ref[1:2, :] * u1 + cw_ref[2:3, :] * u
    carry_ref[...] = u[:, tt - 8:, :]
    nc_ref[...] = carry_ref[:, 6:8, :]
    t_ref[...] = (mm(wb_ref) * y * _silu(mm(wz_ref))).astype(t_ref.dtype)


def _conv_branch(xn, row0, b, t, w, state, conv_w, tb, tt, tc):
    d = xn.shape[1]
    assert row0 % (tb * tt) == 0 and (tb == 1 or tt == t)
    wspec = lambda off: pl.BlockSpec((d, tc), lambda c, bi, ti, o=off // tc: (0, o + c))
    first = row0 // (tb * tt)
    return pl.pallas_call(
        _conv_kernel,
        out_shape=(jax.ShapeDtypeStruct((b, t, CONV_W), BF16),
                   jax.ShapeDtypeStruct((b, CONV_K - 1, CONV_W), F32)),
        grid=(CONV_W // tc, b // tb, t // tt),
        in_specs=[pl.BlockSpec((tb * tt, d), lambda c, bi, ti: (first + bi * (t // tt) + ti, 0)),
                  wspec(OFF_BG), wspec(OFF_CG), wspec(OFF_H), wspec(OFF_ZC),
                  pl.BlockSpec((tb, CONV_K - 1, tc), lambda c, bi, ti: (bi, 0, c)),
                  pl.BlockSpec((CONV_K, tc), lambda c, bi, ti: (0, c))],
        out_specs=(pl.BlockSpec((tb, tt, tc), lambda c, bi, ti: (bi, ti, c)),
                   pl.BlockSpec((tb, CONV_K - 1, tc), lambda c, bi, ti: (bi, 0, c))),
        scratch_shapes=[pltpu.VMEM((tb, 8, tc), F32)],
        compiler_params=_params(("parallel", "parallel", "arbitrary")),
        name="conv_branch",
    )(xn, w, w, w, w, state, conv_w)


_NT = (((1,), (1,)), ((), ()))


def _head_out(o, hn, za, lam_init):
    o = o * lax.rsqrt(jnp.mean(o * o, axis=-1, keepdims=True) + EPS)
    return o * hn * (1.0 - lam_init) * _silu(za)


def _pattn_kernel(lam_ref, q_ref, k_ref, v_ref, bias_ref, za_ref, hn_ref, o_ref,
                  s_ref, p_ref, *, lam_init):
    gi = pl.program_id(2)
    tb = ATT_BLOCK
    hw = tb // 2

    def block_tasks(n, r):
        rows = slice(r * tb, (r + 1) * tb)
        q = q_ref[0, rows, :]
        lane = lax.broadcasted_iota(jnp.int32, q.shape, 1)
        zero = jnp.zeros_like(q)
        qmaps = (jnp.where(lane < D_QK, q, zero), jnp.where(lane >= D_QK, q, zero))
        chunks = [slice(c * hw, (c + 1) * hw) for c in range(2 * (n + 1))]
        sums = [[], []]
        maxes = [None, None]
        outs = [None, None]
        slab_of = lambda mi: (2 * r + mi) % ATT_SLABS

        def scores(mi):
            mpart = None
            for j in range(n + 1):
                s = lax.dot_general(qmaps[mi], k_ref[0, j * tb:(j + 1) * tb, :], _NT,
                                    preferred_element_type=F32)
                if j == n:
                    s = s + bias_ref[0, 1]
                elif j == n - 1:
                    s = s + bias_ref[0, 0]
                s_ref[slab_of(mi), :, j * tb:(j + 1) * tb] = s
                part = jnp.maximum(s[:, :hw], s[:, hw:])
                mpart = part if mpart is None else jnp.maximum(mpart, part)
                if j == n:
                    maxes[mi] = jnp.max(mpart, axis=-1, keepdims=True)
                yield

        def numerator(mi):
            slab = slab_of(mi)
            for st in range(tb // ATT_STRIP):
                srows = slice(st * ATT_STRIP, (st + 1) * ATT_STRIP)
                m = jnp.broadcast_to(maxes[mi][srows], (ATT_STRIP, hw))
                yield
                lpart = None
                for c in chunks:
                    e = jnp.exp2(s_ref[slab, srows, c] - m)
                    p_ref[slab, srows, c] = e.astype(BF16)
                    lpart = e if lpart is None else lpart + e
                    yield
                sums[mi].append(jnp.sum(lpart, axis=-1, keepdims=True))

        def values(mi):
            o = jnp.dot(p_ref[slab_of(mi), :, :(n + 1) * tb], v_ref[0, :(n + 1) * tb, :],
                        preferred_element_type=F32)
            outs[mi] = o / jnp.concatenate(sums[mi], axis=0)
            if mi == 1:
                o = outs[0] - lam_ref[0, 0] * outs[1]
                o_ref[0, rows, :] = _head_out(o, hn_ref[...], za_ref[0, rows, :].astype(F32),
                                              lam_init).astype(o_ref.dtype)
            yield

        return [(scores(mi), n + 1, numerator(mi), (tb // ATT_STRIP) * (1 + len(chunks)), values(mi))
                for mi in range(2)]

    def emit_group(g):
        units = [u for r in range(ATT_GROUP) for u in block_tasks(g * ATT_GROUP + r, r)]
        for t in range(len(units) + 2):
            if t >= 2:
                next(units[t - 2][4])
            a, na = (units[t][0], units[t][1]) if t < len(units) else (iter(()), 0)
            b, nb = (units[t - 1][2], units[t - 1][3]) if 1 <= t <= len(units) else (iter(()), 0)
            da = db = 0
            while da < na or db < nb:
                if db >= nb or (da < na and da * nb <= db * na):
                    next(a)
                    da += 1
                else:
                    next(b)
                    db += 1
            for rest in (a, b):
                for _ in rest:
                    pass

    for g in range(k_ref.shape[1] // (tb * ATT_GROUP)):
        pl.when(gi == g)(functools.partial(emit_group, g))


def _prompt_attention(lam, q, kb, vb, pbias, za, head_norm, lam_init, b, t):
    m = b * t
    tb = ATT_BLOCK
    tg = tb * ATT_GROUP
    assert tb >= NEAR and t % tg == 0
    blk = pl.BlockSpec((1, tg, D_V), lambda bi, h, gi: (h, bi * (t // tg) + gi, 0))
    full = pl.BlockSpec((1, t, D_V), lambda bi, h, gi: (h, bi, 0))
    return pl.pallas_call(
        functools.partial(_pattn_kernel, lam_init=lam_init),
        out_shape=jax.ShapeDtypeStruct((N_HEADS, m, D_V), BF16),
        grid=(b, N_HEADS, t // tg),
        in_specs=[pl.BlockSpec(memory_space=pltpu.SMEM),
                  blk, full, full,
                  pl.BlockSpec((1, 2, tb, tb), lambda bi, h, gi: (h, 0, 0, 0)),
                  blk,
                  pl.BlockSpec((1, D_V), lambda bi, h, gi: (0, 0))],
        out_specs=blk,
        scratch_shapes=[pltpu.VMEM((ATT_SLABS, tb, t + V7X_LANES), F32),
                        pltpu.VMEM((ATT_SLABS, tb, t + V7X_LANES), BF16)],
        compiler_params=_params(("parallel", "parallel", "arbitrary")),
        name="prompt_attention",
    )(lam, q, kb, vb, pbias, za, head_norm)


def _sattn_kernel(lam_ref, q_ref, kc_ref, vc_ref, kn_ref, vn_ref, bc_ref, bn_ref,
                  za_ref, hn_ref, o_ref, s_ref, vt_ref, *, lam_init):
    nq = q_ref.shape[1]
    past = kc_ref.shape[1] // N_HEADS
    far_n = past - bc_ref.shape[2]
    lam = lam_ref[0, 0]
    ck = SAMPLE_KEYS
    nchunk = past // ck

    def q2m_of(h):
        q = q_ref[h]
        lane = lax.broadcasted_iota(jnp.int32, q.shape, 1)
        zero = jnp.zeros_like(q)
        return jnp.concatenate([jnp.where(lane < D_QK, q, zero), jnp.where(lane >= D_QK, q, zero)], axis=0)

    q2m = [q2m_of(h) for h in range(N_HEADS)]

    def heads_major(ref, c):
        rows = ref[0, c * ck * N_HEADS:(c + 1) * ck * N_HEADS, :]
        return jnp.swapaxes(rows.reshape(ck, N_HEADS, D_V), 0, 1).astype(BF16)

    def scores(c):
        kt = heads_major(kc_ref, c)
        yield
        for h in range(N_HEADS):
            s_ref[h, :, c * ck:(c + 1) * ck] = lax.dot_general(q2m[h], kt[h], _NT,
                                                               preferred_element_type=F32)
            yield

    def values(c):
        vt = heads_major(vc_ref, c)
        yield
        for h in range(N_HEADS):
            vt_ref[h, c * ck:(c + 1) * ck, :] = vt[h]
        yield

    def head(h):
        s = s_ref[h]
        s = jnp.concatenate([s[:, :far_n], s[:, far_n:] + bc_ref[h]], axis=1)
        sn = lax.dot_general(q2m[h], kn_ref[h], _NT, preferred_element_type=F32) + bn_ref[h]
        mx = jnp.maximum(jnp.max(s, axis=-1, keepdims=True), jnp.max(sn, axis=-1, keepdims=True))
        e = jnp.exp2(s - mx)
        en = jnp.exp2(sn - mx)
        denom = jnp.sum(e, axis=-1, keepdims=True) + jnp.sum(en, axis=-1, keepdims=True)
        row = lax.broadcasted_iota(jnp.int32, denom.shape, 0)
        w = jnp.where(row < nq, 1.0, -lam) / denom

        def combine(x):
            p = x * w
            return (p[:nq] + p[nq:]).astype(BF16)

        a, an = combine(e), combine(en)
        yield
        o = (jnp.dot(a, vt_ref[h], preferred_element_type=F32)
             + jnp.dot(an, vn_ref[h], preferred_element_type=F32))
        o_ref[h] = _head_out(o, hn_ref[...], za_ref[h].astype(F32), lam_init).astype(o_ref.dtype)
        yield

    def together(*gens):
        gens = list(gens)
        while gens:
            gens = [g for g in gens if next(g, StopIteration) is not StopIteration]

    def first_step(g):
        next(g, None)
        yield

    per_chunk = [scores(c) for c in range(nchunk)]
    next(per_chunk[0])
    for c in range(nchunk):
        ahead = [first_step(per_chunk[c + 1])] if c + 1 < nchunk else []
        together(per_chunk[c], values(c), *ahead)
    for h in range(0, N_HEADS, 2):
        together(head(h), head(h + 1))


def _sample_attention(lam, q, cache_k, cache_v, kb, vb, sbc, sbn, za, head_norm, lam_init, layer,
                      row0, b):
    nq = sbn.shape[2]
    rows = cache_k.shape[1]
    stacked = pl.BlockSpec((N_HEADS, nq, D_V), lambda bi: (0, row0 // nq + bi, 0))
    own = pl.BlockSpec((N_HEADS, nq, D_V), lambda bi: (0, bi, 0))
    cache = pl.BlockSpec((1, rows, D_V), lambda bi: (layer * b + bi, 0, 0))
    whole = lambda a: pl.BlockSpec(a.shape, lambda bi: (0,) * a.ndim)
    return pl.pallas_call(
        functools.partial(_sattn_kernel, lam_init=lam_init),
        out_shape=jax.ShapeDtypeStruct((N_HEADS, b * nq, D_V), BF16),
        grid=(b,),
        in_specs=[pl.BlockSpec(memory_space=pltpu.SMEM),
                  stacked, cache, cache, own, own, whole(sbc), whole(sbn), stacked, whole(head_norm)],
        out_specs=own,
        scratch_shapes=[pltpu.VMEM((N_HEADS, 2 * nq, rows // N_HEADS), F32),
                        pltpu.VMEM((N_HEADS, rows // N_HEADS, D_V), BF16)],
        compiler_params=_params(("parallel",)),
        name="sample_attention",
    )(lam, q, cache_k, cache_v, kb, vb, sbc, sbn, za, head_norm)


def _out_kernel(og_ref, t_ref, ga_ref, gc_ref, wpa_ref, wpc_ref, wo_ref, x_ref, g_ref,
                o_ref, m_ref):
    c = pl.program_id(1)
    nchunk, _, tc = m_ref.shape
    og = jnp.concatenate([og_ref[h] for h in range(og_ref.shape[0])], axis=1)
    ya = jnp.dot(og, wpa_ref[c], preferred_element_type=F32)
    yc = jnp.dot(t_ref[...], wpc_ref[c], preferred_element_type=F32)
    merged = (jax.nn.sigmoid(ga_ref[...].astype(F32)) * ya
              + jax.nn.sigmoid(gc_ref[...].astype(F32)) * yc)
    m_ref[c] = merged.astype(BF16)

    @pl.when(c == nchunk - 1)
    def _():
        y = jnp.dot(m_ref[0], wo_ref[0], preferred_element_type=F32)
        for cc in range(1, nchunk):
            y = y + jnp.dot(m_ref[cc], wo_ref[cc], preferred_element_type=F32)
        y = y * lax.rsqrt(jnp.mean(y * y, axis=-1, keepdims=True) + EPS)
        o_ref[...] = x_ref[...] + y * g_ref[...]


def _cast_chunks_kernel(w_ref, o_ref):
    o_ref[0] = w_ref[...].astype(o_ref.dtype)


def _cast_chunks(w, tc, rows=False):
    k, n = w.shape
    blk, imap = ((tc, n), lambda j: (j, 0)) if rows else ((k, tc), lambda j: (0, j))
    nchunk = (k if rows else n) // tc
    return pl.pallas_call(
        _cast_chunks_kernel,
        out_shape=jax.ShapeDtypeStruct((nchunk,) + blk, BF16),
        grid=(nchunk,),
        in_specs=[pl.BlockSpec(blk, imap)],
        out_specs=pl.BlockSpec((1,) + blk, lambda j: (j, 0, 0)),
        compiler_params=_params(("parallel",)),
        name="cast_chunks",
    )(w)


def _merge_out(og, t, gates, row0, wpa, wpc, wo, x, g, tm, tc):
    m, d = x.shape
    assert row0 % tm == 0
    first = row0 // tm
    resident = lambda a: pl.BlockSpec(a.shape, lambda i, c: (0, 0, 0), pipeline_mode=pl.Buffered(1))
    return pl.pallas_call(
        _out_kernel,
        out_shape=jax.ShapeDtypeStruct((m, d), x.dtype),
        grid=(m // tm, d // tc),
        in_specs=[pl.BlockSpec((N_HEADS, tm, D_V), lambda i, c: (0, i, 0)),
                  pl.BlockSpec((tm, CONV_W), lambda i, c: (i, 0)),
                  pl.BlockSpec((tm, tc), lambda i, c: (first + i, c)),
                  pl.BlockSpec((tm, tc), lambda i, c: (first + i, d // tc + c)),
                  resident(wpa), resident(wpc), resident(wo),
                  pl.BlockSpec((tm, d), lambda i, c: (i, 0)),
                  pl.BlockSpec((1, d), lambda i, c: (0, 0))],
        out_specs=pl.BlockSpec((tm, d), lambda i, c: (i, 0)),
        scratch_shapes=[pltpu.VMEM((d // tc, tm, tc), BF16)],
        compiler_params=_params(("parallel", "arbitrary")),
        name="merge_out",
    )(og, t, gates, gates, wpa, wpc, wo, x, g)


def _proj_specs():
    scale = LOG2E / math.sqrt(D_QK)
    heads = (BF16, "heads")
    return (("q", OFF_Q, ATT_W, 1024, lambda r: (r * scale,), (heads,)),
            ("k", OFF_K, ATT_W, 1024, lambda r: (r, r), (F32, heads)),
            ("v", OFF_V, ATT_W, 1024, lambda r: (r, r), (F32, heads)),
            ("za", OFF_ZA, ATT_W, 1024, lambda r: (r,), (heads,)),
            ("gates", OFF_GA, 2 * D_MODEL, 1024, lambda r: (r,), (BF16,)))


def _layer(xp, xs, conv_p, conv_s, weights, attend_p, attend_s):
    (norm_pre, norm_post, w_in, conv_w, wpa, wpc, wo) = weights
    (bp, tp, d), (bs, ts, _) = xp.shape, xs.shape
    mp, ms = bp * tp, bs * ts
    xp2, xs2 = xp.reshape(mp, d), xs.reshape(ms, d)
    first_tm = 512
    tm = (mp + ms) // 8
    assert tm * 8 == mp + ms and tm % 16 == 0
    res, xn = {}, None
    for name, off, n, tn, epilogue, dtypes in _proj_specs():
        if xn is None:
            xn, *res[name] = _proj((xp2, xs2), w_in, off, n, epilogue, dtypes, first_tm, tn,
                                   "proj_" + name, gain=norm_pre)
        elif F32 in dtypes:
            k_p, k_s, kb_p, kb_s = _proj(xn, w_in, off, n, epilogue, dtypes, first_tm, tn,
                                         "proj_" + name, split=mp)
            res[name] = [(k_p, kb_p), (k_s, kb_s)]
        else:
            res[name] = _proj(xn, w_in, off, n, epilogue, dtypes, tm, tn, "proj_" + name)
    (q,), (za,), (gates,) = (res[n] for n in ("q", "za", "gates"))
    out = []
    for g, (row0, x2, b, t, conv_state, attend, conv_tiles) in enumerate((
            (0, xp2, bp, tp, conv_p, attend_p, (1, 512, 512)),
            (mp, xs2, bs, ts, conv_s, attend_s, (bs, ts, 512)))):
        m = b * t
        (k, kb), (v, vb) = res["k"][g], res["v"][g]
        tcv, new_conv = _conv_branch(xn, row0, b, t, w_in, conv_state, conv_w, *conv_tiles)
        og = attend(q, kb, vb, za)
        y = _merge_out(og, tcv.reshape(m, CONV_W), gates, row0, wpa, wpc, wo, x2,
                       norm_post.reshape(1, d), 512, MERGE_CHUNK)
        out.append((y.reshape(b, t, d), k.reshape(b, t, N_HEADS, 2 * D_QK),
                    v.reshape(b, t, N_HEADS, D_V), new_conv))
    return out


def kernel(x_prompt, x_sample, cache_k, cache_v, state_conv, norm_pre, norm_post, w_in,
           lambda_q1, lambda_k1, lambda_q2, lambda_k2, head_norm, conv_w, w_proj_attn,
           w_proj_conv, w_out, rel_bias):
    depth = w_in.shape[0]
    past = cache_k.shape[2]
    dec_b, dec_t = x_sample.shape[0], x_sample.shape[1]
    xp, xs = x_prompt, x_sample
    zero_conv = jnp.zeros((xp.shape[0], CONV_K - 1, CONV_W), xp.dtype)
    pbias = _prompt_bias(rel_bias)
    outs = [[] for _ in range(6)]
    for l in range(depth):
        lam_init = 0.8 - 0.6 * math.exp(-0.3 * l)
        sbc, sbn, lam = _sample_bias_and_lambda(
            rel_bias, lambda_q1[l][None], lambda_k1[l][None], lambda_q2[l][None], lambda_k2[l][None],
            past, dec_t, lam_init)
        hn = head_norm[l].reshape(1, D_V)
        weights = (norm_pre[l], norm_post[l], w_in[l], conv_w[l],
                   _cast_chunks(w_proj_attn[l], MERGE_CHUNK), _cast_chunks(w_proj_conv[l], MERGE_CHUNK),
                   _cast_chunks(w_out[l], MERGE_CHUNK, rows=True))

        pb, pt = x_prompt.shape[0], x_prompt.shape[1]

        def attend_prompt(q, kb, vb, za):
            return _prompt_attention(lam, q, kb, vb, pbias, za, hn, lam_init, pb, pt)

        ck = cache_k.reshape(depth * dec_b, past * N_HEADS, D_V)
        cv = cache_v.reshape(depth * dec_b, past * N_HEADS, D_V)

        def attend_sample(q, kb, vb, za, l=l):
            return _sample_attention(lam, q, ck, cv, kb, vb, sbc, sbn, za, hn, lam_init, l,
                                     pb * pt, dec_b)

        (xp, kp, vp, cp), (xs, ksm, vsm, csm) = _layer(xp, xs, zero_conv, state_conv[l], weights,
                                                       attend_prompt, attend_sample)
        for lst, val in zip(outs, (kp, vp, cp, ksm, vsm, csm)):
            lst.append(val)
    return (xp, xs) + tuple(jnp.stack(o, axis=0) for o in outs)
```

```python
import functools
import math

import jax
import jax.numpy as jnp
from jax import lax
from jax.experimental import pallas as pl
from jax.experimental.pallas import tpu as pltpu

D_MODEL = 2048
N_HEADS = 8
D_QK = 64
D_V = 2 * D_QK
ATT_W = N_HEADS * D_V
CONV_W = D_MODEL // 2
CONV_K = 3
CHUNK = 64
NUM_BUCKETS = 32
MAX_DISTANCE = 128
EPS = 1e-6
MASKED = -1e30
LOG2E = math.log2(math.e)

OFF_Q, OFF_K, OFF_V, OFF_ZA = 0, 1024, 2048, 3072
OFF_BG, OFF_CG, OFF_H, OFF_ZC = 4096, 5120, 6144, 7168
OFF_GA, OFF_GC = 8192, 10240

ATT_BLOCK = 256
ATT_GROUP = 8
ATT_SLABS = 4
ATT_STRIP = 64
MERGE_CHUNK = 1024
SAMPLE_KEYS = 256
V7X_VMEM_LIMIT = 56 * 1024 * 1024
V7X_LANES = 128

F32 = jnp.float32
BF16 = jnp.bfloat16


def _params(sem, vmem=V7X_VMEM_LIMIT):
    return pltpu.CompilerParams(dimension_semantics=sem, vmem_limit_bytes=vmem)


def _bucket_py(rel):
    half = NUM_BUCKETS // 2
    max_exact = half // 2
    steps = half - max_exact
    n = abs(rel)
    if n < max_exact:
        v = n
    else:
        v = max_exact + sum(n ** steps * max_exact ** j >= MAX_DISTANCE ** j * max_exact ** steps
                            for j in range(1, steps))
    return v + (half if rel > 0 else 0)


NEAR = MAX_DISTANCE
FAR_BUCKET = _bucket_py(-NEAR)
_BREAKS = tuple((rel, _bucket_py(rel)) for rel in range(-NEAR + 1, NEAR + 1)
                if _bucket_py(rel) != _bucket_py(rel - 1))


def _bias_minus_far(rel, value_of, max_rel):
    val = value_of(FAR_BUCKET)
    for first_rel, bucket in _BREAKS:
        if first_rel > max_rel:
            break
        val = jnp.where(rel >= first_rel, value_of(bucket), val)
    return (val - value_of(FAR_BUCKET)) * LOG2E


_BIAS_ROWS = 32


def _pbias_kernel(rb_ref, pb_ref):
    h = pl.program_id(0)
    tb = ATT_BLOCK
    value_of = lambda b: rb_ref[b, h]
    for t in range(2):
        def rows(i, carry, t=t):
            r0 = pl.multiple_of(i * _BIAS_ROWS, _BIAS_ROWS)
            r = r0 + lax.broadcasted_iota(jnp.int32, (_BIAS_ROWS, tb), 0)
            c = lax.broadcasted_iota(jnp.int32, (_BIAS_ROWS, tb), 1)
            val = _bias_minus_far(c - r - (1 - t) * tb, value_of, max_rel=tb - 1 if t == 1 else -1)
            if t == 1:
                val = jnp.where(c // CHUNK <= r // CHUNK, val, MASKED)
            pb_ref[0, t, pl.ds(r0, _BIAS_ROWS), :] = val
            return carry

        lax.fori_loop(0, tb // _BIAS_ROWS, rows, 0)


def _prompt_bias(rel_bias):
    tb = ATT_BLOCK
    return pl.pallas_call(
        _pbias_kernel,
        out_shape=jax.ShapeDtypeStruct((N_HEADS, 2, tb, tb), F32),
        grid=(N_HEADS,),
        in_specs=[pl.BlockSpec(memory_space=pltpu.SMEM)],
        out_specs=pl.BlockSpec((1, 2, tb, tb), lambda h: (h, 0, 0, 0)),
        compiler_params=_params(("parallel",)),
        name="prompt_bias",
    )(rel_bias)


def _sbias_kernel(rb_ref, lq1_ref, lk1_ref, lq2_ref, lk2_ref, sbc_ref, sbn_ref, lam_ref,
                  *, past, lam_init):
    nq = sbn_ref.shape[2]

    def tile(shape, k0, value_of):
        qpos = past + lax.broadcasted_iota(jnp.int32, shape, 0) % nq
        kpos = k0 + lax.broadcasted_iota(jnp.int32, shape, 1)
        val = _bias_minus_far(kpos - qpos, value_of, max_rel=k0 + shape[1] - 1 - past)
        return jnp.where(kpos // CHUNK <= qpos // CHUNK, val, MASKED)

    for h in range(N_HEADS):
        value_of = lambda b, h=h: rb_ref[b, h]
        sbc_ref[h] = tile(sbc_ref.shape[1:], past - sbc_ref.shape[2], value_of)
        sbn_ref[h] = tile(sbn_ref.shape[1:], past, value_of)

    s1 = jnp.sum(lq1_ref[...].astype(F32) * lk1_ref[...].astype(F32), axis=-1, keepdims=True)
    s2 = jnp.sum(lq2_ref[...].astype(F32) * lk2_ref[...].astype(F32), axis=-1, keepdims=True)
    lam_ref[...] = jnp.exp(s1) - jnp.exp(s2) + lam_init


def _sample_bias_and_lambda(rel_bias, lq1, lk1, lq2, lk2, past, nq, lam_init):
    vmem = pl.BlockSpec(memory_space=pltpu.VMEM)
    return pl.pallas_call(
        functools.partial(_sbias_kernel, past=past, lam_init=lam_init),
        out_shape=(jax.ShapeDtypeStruct((N_HEADS, 2 * nq, NEAR), F32),
                   jax.ShapeDtypeStruct((N_HEADS, 2 * nq, nq), F32),
                   jax.ShapeDtypeStruct((1, 1), F32)),
        in_specs=[pl.BlockSpec(memory_space=pltpu.SMEM)] + [vmem] * 4,
        out_specs=(vmem, vmem, vmem),
        name="sample_bias_lambda",
    )(rel_bias, lq1, lk1, lq2, lk2)


def _silu(x):
    return x * jax.nn.sigmoid(x)


def _proj_kernel(*refs, epilogue, n_first, n_split):
    refs = list(refs)
    if n_first is None:
        x = refs.pop(0)[...]
        w_ref = refs.pop(0)
    else:
        xa_ref, xb_ref, g_ref, w_ref, xn_ref = (refs.pop(0) for _ in range(5))

        def prenorm(x_ref):
            xf = x_ref[...].astype(F32)
            xf = xf * lax.rsqrt(jnp.mean(xf * xf, axis=-1, keepdims=True) + EPS)
            xn_ref[...] = (xf * g_ref[...]).astype(BF16)

        i = pl.program_id(1)
        pl.when(i < n_first)(functools.partial(prenorm, xa_ref))
        pl.when(i >= n_first)(functools.partial(prenorm, xb_ref))
        x = xn_ref[...]
    r = jnp.dot(x, w_ref[...].astype(BF16), preferred_element_type=F32)
    results = epilogue(r)

    def store(o_refs):
        for o_ref, o in zip(o_refs, results):
            if len(o_ref.shape) == 3:
                for h in range(o_ref.shape[0]):
                    o_ref[h] = o[:, h * D_V:(h + 1) * D_V].astype(o_ref.dtype)
            else:
                o_ref[...] = o.astype(o_ref.dtype)

    if n_split is None:
        store(refs)
    else:
        i = pl.program_id(1)
        pl.when(i < n_split)(functools.partial(store, refs[0::2]))
        pl.when(i >= n_split)(functools.partial(store, refs[1::2]))


def _proj(x, w, col_off, n, epilogue, out_dtypes, tm, tn, name, gain=None, split=None):
    jb = col_off // tn
    out_shape, out_specs = [], []
    if gain is None:
        m, d = x.shape
        n_first = None
        in_specs = [pl.BlockSpec((tm, d), lambda j, i: (i, 0))]
        args = (x, w)
    else:
        xa, xb = x
        d = xa.shape[1]
        m = xa.shape[0] + xb.shape[0]
        n_first = xa.shape[0] // tm
        assert n == tn and xa.shape[0] % tm == 0 and xb.shape[0] % tm == 0
        in_specs = [pl.BlockSpec((tm, d), lambda j, i: (jnp.minimum(i, n_first - 1), 0)),
                    pl.BlockSpec((tm, d), lambda j, i: (jnp.maximum(i - n_first, 0), 0)),
                    pl.BlockSpec((1, d), lambda j, i: (0, 0))]
        out_shape.append(jax.ShapeDtypeStruct((m, d), BF16))
        out_specs.append(pl.BlockSpec((tm, d), lambda j, i: (i, 0)))
        args = (xa, xb, gain.reshape(1, d), w)
    in_specs.append(pl.BlockSpec((d, tn), lambda j, i: (0, jb + j)))
    if split is None:
        n_split = None
        groups = ((m, lambda i: i),)
    else:
        assert split % tm == 0 and (m - split) % tm == 0
        n_split = split // tm
        groups = ((split, lambda i: jnp.minimum(i, n_split - 1)),
                  (m - split, lambda i: jnp.maximum(i - n_split, 0)))
    for dt in out_dtypes:
        for rows, row_block in groups:
            if isinstance(dt, tuple):
                out_shape.append(jax.ShapeDtypeStruct((n // D_V, rows, D_V), dt[0]))
                out_specs.append(pl.BlockSpec((tn // D_V, tm, D_V),
                                              lambda j, i, rb=row_block: (j, rb(i), 0)))
            else:
                out_shape.append(jax.ShapeDtypeStruct((rows, n), dt))
                out_specs.append(pl.BlockSpec((tm, tn), lambda j, i, rb=row_block: (rb(i), j)))
    return pl.pallas_call(
        functools.partial(_proj_kernel, epilogue=epilogue, n_first=n_first, n_split=n_split),
        out_shape=tuple(out_shape),
        grid=(n // tn, m // tm),
        in_specs=in_specs,
        out_specs=tuple(out_specs),
        compiler_params=_params(("parallel", "arbitrary" if split else "parallel")),
        name=name,
    )(*args)


def _conv_kernel(x_ref, wb_ref, wc_ref, wh_ref, wz_ref, st_ref, cw_ref,
                 t_ref, nc_ref, carry_ref):
    ti = pl.program_id(2)

    @pl.when(ti == 0)
    def _():
        carry_ref[:, 6:8, :] = st_ref[...]

    tb, tt, _ = t_ref.shape
    x = x_ref[...]

    def mm(w_ref):
        return jnp.dot(x, w_ref[...].astype(BF16), preferred_element_type=F32).reshape(tb, tt, -1)

    u = mm(wc_ref) * mm(wh_ref)
    c0 = carry_ref[:, 6:7, :]
    c1 = carry_ref[:, 7:8, :]
    row = lax.broadcasted_iota(jnp.int32, u.shape, 1)
    u1 = jnp.where(row == 0, c1, pltpu.roll(u, 1, 1))
    u2 = jnp.where(row == 0, c0, jnp.where(row == 1, c1, pltpu.roll(u, 2, 1)))
    y = cw_ref[0:1, :] * u2 + cw_ref[1:2, :] * u1 + cw_ref[2:3, :] * u
    carry_ref[...] = u[:, tt - 8:, :]
    nc_ref[...] = carry_ref[:, 6:8, :]
    t_ref[...] = (mm(wb_ref) * y * _silu(mm(wz_ref))).astype(t_ref.dtype)


def _conv_branch(xn, row0, b, t, w, state, conv_w, tb, tt, tc):
    d = xn.shape[1]
    assert row0 % (tb * tt) == 0 and (tb == 1 or tt == t)
    wspec = lambda off: pl.BlockSpec((d, tc), lambda c, bi, ti, o=off // tc: (0, o + c))
    first = row0 // (tb * tt)
    return pl.pallas_call(
        _conv_kernel,
        out_shape=(jax.ShapeDtypeStruct((b, t, CONV_W), BF16),
                   jax.ShapeDtypeStruct((b, CONV_K - 1, CONV_W), F32)),
        grid=(CONV_W // tc, b // tb, t // tt),
        in_specs=[pl.BlockSpec((tb * tt, d), lambda c, bi, ti: (first + bi * (t // tt) + ti, 0)),
                  wspec(OFF_BG), wspec(OFF_CG), wspec(OFF_H), wspec(OFF_ZC),
                  pl.BlockSpec((tb, CONV_K - 1, tc), lambda c, bi, ti: (bi, 0, c)),
                  pl.BlockSpec((CONV_K, tc), lambda c, bi, ti: (0, c))],
        out_specs=(pl.BlockSpec((tb, tt, tc), lambda c, bi, ti: (bi, ti, c)),
                   pl.BlockSpec((tb, CONV_K - 1, tc), lambda c, bi, ti: (bi, 0, c))),
        scratch_shapes=[pltpu.VMEM((tb, 8, tc), F32)],
        compiler_params=_params(("parallel", "parallel", "arbitrary")),
        name="conv_branch",
    )(xn, w, w, w, w, state, conv_w)


_NT = (((1,), (1,)), ((), ()))


def _head_out(o, hn, za, lam_init):
    o = o * lax.rsqrt(jnp.mean(o * o, axis=-1, keepdims=True) + EPS)
    return o * hn * (1.0 - lam_init) * _silu(za)


def _pattn_kernel(lam_ref, q_ref, k_ref, v_ref, bias_ref, za_ref, hn_ref, o_ref,
                  s_ref, p_ref, *, lam_init):
    gi = pl.program_id(2)
    tb = ATT_BLOCK
    hw = tb // 2

    def block_tasks(n, r):
        rows = slice(r * tb, (r + 1) * tb)
        q = q_ref[0, rows, :]
        lane = lax.broadcasted_iota(jnp.int32, q.shape, 1)
        zero = jnp.zeros_like(q)
        qmaps = (jnp.where(lane < D_QK, q, zero), jnp.where(lane >= D_QK, q, zero))
        chunks = [slice(c * hw, (c + 1) * hw) for c in range(2 * (n + 1))]
        sums = [[], []]
        maxes = [None, None]
        outs = [None, None]
        slab_of = lambda mi: (2 * r + mi) % ATT_SLABS

        def scores(mi):
            mpart = None
            for j in range(n + 1):
                s = lax.dot_general(qmaps[mi], k_ref[0, j * tb:(j + 1) * tb, :], _NT,
                                    preferred_element_type=F32)
                if j == n:
                    s = s + bias_ref[0, 1]
                elif j == n - 1:
                    s = s + bias_ref[0, 0]
                s_ref[slab_of(mi), :, j * tb:(j + 1) * tb] = s
                part = jnp.maximum(s[:, :hw], s[:, hw:])
                mpart = part if mpart is None else jnp.maximum(mpart, part)
                if j == n:
                    maxes[mi] = jnp.max(mpart, axis=-1, keepdims=True)
                yield

        def numerator(mi):
            slab = slab_of(mi)
            for st in range(tb // ATT_STRIP):
                srows = slice(st * ATT_STRIP, (st + 1) * ATT_STRIP)
                m = jnp.broadcast_to(maxes[mi][srows], (ATT_STRIP, hw))
                yield
                lpart = None
                for c in chunks:
                    e = jnp.exp2(s_ref[slab, srows, c] - m)
                    p_ref[slab, srows, c] = e.astype(BF16)
                    lpart = e if lpart is None else lpart + e
                    yield
                sums[mi].append(jnp.sum(lpart, axis=-1, keepdims=True))

        def values(mi):
            o = jnp.dot(p_ref[slab_of(mi), :, :(n + 1) * tb], v_ref[0, :(n + 1) * tb, :],
                        preferred_element_type=F32)
            outs[mi] = o / jnp.concatenate(sums[mi], axis=0)
            if mi == 1:
                o = outs[0] - lam_ref[0, 0] * outs[1]
                o_ref[0, rows, :] = _head_out(o, hn_ref[...], za_ref[0, rows, :].astype(F32),
                                              lam_init).astype(o_ref.dtype)
            yield

        return [(scores(mi), n + 1, numerator(mi), (tb // ATT_STRIP) * (1 + len(chunks)), values(mi))
                for mi in range(2)]

    def emit_group(g):
        units = [u for r in range(ATT_GROUP) for u in block_tasks(g * ATT_GROUP + r, r)]
        for t in range(len(units) + 2):
            if t >= 2:
                next(units[t - 2][4])
            a, na = (units[t][0], units[t][1]) if t < len(units) else (iter(()), 0)
            b, nb = (units[t - 1][2], units[t - 1][3]) if 1 <= t <= len(units) else (iter(()), 0)
            da = db = 0
            while da < na or db < nb:
                if db >= nb or (da < na and da * nb <= db * na):
                    next(a)
                    da += 1
                else:
                    next(b)
                    db += 1
            for rest in (a, b):
                for _ in rest:
                    pass

    for g in range(k_ref.shape[1] // (tb * ATT_GROUP)):
        pl.when(gi == g)(functools.partial(emit_group, g))


def _prompt_attention(lam, q, kb, vb, pbias, za, head_norm, lam_init, b, t):
    m = b * t
    tb = ATT_BLOCK
    tg = tb * ATT_GROUP
    assert tb >= NEAR and t % tg == 0
    blk = pl.BlockSpec((1, tg, D_V), lambda bi, h, gi: (h, bi * (t // tg) + gi, 0))
    full = pl.BlockSpec((1, t, D_V), lambda bi, h, gi: (h, bi, 0))
    return pl.pallas_call(
        functools.partial(_pattn_kernel, lam_init=lam_init),
        out_shape=jax.ShapeDtypeStruct((N_HEADS, m, D_V), BF16),
        grid=(b, N_HEADS, t // tg),
        in_specs=[pl.BlockSpec(memory_space=pltpu.SMEM),
                  blk, full, full,
                  pl.BlockSpec((1, 2, tb, tb), lambda bi, h, gi: (h, 0, 0, 0)),
                  blk,
                  pl.BlockSpec((1, D_V), lambda bi, h, gi: (0, 0))],
        out_specs=blk,
        scratch_shapes=[pltpu.VMEM((ATT_SLABS, tb, t + V7X_LANES), F32),
                        pltpu.VMEM((ATT_SLABS, tb, t + V7X_LANES), BF16)],
        compiler_params=_params(("parallel", "parallel", "arbitrary")),
        name="prompt_attention",
    )(lam, q, kb, vb, pbias, za, head_norm)


def _sattn_kernel(lam_ref, q_ref, kc_ref, vc_ref, kn_ref, vn_ref, bc_ref, bn_ref,
                  za_ref, hn_ref, o_ref, s_ref, vt_ref, *, lam_init):
    nq = q_ref.shape[1]
    past = kc_ref.shape[1] // N_HEADS
    far_n = past - bc_ref.shape[2]
    lam = lam_ref[0, 0]
    ck = SAMPLE_KEYS
    nchunk = past // ck

    def q2m_of(h):
        q = q_ref[h]
        lane = lax.broadcasted_iota(jnp.int32, q.shape, 1)
        zero = jnp.zeros_like(q)
        return jnp.concatenate([jnp.where(lane < D_QK, q, zero), jnp.where(lane >= D_QK, q, zero)], axis=0)

    q2m = [q2m_of(h) for h in range(N_HEADS)]

    def heads_major(ref, c):
        rows = ref[0, c * ck * N_HEADS:(c + 1) * ck * N_HEADS, :]
        return jnp.swapaxes(rows.reshape(ck, N_HEADS, D_V), 0, 1).astype(BF16)

    def scores(c):
        kt = heads_major(kc_ref, c)
        yield
        for h in range(N_HEADS):
            s_ref[h, :, c * ck:(c + 1) * ck] = lax.dot_general(q2m[h], kt[h], _NT,
                                                               preferred_element_type=F32)
            yield

    def values(c):
        vt = heads_major(vc_ref, c)
        yield
        for h in range(N_HEADS):
            vt_ref[h, c * ck:(c + 1) * ck, :] = vt[h]
        yield

    def head(h):
        s = s_ref[h, :, :past]
        s = jnp.concatenate([s[:, :far_n], s[:, far_n:] + bc_ref[h]], axis=1)
        sn = lax.dot_general(q2m[h], kn_ref[h], _NT, preferred_element_type=F32) + bn_ref[h]
        mx = jnp.maximum(jnp.max(s, axis=-1, keepdims=True), jnp.max(sn, axis=-1, keepdims=True))
        e = jnp.exp2(s - mx)
        en = jnp.exp2(sn - mx)
        denom = jnp.sum(e, axis=-1, keepdims=True) + jnp.sum(en, axis=-1, keepdims=True)
        row = lax.broadcasted_iota(jnp.int32, denom.shape, 0)
        w = jnp.where(row < nq, 1.0, -lam) / denom

        def combine(x):
            p = x * w
            return (p[:nq] + p[nq:]).astype(BF16)

        a, an = combine(e), combine(en)
        yield
        o = (jnp.dot(a, vt_ref[h], preferred_element_type=F32)
             + jnp.dot(an, vn_ref[h], preferred_element_type=F32))
        o_ref[h] = _head_out(o, hn_ref[...], za_ref[h].astype(F32), lam_init).astype(o_ref.dtype)
        yield

    def together(*gens):
        gens = list(gens)
        while gens:
            gens = [g for g in gens if next(g, StopIteration) is not StopIteration]

    def first_step(g):
        next(g, None)
        yield

    per_chunk = [scores(c) for c in range(nchunk)]
    next(per_chunk[0])
    for c in range(nchunk):
        ahead = [first_step(per_chunk[c + 1])] if c + 1 < nchunk else []
        together(per_chunk[c], values(c), *ahead)
    for h in range(0, N_HEADS, 2):
        together(head(h), head(h + 1))


def _sample_attention(lam, q, cache_k, cache_v, kb, vb, sbc, sbn, za, head_norm, lam_init, layer,
                      row0, b):
    nq = sbn.shape[2]
    rows = cache_k.shape[1]
    stacked = pl.BlockSpec((N_HEADS, nq, D_V), lambda bi: (0, row0 // nq + bi, 0))
    own = pl.BlockSpec((N_HEADS, nq, D_V), lambda bi: (0, bi, 0))
    cache = pl.BlockSpec((1, rows, D_V), lambda bi: (layer * b + bi, 0, 0))
    whole = lambda a: pl.BlockSpec(a.shape, lambda bi: (0,) * a.ndim)
    return pl.pallas_call(
        functools.partial(_sattn_kernel, lam_init=lam_init),
        out_shape=jax.ShapeDtypeStruct((N_HEADS, b * nq, D_V), BF16),
        grid=(b,),
        in_specs=[pl.BlockSpec(memory_space=pltpu.SMEM),
                  stacked, cache, cache, own, own, whole(sbc), whole(sbn), stacked, whole(head_norm)],
        out_specs=own,
        scratch_shapes=[pltpu.VMEM((N_HEADS, 2 * nq, rows // N_HEADS + V7X_LANES), F32),
                        pltpu.VMEM((N_HEADS, rows // N_HEADS, D_V), BF16)],
        compiler_params=_params(("parallel",)),
        name="sample_attention",
    )(lam, q, cache_k, cache_v, kb, vb, sbc, sbn, za, head_norm)


def _out_kernel(og_ref, t_ref, ga_ref, gc_ref, wpa_ref, wpc_ref, wo_ref, x_ref, g_ref,
                o_ref, m_ref):
    c = pl.program_id(1)
    nchunk, _, tc = m_ref.shape
    og = jnp.concatenate([og_ref[h] for h in range(og_ref.shape[0])], axis=1)
    ya = jnp.dot(og, wpa_ref[c], preferred_element_type=F32)
    yc = jnp.dot(t_ref[...], wpc_ref[c], preferred_element_type=F32)
    merged = (jax.nn.sigmoid(ga_ref[...].astype(F32)) * ya
              + jax.nn.sigmoid(gc_ref[...].astype(F32)) * yc)
    m_ref[c] = merged.astype(BF16)

    @pl.when(c == nchunk - 1)
    def _():
        y = jnp.dot(m_ref[0], wo_ref[0], preferred_element_type=F32)
        for cc in range(1, nchunk):
            y = y + jnp.dot(m_ref[cc], wo_ref[cc], preferred_element_type=F32)
        y = y * lax.rsqrt(jnp.mean(y * y, axis=-1, keepdims=True) + EPS)
        o_ref[...] = x_ref[...] + y * g_ref[...]


def _cast_chunks_kernel(w_ref, o_ref):
    o_ref[0] = w_ref[...].astype(o_ref.dtype)


def _cast_chunks(w, tc, rows=False):
    k, n = w.shape
    blk, imap = ((tc, n), lambda j: (j, 0)) if rows else ((k, tc), lambda j: (0, j))
    nchunk = (k if rows else n) // tc
    return pl.pallas_call(
        _cast_chunks_kernel,
        out_shape=jax.ShapeDtypeStruct((nchunk,) + blk, BF16),
        grid=(nchunk,),
        in_specs=[pl.BlockSpec(blk, imap)],
        out_specs=pl.BlockSpec((1,) + blk, lambda j: (j, 0, 0)),
        compiler_params=_params(("parallel",)),
        name="cast_chunks",
    )(w)


def _merge_out(og, t, gates, row0, wpa, wpc, wo, x, g, tm, tc):
    m, d = x.shape
    assert row0 % tm == 0
    first = row0 // tm
    resident = lambda a: pl.BlockSpec(a.shape, lambda i, c: (0, 0, 0), pipeline_mode=pl.Buffered(1))
    return pl.pallas_call(
        _out_kernel,
        out_shape=jax.ShapeDtypeStruct((m, d), x.dtype),
        grid=(m // tm, d // tc),
        in_specs=[pl.BlockSpec((N_HEADS, tm, D_V), lambda i, c: (0, i, 0)),
                  pl.BlockSpec((tm, CONV_W), lambda i, c: (i, 0)),
                  pl.BlockSpec((tm, tc), lambda i, c: (first + i, c)),
                  pl.BlockSpec((tm, tc), lambda i, c: (first + i, d // tc + c)),
                  resident(wpa), resident(wpc), resident(wo),
                  pl.BlockSpec((tm, d), lambda i, c: (i, 0)),
                  pl.BlockSpec((1, d), lambda i, c: (0, 0))],
        out_specs=pl.BlockSpec((tm, d), lambda i, c: (i, 0)),
        scratch_shapes=[pltpu.VMEM((d // tc, tm, tc), BF16)],
        compiler_params=_params(("parallel", "arbitrary")),
        name="merge_out",
    )(og, t, gates, gates, wpa, wpc, wo, x, g)


def _proj_specs():
    scale = LOG2E / math.sqrt(D_QK)
    heads = (BF16, "heads")
    return (("q", OFF_Q, ATT_W, 1024, lambda r: (r * scale,), (heads,)),
            ("k", OFF_K, ATT_W, 1024, lambda r: (r, r), (F32, heads)),
            ("v", OFF_V, ATT_W, 1024, lambda r: (r, r), (F32, heads)),
            ("za", OFF_ZA, ATT_W, 1024, lambda r: (r,), (heads,)),
            ("gates", OFF_GA, 2 * D_MODEL, 1024, lambda r: (r,), (BF16,)))


def _layer(xp, xs, conv_p, conv_s, weights, attend_p, attend_s):
    (norm_pre, norm_post, w_in, conv_w, wpa, wpc, wo) = weights
    (bp, tp, d), (bs, ts, _) = xp.shape, xs.shape
    mp, ms = bp * tp, bs * ts
    xp2, xs2 = xp.reshape(mp, d), xs.reshape(ms, d)
    first_tm = 512
    tm = (mp + ms) // 8
    assert tm * 8 == mp + ms and tm % 16 == 0
    res, xn = {}, None
    for name, off, n, tn, epilogue, dtypes in _proj_specs():
        if xn is None:
            xn, *res[name] = _proj((xp2, xs2), w_in, off, n, epilogue, dtypes, first_tm, tn,
                                   "proj_" + name, gain=norm_pre)
        elif F32 in dtypes:
            k_p, k_s, kb_p, kb_s = _proj(xn, w_in, off, n, epilogue, dtypes, first_tm, tn,
                                         "proj_" + name, split=mp)
            res[name] = [(k_p, kb_p), (k_s, kb_s)]
        else:
            res[name] = _proj(xn, w_in, off, n, epilogue, dtypes, tm, tn, "proj_" + name)
    (q,), (za,), (gates,) = (res[n] for n in ("q", "za", "gates"))
    out = []
    for g, (row0, x2, b, t, conv_state, attend, conv_tiles) in enumerate((
            (0, xp2, bp, tp, conv_p, attend_p, (1, 1024, 512)),
            (mp, xs2, bs, ts, conv_s, attend_s, (bs, ts, 512)))):
        m = b * t
        (k, kb), (v, vb) = res["k"][g], res["v"][g]
        tcv, new_conv = _conv_branch(xn, row0, b, t, w_in, conv_state, conv_w, *conv_tiles)
        og = attend(q, kb, vb, za)
        y = _merge_out(og, tcv.reshape(m, CONV_W), gates, row0, wpa, wpc, wo, x2,
                       norm_post.reshape(1, d), 512, MERGE_CHUNK)
        out.append((y.reshape(b, t, d), k.reshape(b, t, N_HEADS, 2 * D_QK),
                    v.reshape(b, t, N_HEADS, D_V), new_conv))
    return out


def kernel(x_prompt, x_sample, cache_k, cache_v, state_conv, norm_pre, norm_post, w_in,
           lambda_q1, lambda_k1, lambda_q2, lambda_k2, head_norm, conv_w, w_proj_attn,
           w_proj_conv, w_out, rel_bias):
    depth = w_in.shape[0]
    past = cache_k.shape[2]
    dec_b, dec_t = x_sample.shape[0], x_sample.shape[1]
    xp, xs = x_prompt, x_sample
    zero_conv = jnp.zeros((xp.shape[0], CONV_K - 1, CONV_W), xp.dtype)
    pbias = _prompt_bias(rel_bias)
    outs = [[] for _ in range(6)]
    for l in range(depth):
        lam_init = 0.8 - 0.6 * math.exp(-0.3 * l)
        sbc, sbn, lam = _sample_bias_and_lambda(
            rel_bias, lambda_q1[l][None], lambda_k1[l][None], lambda_q2[l][None], lambda_k2[l][None],
            past, dec_t, lam_init)
        hn = head_norm[l].reshape(1, D_V)
        weights = (norm_pre[l], norm_post[l], w_in[l], conv_w[l],
                   _cast_chunks(w_proj_attn[l], MERGE_CHUNK), _cast_chunks(w_proj_conv[l], MERGE_CHUNK),
                   _cast_chunks(w_out[l], MERGE_CHUNK, rows=True))

        pb, pt = x_prompt.shape[0], x_prompt.shape[1]

        def attend_prompt(q, kb, vb, za):
            return _prompt_attention(lam, q, kb, vb, pbias, za, hn, lam_init, pb, pt)

        ck = cache_k.reshape(depth * dec_b, past * N_HEADS, D_V)
        cv = cache_v.reshape(depth * dec_b, past * N_HEADS, D_V)

        def attend_sample(q, kb, vb, za, l=l):
            return _sample_attention(lam, q, ck, cv, kb, vb, sbc, sbn, za, hn, lam_init, l,
                                     pb * pt, dec_b)

        (xp, kp, vp, cp), (xs, ksm, vsm, csm) = _layer(xp, xs, zero_conv, state_conv[l], weights,
                                                       attend_prompt, attend_sample)
        for lst, val in zip(outs, (kp, vp, cp, ksm, vsm, csm)):
            lst.append(val)
    return (xp, xs) + tuple(jnp.stack(o, axis=0) for o in outs)
```

```python
import functools
import math

import jax
import jax.numpy as jnp
from jax import lax
from jax.experimental import pallas as pl
from jax.experimental.pallas import tpu as pltpu

D_MODEL = 2048
N_HEADS = 8
D_QK = 64
D_V = 2 * D_QK
ATT_W = N_HEADS * D_V
CONV_W = D_MODEL // 2
CONV_K = 3
CHUNK = 64
NUM_BUCKETS = 32
MAX_DISTANCE = 128
EPS = 1e-6
MASKED = -1e30
LOG2E = math.log2(math.e)

OFF_Q, OFF_K, OFF_V, OFF_ZA = (i * ATT_W for i in range(4))
OFF_BG, OFF_CG, OFF_H, OFF_ZC = (4 * ATT_W + i * CONV_W for i in range(4))
OFF_GA = 4 * ATT_W + 4 * CONV_W

ATT_BLOCK = 256
ATT_GROUP = 8
ATT_SLABS = 4
ATT_STRIP = 64
PROJ_COLS = 1024
PROJ_ROWS = 512
MERGE_CHUNK = 1024
SAMPLE_KEYS = 256
V7X_VMEM_LIMIT = 56 * 1024 * 1024
V7X_LANES = 128

F32 = jnp.float32
BF16 = jnp.bfloat16


def _params(sem, vmem=V7X_VMEM_LIMIT):
    return pltpu.CompilerParams(dimension_semantics=sem, vmem_limit_bytes=vmem)


def _bucket_py(rel):
    half = NUM_BUCKETS // 2
    max_exact = half // 2
    steps = half - max_exact
    n = abs(rel)
    if n < max_exact:
        v = n
    else:
        v = max_exact + sum(n ** steps * max_exact ** j >= MAX_DISTANCE ** j * max_exact ** steps
                            for j in range(1, steps))
    return v + (half if rel > 0 else 0)


NEAR = MAX_DISTANCE
FAR_BUCKET = _bucket_py(-NEAR)
_BREAKS = tuple((rel, _bucket_py(rel)) for rel in range(-NEAR + 1, NEAR + 1)
                if _bucket_py(rel) != _bucket_py(rel - 1))


def _bias_minus_far(rel, value_of, max_rel):
    val = value_of(FAR_BUCKET)
    for first_rel, bucket in _BREAKS:
        if first_rel > max_rel:
            break
        val = jnp.where(rel >= first_rel, value_of(bucket), val)
    return (val - value_of(FAR_BUCKET)) * LOG2E


_BIAS_ROWS = 32


def _pbias_kernel(rb_ref, pb_ref):
    h = pl.program_id(0)
    tb = ATT_BLOCK
    value_of = lambda b: rb_ref[b, h]
    for t in range(2):
        def rows(i, carry, t=t):
            r0 = pl.multiple_of(i * _BIAS_ROWS, _BIAS_ROWS)
            r = r0 + lax.broadcasted_iota(jnp.int32, (_BIAS_ROWS, tb), 0)
            c = lax.broadcasted_iota(jnp.int32, (_BIAS_ROWS, tb), 1)
            val = _bias_minus_far(c - r - (1 - t) * tb, value_of, max_rel=tb - 1 if t == 1 else -1)
            if t == 1:
                val = jnp.where(c // CHUNK <= r // CHUNK, val, MASKED)
            pb_ref[0, t, pl.ds(r0, _BIAS_ROWS), :] = val
            return carry

        lax.fori_loop(0, tb // _BIAS_ROWS, rows, 0)


def _prompt_bias(rel_bias):
    tb = ATT_BLOCK
    return pl.pallas_call(
        _pbias_kernel,
        out_shape=jax.ShapeDtypeStruct((N_HEADS, 2, tb, tb), F32),
        grid=(N_HEADS,),
        in_specs=[pl.BlockSpec(memory_space=pltpu.SMEM)],
        out_specs=pl.BlockSpec((1, 2, tb, tb), lambda h: (h, 0, 0, 0)),
        compiler_params=_params(("parallel",)),
        name="prompt_bias",
    )(rel_bias)


def _sbias_kernel(rb_ref, lq1_ref, lk1_ref, lq2_ref, lk2_ref, sbc_ref, sbn_ref, lam_ref,
                  *, past, lam_init):
    nq = sbn_ref.shape[2]

    def tile(shape, k0, value_of):
        qpos = past + lax.broadcasted_iota(jnp.int32, shape, 0) % nq
        kpos = k0 + lax.broadcasted_iota(jnp.int32, shape, 1)
        val = _bias_minus_far(kpos - qpos, value_of, max_rel=k0 + shape[1] - 1 - past)
        return jnp.where(kpos // CHUNK <= qpos // CHUNK, val, MASKED)

    for h in range(N_HEADS):
        value_of = lambda b, h=h: rb_ref[b, h]
        sbc_ref[h] = tile(sbc_ref.shape[1:], past - sbc_ref.shape[2], value_of)
        sbn_ref[h] = tile(sbn_ref.shape[1:], past, value_of)

    s1 = jnp.sum(lq1_ref[...].astype(F32) * lk1_ref[...].astype(F32), axis=-1, keepdims=True)
    s2 = jnp.sum(lq2_ref[...].astype(F32) * lk2_ref[...].astype(F32), axis=-1, keepdims=True)
    lam_ref[...] = jnp.exp(s1) - jnp.exp(s2) + lam_init


def _sample_bias_and_lambda(rel_bias, lq1, lk1, lq2, lk2, past, nq, lam_init):
    vmem = pl.BlockSpec(memory_space=pltpu.VMEM)
    return pl.pallas_call(
        functools.partial(_sbias_kernel, past=past, lam_init=lam_init),
        out_shape=(jax.ShapeDtypeStruct((N_HEADS, 2 * nq, NEAR), F32),
                   jax.ShapeDtypeStruct((N_HEADS, 2 * nq, nq), F32),
                   jax.ShapeDtypeStruct((1, 1), F32)),
        in_specs=[pl.BlockSpec(memory_space=pltpu.SMEM)] + [vmem] * 4,
        out_specs=(vmem, vmem, vmem),
        name="sample_bias_lambda",
    )(rel_bias, lq1, lk1, lq2, lk2)


def _silu(x):
    return x * jax.nn.sigmoid(x)


def _proj_kernel(*refs, epilogue, n_first, n_split):
    refs = list(refs)
    if n_first is None:
        x = refs.pop(0)[...]
        w_ref = refs.pop(0)
    else:
        xa_ref, xb_ref, g_ref, w_ref, xn_ref = (refs.pop(0) for _ in range(5))

        def prenorm(x_ref):
            xf = x_ref[...].astype(F32)
            xf = xf * lax.rsqrt(jnp.mean(xf * xf, axis=-1, keepdims=True) + EPS)
            xn_ref[...] = (xf * g_ref[...]).astype(BF16)

        i = pl.program_id(1)
        pl.when(i < n_first)(functools.partial(prenorm, xa_ref))
        pl.when(i >= n_first)(functools.partial(prenorm, xb_ref))
        x = xn_ref[...]
    r = jnp.dot(x, w_ref[...].astype(BF16), preferred_element_type=F32)
    results = epilogue(r)

    def store(o_refs):
        for o_ref, o in zip(o_refs, results):
            if len(o_ref.shape) == 3:
                for h in range(o_ref.shape[0]):
                    o_ref[h] = o[:, h * D_V:(h + 1) * D_V].astype(o_ref.dtype)
            else:
                o_ref[...] = o.astype(o_ref.dtype)

    if n_split is None:
        store(refs)
    else:
        i = pl.program_id(1)
        pl.when(i < n_split)(functools.partial(store, refs[0::2]))
        pl.when(i >= n_split)(functools.partial(store, refs[1::2]))


def _proj(x, w, col_off, n, epilogue, out_dtypes, tm, tn, name, gain=None, split=None):
    jb = col_off // tn
    out_shape, out_specs = [], []
    if gain is None:
        m, d = x.shape
        n_first = None
        in_specs = [pl.BlockSpec((tm, d), lambda j, i: (i, 0))]
        args = (x, w)
    else:
        xa, xb = x
        d = xa.shape[1]
        m = xa.shape[0] + xb.shape[0]
        n_first = xa.shape[0] // tm
        assert n == tn and xa.shape[0] % tm == 0 and xb.shape[0] % tm == 0
        in_specs = [pl.BlockSpec((tm, d), lambda j, i: (jnp.minimum(i, n_first - 1), 0)),
                    pl.BlockSpec((tm, d), lambda j, i: (jnp.maximum(i - n_first, 0), 0)),
                    pl.BlockSpec((1, d), lambda j, i: (0, 0))]
        out_shape.append(jax.ShapeDtypeStruct((m, d), BF16))
        out_specs.append(pl.BlockSpec((tm, d), lambda j, i: (i, 0)))
        args = (xa, xb, gain.reshape(1, d), w)
    in_specs.append(pl.BlockSpec((d, tn), lambda j, i: (0, jb + j)))
    if split is None:
        n_split = None
        groups = ((m, lambda i: i),)
    else:
        assert split % tm == 0 and (m - split) % tm == 0
        n_split = split // tm
        groups = ((split, lambda i: jnp.minimum(i, n_split - 1)),
                  (m - split, lambda i: jnp.maximum(i - n_split, 0)))
    for dt in out_dtypes:
        for rows, row_block in groups:
            if isinstance(dt, tuple):
                out_shape.append(jax.ShapeDtypeStruct((n // D_V, rows, D_V), dt[0]))
                out_specs.append(pl.BlockSpec((tn // D_V, tm, D_V),
                                              lambda j, i, rb=row_block: (j, rb(i), 0)))
            else:
                out_shape.append(jax.ShapeDtypeStruct((rows, n), dt))
                out_specs.append(pl.BlockSpec((tm, tn), lambda j, i, rb=row_block: (rb(i), j)))
    return pl.pallas_call(
        functools.partial(_proj_kernel, epilogue=epilogue, n_first=n_first, n_split=n_split),
        out_shape=tuple(out_shape),
        grid=(n // tn, m // tm),
        in_specs=in_specs,
        out_specs=tuple(out_specs),
        compiler_params=_params(("parallel", "arbitrary" if split else "parallel")),
        name=name,
    )(*args)


def _conv_kernel(x_ref, wb_ref, wc_ref, wh_ref, wz_ref, st_ref, cw_ref,
                 t_ref, nc_ref, carry_ref):
    ti = pl.program_id(2)

    @pl.when(ti == 0)
    def _():
        carry_ref[:, 6:8, :] = st_ref[...]

    tb, tt, _ = t_ref.shape
    x = x_ref[...]

    def mm(w_ref):
        return jnp.dot(x, w_ref[...].astype(BF16), preferred_element_type=F32).reshape(tb, tt, -1)

    u = mm(wc_ref) * mm(wh_ref)
    c0 = carry_ref[:, 6:7, :]
    c1 = carry_ref[:, 7:8, :]
    row = lax.broadcasted_iota(jnp.int32, u.shape, 1)
    u1 = jnp.where(row == 0, c1, pltpu.roll(u, 1, 1))
    u2 = jnp.where(row == 0, c0, jnp.where(row == 1, c1, pltpu.roll(u, 2, 1)))
    y = cw_ref[0:1, :] * u2 + cw_ref[1:2, :] * u1 + cw_ref[2:3, :] * u
    carry_ref[...] = u[:, tt - 8:, :]
    nc_ref[...] = carry_ref[:, 6:8, :]
    t_ref[...] = (mm(wb_ref) * y * _silu(mm(wz_ref))).astype(t_ref.dtype)


def _conv_branch(xn, row0, b, t, w, state, conv_w, tb, tt, tc):
    d = xn.shape[1]
    assert row0 % (tb * tt) == 0 and (tb == 1 or tt == t)
    wspec = lambda off: pl.BlockSpec((d, tc), lambda c, bi, ti, o=off // tc: (0, o + c))
    first = row0 // (tb * tt)
    return pl.pallas_call(
        _conv_kernel,
        out_shape=(jax.ShapeDtypeStruct((b, t, CONV_W), BF16),
                   jax.ShapeDtypeStruct((b, CONV_K - 1, CONV_W), F32)),
        grid=(CONV_W // tc, b // tb, t // tt),
        in_specs=[pl.BlockSpec((tb * tt, d), lambda c, bi, ti: (first + bi * (t // tt) + ti, 0)),
                  wspec(OFF_BG), wspec(OFF_CG), wspec(OFF_H), wspec(OFF_ZC),
                  pl.BlockSpec((tb, CONV_K - 1, tc), lambda c, bi, ti: (bi, 0, c)),
                  pl.BlockSpec((CONV_K, tc), lambda c, bi, ti: (0, c))],
        out_specs=(pl.BlockSpec((tb, tt, tc), lambda c, bi, ti: (bi, ti, c)),
                   pl.BlockSpec((tb, CONV_K - 1, tc), lambda c, bi, ti: (bi, 0, c))),
        scratch_shapes=[pltpu.VMEM((tb, 8, tc), F32)],
        compiler_params=_params(("parallel", "parallel", "arbitrary")),
        name="conv_branch",
    )(xn, w, w, w, w, state, conv_w)


_NT = (((1,), (1,)), ((), ()))


def _head_out(o, hn, za, lam_init):
    o = o * lax.rsqrt(jnp.mean(o * o, axis=-1, keepdims=True) + EPS)
    return o * hn * (1.0 - lam_init) * _silu(za)


def _pattn_kernel(lam_ref, q_ref, k_ref, v_ref, bias_ref, za_ref, hn_ref, o_ref,
                  s_ref, p_ref, *, lam_init):
    gi = pl.program_id(2)
    tb = ATT_BLOCK
    hw = tb // 2

    def block_tasks(n, r):
        rows = slice(r * tb, (r + 1) * tb)
        q = q_ref[0, rows, :]
        lane = lax.broadcasted_iota(jnp.int32, q.shape, 1)
        zero = jnp.zeros_like(q)
        qmaps = (jnp.where(lane < D_QK, q, zero), jnp.where(lane >= D_QK, q, zero))
        chunks = [slice(c * hw, (c + 1) * hw) for c in range(2 * (n + 1))]
        sums = [[], []]
        maxes = [None, None]
        outs = [None, None]
        slab_of = lambda mi: (2 * r + mi) % ATT_SLABS

        def scores(mi):
            mpart = None
            for j in range(n + 1):
                s = lax.dot_general(qmaps[mi], k_ref[0, j * tb:(j + 1) * tb, :], _NT,
                                    preferred_element_type=F32)
                if j == n:
                    s = s + bias_ref[0, 1]
                elif j == n - 1:
                    s = s + bias_ref[0, 0]
                s_ref[slab_of(mi), :, j * tb:(j + 1) * tb] = s
                part = jnp.maximum(s[:, :hw], s[:, hw:])
                mpart = part if mpart is None else jnp.maximum(mpart, part)
                if j == n:
                    maxes[mi] = jnp.max(mpart, axis=-1, keepdims=True)
                yield

        def numerator(mi):
            slab = slab_of(mi)
            for st in range(tb // ATT_STRIP):
                srows = slice(st * ATT_STRIP, (st + 1) * ATT_STRIP)
                m = jnp.broadcast_to(maxes[mi][srows], (ATT_STRIP, hw))
                yield
                lpart = None
                for c in chunks:
                    e = jnp.exp2(s_ref[slab, srows, c] - m)
                    p_ref[slab, srows, c] = e.astype(BF16)
                    lpart = e if lpart is None else lpart + e
                    yield
                sums[mi].append(jnp.sum(lpart, axis=-1, keepdims=True))

        def values(mi):
            o = jnp.dot(p_ref[slab_of(mi), :, :(n + 1) * tb], v_ref[0, :(n + 1) * tb, :],
                        preferred_element_type=F32)
            outs[mi] = o / jnp.concatenate(sums[mi], axis=0)
            if mi == 1:
                o = outs[0] - lam_ref[0, 0] * outs[1]
                o_ref[0, rows, :] = _head_out(o, hn_ref[...], za_ref[0, rows, :].astype(F32),
                                              lam_init).astype(o_ref.dtype)
            yield

        return [(scores(mi), n + 1, numerator(mi), (tb // ATT_STRIP) * (1 + len(chunks)), values(mi))
                for mi in range(2)]

    def emit_group(g):
        units = [u for r in range(ATT_GROUP) for u in block_tasks(g * ATT_GROUP + r, r)]
        for t in range(len(units) + 2):
            if t >= 2:
                next(units[t - 2][4])
            a, na = (units[t][0], units[t][1]) if t < len(units) else (iter(()), 0)
            b, nb = (units[t - 1][2], units[t - 1][3]) if 1 <= t <= len(units) else (iter(()), 0)
            da = db = 0
            while da < na or db < nb:
                if db >= nb or (da < na and da * nb <= db * na):
                    next(a)
                    da += 1
                else:
                    next(b)
                    db += 1
            for rest in (a, b):
                for _ in rest:
                    pass

    for g in range(k_ref.shape[1] // (tb * ATT_GROUP)):
        pl.when(gi == g)(functools.partial(emit_group, g))


def _prompt_attention(lam, q, kb, vb, pbias, za, head_norm, lam_init, b, t):
    m = b * t
    tb = ATT_BLOCK
    tg = tb * ATT_GROUP
    assert tb >= NEAR and t % tg == 0
    blk = pl.BlockSpec((1, tg, D_V), lambda bi, h, gi: (h, bi * (t // tg) + gi, 0))
    full = pl.BlockSpec((1, t, D_V), lambda bi, h, gi: (h, bi, 0))
    return pl.pallas_call(
        functools.partial(_pattn_kernel, lam_init=lam_init),
        out_shape=jax.ShapeDtypeStruct((N_HEADS, m, D_V), BF16),
        grid=(b, N_HEADS, t // tg),
        in_specs=[pl.BlockSpec(memory_space=pltpu.SMEM),
                  blk, full, full,
                  pl.BlockSpec((1, 2, tb, tb), lambda bi, h, gi: (h, 0, 0, 0)),
                  blk,
                  pl.BlockSpec((1, D_V), lambda bi, h, gi: (0, 0))],
        out_specs=blk,
        scratch_shapes=[pltpu.VMEM((ATT_SLABS, tb, t + V7X_LANES), F32),
                        pltpu.VMEM((ATT_SLABS, tb, t + V7X_LANES), BF16)],
        compiler_params=_params(("parallel", "parallel", "arbitrary")),
        name="prompt_attention",
    )(lam, q, kb, vb, pbias, za, head_norm)


def _sattn_kernel(lam_ref, q_ref, kc_ref, vc_ref, kn_ref, vn_ref, bc_ref, bn_ref,
                  za_ref, hn_ref, o_ref, s_ref, vt_ref, *, lam_init):
    nq = q_ref.shape[1]
    past = kc_ref.shape[1] // N_HEADS
    far_n = past - bc_ref.shape[2]
    lam = lam_ref[0, 0]
    ck = SAMPLE_KEYS
    nchunk = past // ck

    def q2m_of(h):
        q = q_ref[h]
        lane = lax.broadcasted_iota(jnp.int32, q.shape, 1)
        zero = jnp.zeros_like(q)
        return jnp.concatenate([jnp.where(lane < D_QK, q, zero), jnp.where(lane >= D_QK, q, zero)], axis=0)

    q2m = [q2m_of(h) for h in range(N_HEADS)]

    def heads_major(ref, c):
        rows = ref[0, c * ck * N_HEADS:(c + 1) * ck * N_HEADS, :]
        return jnp.swapaxes(rows.reshape(ck, N_HEADS, D_V), 0, 1).astype(BF16)

    def scores(c):
        kt = heads_major(kc_ref, c)
        yield
        for h in range(N_HEADS):
            s_ref[h, :, c * ck:(c + 1) * ck] = lax.dot_general(q2m[h], kt[h], _NT,
                                                               preferred_element_type=F32)
            yield

    def values(c):
        vt = heads_major(vc_ref, c)
        yield
        for h in range(N_HEADS):
            vt_ref[h, c * ck:(c + 1) * ck, :] = vt[h]
        yield

    def head(h):
        s = s_ref[h, :, :past]
        s = jnp.concatenate([s[:, :far_n], s[:, far_n:] + bc_ref[h]], axis=1)
        sn = lax.dot_general(q2m[h], kn_ref[h], _NT, preferred_element_type=F32) + bn_ref[h]
        mx = jnp.maximum(jnp.max(s, axis=-1, keepdims=True), jnp.max(sn, axis=-1, keepdims=True))
        e = jnp.exp2(s - mx)
        en = jnp.exp2(sn - mx)
        denom = jnp.sum(e, axis=-1, keepdims=True) + jnp.sum(en, axis=-1, keepdims=True)
        row = lax.broadcasted_iota(jnp.int32, denom.shape, 0)
        w = jnp.where(row < nq, 1.0, -lam) / denom

        def combine(x):
            p = x * w
            return (p[:nq] + p[nq:]).astype(BF16)

        a, an = combine(e), combine(en)
        yield
        o = (jnp.dot(a, vt_ref[h], preferred_element_type=F32)
             + jnp.dot(an, vn_ref[h], preferred_element_type=F32))
        o_ref[h] = _head_out(o, hn_ref[...], za_ref[h].astype(F32), lam_init).astype(o_ref.dtype)
        yield

    def together(*gens):
        gens = list(gens)
        while gens:
            gens = [g for g in gens if next(g, StopIteration) is not StopIteration]

    def first_step(g):
        next(g, None)
        yield

    per_chunk = [scores(c) for c in range(nchunk)]
    next(per_chunk[0])
    for c in range(nchunk):
        ahead = [first_step(per_chunk[c + 1])] if c + 1 < nchunk else []
        together(per_chunk[c], values(c), *ahead)
    for h in range(0, N_HEADS, 2):
        together(head(h), head(h + 1))


def _sample_attention(lam, q, cache_k, cache_v, kb, vb, sbc, sbn, za, head_norm, lam_init, layer,
                      row0, b):
    nq = sbn.shape[2]
    rows = cache_k.shape[1]
    stacked = pl.BlockSpec((N_HEADS, nq, D_V), lambda bi: (0, row0 // nq + bi, 0))
    own = pl.BlockSpec((N_HEADS, nq, D_V), lambda bi: (0, bi, 0))
    cache = pl.BlockSpec((1, rows, D_V), lambda bi: (layer * b + bi, 0, 0))
    whole = lambda a: pl.BlockSpec(a.shape, lambda bi: (0,) * a.ndim)
    return pl.pallas_call(
        functools.partial(_sattn_kernel, lam_init=lam_init),
        out_shape=jax.ShapeDtypeStruct((N_HEADS, b * nq, D_V), BF16),
        grid=(b,),
        in_specs=[pl.BlockSpec(memory_space=pltpu.SMEM),
                  stacked, cache, cache, own, own, whole(sbc), whole(sbn), stacked, whole(head_norm)],
        out_specs=own,
        scratch_shapes=[pltpu.VMEM((N_HEADS, 2 * nq, rows // N_HEADS + V7X_LANES), F32),
                        pltpu.VMEM((N_HEADS, rows // N_HEADS, D_V), BF16)],
        compiler_params=_params(("parallel",)),
        name="sample_attention",
    )(lam, q, cache_k, cache_v, kb, vb, sbc, sbn, za, head_norm)


def _out_kernel(og_ref, t_ref, ga_ref, gc_ref, wpa_ref, wpc_ref, wo_ref, x_ref, g_ref,
                o_ref, m_ref):
    c = pl.program_id(1)
    nchunk, _, tc = m_ref.shape
    og = jnp.concatenate([og_ref[h] for h in range(og_ref.shape[0])], axis=1)
    ya = jnp.dot(og, wpa_ref[c], preferred_element_type=F32)
    yc = jnp.dot(t_ref[...], wpc_ref[c], preferred_element_type=F32)
    merged = (jax.nn.sigmoid(ga_ref[...].astype(F32)) * ya
              + jax.nn.sigmoid(gc_ref[...].astype(F32)) * yc)
    m_ref[c] = merged.astype(BF16)

    @pl.when(c == nchunk - 1)
    def _():
        y = jnp.dot(m_ref[0], wo_ref[0], preferred_element_type=F32)
        for cc in range(1, nchunk):
            y = y + jnp.dot(m_ref[cc], wo_ref[cc], preferred_element_type=F32)
        y = y * lax.rsqrt(jnp.mean(y * y, axis=-1, keepdims=True) + EPS)
        o_ref[...] = x_ref[...] + y * g_ref[...]


def _cast_chunks_kernel(w_ref, o_ref):
    o_ref[0] = w_ref[...].astype(o_ref.dtype)


def _cast_chunks(w, tc, rows=False):
    k, n = w.shape
    blk, imap = ((tc, n), lambda j: (j, 0)) if rows else ((k, tc), lambda j: (0, j))
    nchunk = (k if rows else n) // tc
    return pl.pallas_call(
        _cast_chunks_kernel,
        out_shape=jax.ShapeDtypeStruct((nchunk,) + blk, BF16),
        grid=(nchunk,),
        in_specs=[pl.BlockSpec(blk, imap)],
        out_specs=pl.BlockSpec((1,) + blk, lambda j: (j, 0, 0)),
        compiler_params=_params(("parallel",)),
        name="cast_chunks",
    )(w)


def _merge_out(og, t, gates, row0, wpa, wpc, wo, x, g, tm, tc):
    m, d = x.shape
    assert row0 % tm == 0
    first = row0 // tm
    resident = lambda a: pl.BlockSpec(a.shape, lambda i, c: (0, 0, 0), pipeline_mode=pl.Buffered(1))
    return pl.pallas_call(
        _out_kernel,
        out_shape=jax.ShapeDtypeStruct((m, d), x.dtype),
        grid=(m // tm, d // tc),
        in_specs=[pl.BlockSpec((N_HEADS, tm, D_V), lambda i, c: (0, i, 0)),
                  pl.BlockSpec((tm, CONV_W), lambda i, c: (i, 0)),
                  pl.BlockSpec((tm, tc), lambda i, c: (first + i, c)),
                  pl.BlockSpec((tm, tc), lambda i, c: (first + i, d // tc + c)),
                  resident(wpa), resident(wpc), resident(wo),
                  pl.BlockSpec((tm, d), lambda i, c: (i, 0)),
                  pl.BlockSpec((1, d), lambda i, c: (0, 0))],
        out_specs=pl.BlockSpec((tm, d), lambda i, c: (i, 0)),
        scratch_shapes=[pltpu.VMEM((d // tc, tm, tc), BF16)],
        compiler_params=_params(("parallel", "arbitrary")),
        name="merge_out",
    )(og, t, gates, gates, wpa, wpc, wo, x, g)


def _proj_specs():
    scale = LOG2E / math.sqrt(D_QK)
    heads = (BF16, "heads")
    return (("q", OFF_Q, ATT_W, PROJ_COLS, lambda r: (r * scale,), (heads,)),
            ("k", OFF_K, ATT_W, PROJ_COLS, lambda r: (r, r), (F32, heads)),
            ("v", OFF_V, ATT_W, PROJ_COLS, lambda r: (r, r), (F32, heads)),
            ("za", OFF_ZA, ATT_W, PROJ_COLS, lambda r: (r,), (heads,)),
            ("gates", OFF_GA, 2 * D_MODEL, PROJ_COLS, lambda r: (r,), (BF16,)))


def _layer(xp, xs, conv_p, conv_s, weights, attend_p, attend_s):
    (norm_pre, norm_post, w_in, conv_w, wpa, wpc, wo) = weights
    (bp, tp, d), (bs, ts, _) = xp.shape, xs.shape
    mp, ms = bp * tp, bs * ts
    xp2, xs2 = xp.reshape(mp, d), xs.reshape(ms, d)
    tm = (mp + ms) // 8
    assert tm * 8 == mp + ms and tm % 16 == 0
    res, xn = {}, None
    for name, off, n, tn, epilogue, dtypes in _proj_specs():
        if xn is None:
            xn, *res[name] = _proj((xp2, xs2), w_in, off, n, epilogue, dtypes, PROJ_ROWS, tn,
                                   "proj_" + name, gain=norm_pre)
        elif F32 in dtypes:
            k_p, k_s, kb_p, kb_s = _proj(xn, w_in, off, n, epilogue, dtypes, PROJ_ROWS, tn,
                                         "proj_" + name, split=mp)
            res[name] = [(k_p, kb_p), (k_s, kb_s)]
        else:
            res[name] = _proj(xn, w_in, off, n, epilogue, dtypes, tm, tn, "proj_" + name)
    (q,), (za,), (gates,) = (res[n] for n in ("q", "za", "gates"))
    out = []
    for g, (row0, x2, b, t, conv_state, attend, conv_tiles) in enumerate((
            (0, xp2, bp, tp, conv_p, attend_p, (1, 1024, 512)),
            (mp, xs2, bs, ts, conv_s, attend_s, (bs, ts, 512)))):
        m = b * t
        (k, kb), (v, vb) = res["k"][g], res["v"][g]
        tcv, new_conv = _conv_branch(xn, row0, b, t, w_in, conv_state, conv_w, *conv_tiles)
        og = attend(q, kb, vb, za)
        y = _merge_out(og, tcv.reshape(m, CONV_W), gates, row0, wpa, wpc, wo, x2,
                       norm_post.reshape(1, d), PROJ_ROWS, MERGE_CHUNK)
        out.append((y.reshape(b, t, d), k.reshape(b, t, N_HEADS, 2 * D_QK),
                    v.reshape(b, t, N_HEADS, D_V), new_conv))
    return out


def kernel(x_prompt, x_sample, cache_k, cache_v, state_conv, norm_pre, norm_post, w_in,
           lambda_q1, lambda_k1, lambda_q2, lambda_k2, head_norm, conv_w, w_proj_attn,
           w_proj_conv, w_out, rel_bias):
    depth = w_in.shape[0]
    past = cache_k.shape[2]
    dec_b, dec_t = x_sample.shape[0], x_sample.shape[1]
    xp, xs = x_prompt, x_sample
    zero_conv = jnp.zeros((xp.shape[0], CONV_K - 1, CONV_W), xp.dtype)
    pbias = _prompt_bias(rel_bias)
    outs = [[] for _ in range(6)]
    for l in range(depth):
        lam_init = 0.8 - 0.6 * math.exp(-0.3 * l)
        sbc, sbn, lam = _sample_bias_and_lambda(
            rel_bias, lambda_q1[l][None], lambda_k1[l][None], lambda_q2[l][None], lambda_k2[l][None],
            past, dec_t, lam_init)
        hn = head_norm[l].reshape(1, D_V)
        weights = (norm_pre[l], norm_post[l], w_in[l], conv_w[l],
                   _cast_chunks(w_proj_attn[l], MERGE_CHUNK), _cast_chunks(w_proj_conv[l], MERGE_CHUNK),
                   _cast_chunks(w_out[l], MERGE_CHUNK, rows=True))

        pb, pt = x_prompt.shape[0], x_prompt.shape[1]

        def attend_prompt(q, kb, vb, za):
            return _prompt_attention(lam, q, kb, vb, pbias, za, hn, lam_init, pb, pt)

        ck = cache_k.reshape(depth * dec_b, past * N_HEADS, D_V)
        cv = cache_v.reshape(depth * dec_b, past * N_HEADS, D_V)

        def attend_sample(q, kb, vb, za, l=l):
            return _sample_attention(lam, q, ck, cv, kb, vb, sbc, sbn, za, hn, lam_init, l,
                                     pb * pt, dec_b)

        (xp, kp, vp, cp), (xs, ksm, vsm, csm) = _layer(xp, xs, zero_conv, state_conv[l], weights,
                                                       attend_prompt, attend_sample)
        for lst, val in zip(outs, (kp, vp, cp, ksm, vsm, csm)):
            lst.append(val)
    return (xp, xs) + tuple(jnp.stack(o, axis=0) for o in outs)
```

```python
import functools
import math

import jax
import jax.numpy as jnp
from jax import lax
from jax.experimental import pallas as pl
from jax.experimental.pallas import tpu as pltpu

D_MODEL = 2048
N_HEADS = 8
D_QK = 64
D_V = 2 * D_QK
ATT_W = N_HEADS * D_V
CONV_W = D_MODEL // 2
CONV_K = 3
CHUNK = 64
NUM_BUCKETS = 32
MAX_DISTANCE = 128
EPS = 1e-6
MASKED = -1e30
LOG2E = math.log2(math.e)

OFF_Q, OFF_K, OFF_V, OFF_ZA = (i * ATT_W for i in range(4))
OFF_BG, OFF_CG, OFF_H, OFF_ZC = (4 * ATT_W + i * CONV_W for i in range(4))
OFF_GA = 4 * ATT_W + 4 * CONV_W

ATT_BLOCK = 256
ATT_GROUP = 4
ATT_SLABS = 4
ATT_STRIP = 64
PROJ_COLS = 1024
PROJ_ROWS = 512
MERGE_CHUNK = 1024
SAMPLE_KEYS = 256
V7X_VMEM_LIMIT = 56 * 1024 * 1024
V7X_LANES = 128

F32 = jnp.float32
BF16 = jnp.bfloat16


def _params(sem, vmem=V7X_VMEM_LIMIT):
    return pltpu.CompilerParams(dimension_semantics=sem, vmem_limit_bytes=vmem)


def _bucket_py(rel):
    half = NUM_BUCKETS // 2
    max_exact = half // 2
    steps = half - max_exact
    n = abs(rel)
    if n < max_exact:
        v = n
    else:
        v = max_exact + sum(n ** steps * max_exact ** j >= MAX_DISTANCE ** j * max_exact ** steps
                            for j in range(1, steps))
    return v + (half if rel > 0 else 0)


NEAR = MAX_DISTANCE
FAR_BUCKET = _bucket_py(-NEAR)
_BREAKS = tuple((rel, _bucket_py(rel)) for rel in range(-NEAR + 1, NEAR + 1)
                if _bucket_py(rel) != _bucket_py(rel - 1))


def _bias_minus_far(rel, value_of, max_rel):
    val = value_of(FAR_BUCKET)
    for first_rel, bucket in _BREAKS:
        if first_rel > max_rel:
            break
        val = jnp.where(rel >= first_rel, value_of(bucket), val)
    return (val - value_of(FAR_BUCKET)) * LOG2E


_BIAS_ROWS = 32


def _pbias_kernel(rb_ref, pb_ref):
    h = pl.program_id(0)
    tb = ATT_BLOCK
    value_of = lambda b: rb_ref[b, h]
    for t in range(2):
        def rows(i, carry, t=t):
            r0 = pl.multiple_of(i * _BIAS_ROWS, _BIAS_ROWS)
            r = r0 + lax.broadcasted_iota(jnp.int32, (_BIAS_ROWS, tb), 0)
            c = lax.broadcasted_iota(jnp.int32, (_BIAS_ROWS, tb), 1)
            val = _bias_minus_far(c - r - (1 - t) * tb, value_of, max_rel=tb - 1 if t == 1 else -1)
            if t == 1:
                val = jnp.where(c // CHUNK <= r // CHUNK, val, MASKED)
            pb_ref[0, t, pl.ds(r0, _BIAS_ROWS), :] = val
            return carry

        lax.fori_loop(0, tb // _BIAS_ROWS, rows, 0)


def _prompt_bias(rel_bias):
    tb = ATT_BLOCK
    return pl.pallas_call(
        _pbias_kernel,
        out_shape=jax.ShapeDtypeStruct((N_HEADS, 2, tb, tb), F32),
        grid=(N_HEADS,),
        in_specs=[pl.BlockSpec(memory_space=pltpu.SMEM)],
        out_specs=pl.BlockSpec((1, 2, tb, tb), lambda h: (h, 0, 0, 0)),
        compiler_params=_params(("parallel",)),
        name="prompt_bias",
    )(rel_bias)


def _sbias_kernel(rb_ref, lq1_ref, lk1_ref, lq2_ref, lk2_ref, sbc_ref, sbn_ref, lam_ref,
                  *, past, lam_init):
    nq = sbn_ref.shape[2]

    def tile(shape, k0, value_of):
        qpos = past + lax.broadcasted_iota(jnp.int32, shape, 0) % nq
        kpos = k0 + lax.broadcasted_iota(jnp.int32, shape, 1)
        val = _bias_minus_far(kpos - qpos, value_of, max_rel=k0 + shape[1] - 1 - past)
        return jnp.where(kpos // CHUNK <= qpos // CHUNK, val, MASKED)

    for h in range(N_HEADS):
        value_of = lambda b, h=h: rb_ref[b, h]
        sbc_ref[h] = tile(sbc_ref.shape[1:], past - sbc_ref.shape[2], value_of)
        sbn_ref[h] = tile(sbn_ref.shape[1:], past, value_of)

    s1 = jnp.sum(lq1_ref[...].astype(F32) * lk1_ref[...].astype(F32), axis=-1, keepdims=True)
    s2 = jnp.sum(lq2_ref[...].astype(F32) * lk2_ref[...].astype(F32), axis=-1, keepdims=True)
    lam_ref[...] = jnp.exp(s1) - jnp.exp(s2) + lam_init


def _sample_bias_and_lambda(rel_bias, lq1, lk1, lq2, lk2, past, nq, lam_init):
    vmem = pl.BlockSpec(memory_space=pltpu.VMEM)
    return pl.pallas_call(
        functools.partial(_sbias_kernel, past=past, lam_init=lam_init),
        out_shape=(jax.ShapeDtypeStruct((N_HEADS, 2 * nq, NEAR), F32),
                   jax.ShapeDtypeStruct((N_HEADS, 2 * nq, nq), F32),
                   jax.ShapeDtypeStruct((1, 1), F32)),
        in_specs=[pl.BlockSpec(memory_space=pltpu.SMEM)] + [vmem] * 4,
        out_specs=(vmem, vmem, vmem),
        name="sample_bias_lambda",
    )(rel_bias, lq1, lk1, lq2, lk2)


def _silu(x):
    return x * jax.nn.sigmoid(x)


def _proj_kernel(*refs, epilogue, n_first, n_split):
    refs = list(refs)
    if n_first is None:
        x = refs.pop(0)[...]
        w_ref = refs.pop(0)
    else:
        xa_ref, xb_ref, g_ref, w_ref, xn_ref = (refs.pop(0) for _ in range(5))

        def prenorm(x_ref):
            xf = x_ref[...].astype(F32)
            xf = xf * lax.rsqrt(jnp.mean(xf * xf, axis=-1, keepdims=True) + EPS)
            xn_ref[...] = (xf * g_ref[...]).astype(BF16)

        i = pl.program_id(1)
        pl.when(i < n_first)(functools.partial(prenorm, xa_ref))
        pl.when(i >= n_first)(functools.partial(prenorm, xb_ref))
        x = xn_ref[...]
    r = jnp.dot(x, w_ref[...].astype(BF16), preferred_element_type=F32)
    results = epilogue(r)

    def store(o_refs):
        for o_ref, o in zip(o_refs, results):
            if len(o_ref.shape) == 3:
                for h in range(o_ref.shape[0]):
                    o_ref[h] = o[:, h * D_V:(h + 1) * D_V].astype(o_ref.dtype)
            else:
                o_ref[...] = o.astype(o_ref.dtype)

    if n_split is None:
        store(refs)
    else:
        i = pl.program_id(1)
        pl.when(i < n_split)(functools.partial(store, refs[0::2]))
        pl.when(i >= n_split)(functools.partial(store, refs[1::2]))


def _proj(x, w, col_off, n, epilogue, out_dtypes, tm, tn, name, gain=None, split=None):
    jb = col_off // tn
    out_shape, out_specs = [], []
    if gain is None:
        m, d = x.shape
        n_first = None
        in_specs = [pl.BlockSpec((tm, d), lambda j, i: (i, 0))]
        args = (x, w)
    else:
        xa, xb = x
        d = xa.shape[1]
        m = xa.shape[0] + xb.shape[0]
        n_first = xa.shape[0] // tm
        assert n == tn and xa.shape[0] % tm == 0 and xb.shape[0] % tm == 0
        in_specs = [pl.BlockSpec((tm, d), lambda j, i: (jnp.minimum(i, n_first - 1), 0)),
                    pl.BlockSpec((tm, d), lambda j, i: (jnp.maximum(i - n_first, 0), 0)),
                    pl.BlockSpec((1, d), lambda j, i: (0, 0))]
        out_shape.append(jax.ShapeDtypeStruct((m, d), BF16))
        out_specs.append(pl.BlockSpec((tm, d), lambda j, i: (i, 0)))
        args = (xa, xb, gain.reshape(1, d), w)
    in_specs.append(pl.BlockSpec((d, tn), lambda j, i: (0, jb + j)))
    if split is None:
        n_split = None
        groups = ((m, lambda i: i),)
    else:
        assert split % tm == 0 and (m - split) % tm == 0
        n_split = split // tm
        groups = ((split, lambda i: jnp.minimum(i, n_split - 1)),
                  (m - split, lambda i: jnp.maximum(i - n_split, 0)))
    for dt in out_dtypes:
        for rows, row_block in groups:
            if isinstance(dt, tuple):
                out_shape.append(jax.ShapeDtypeStruct((n // D_V, rows, D_V), dt[0]))
                out_specs.append(pl.BlockSpec((tn // D_V, tm, D_V),
                                              lambda j, i, rb=row_block: (j, rb(i), 0)))
            else:
                out_shape.append(jax.ShapeDtypeStruct((rows, n), dt))
                out_specs.append(pl.BlockSpec((tm, tn), lambda j, i, rb=row_block: (rb(i), j)))
    return pl.pallas_call(
        functools.partial(_proj_kernel, epilogue=epilogue, n_first=n_first, n_split=n_split),
        out_shape=tuple(out_shape),
        grid=(n // tn, m // tm),
        in_specs=in_specs,
        out_specs=tuple(out_specs),
        compiler_params=_params(("parallel", "arbitrary" if split else "parallel")),
        name=name,
    )(*args)


def _conv_kernel(x_ref, wb_ref, wc_ref, wh_ref, wz_ref, st_ref, cw_ref,
                 t_ref, nc_ref, carry_ref):
    ti = pl.program_id(2)

    @pl.when(ti == 0)
    def _():
        carry_ref[:, 6:8, :] = st_ref[...]

    tb, tt, _ = t_ref.shape
    x = x_ref[...]

    def mm(w_ref):
        return jnp.dot(x, w_ref[...].astype(BF16), preferred_element_type=F32).reshape(tb, tt, -1)

    u = mm(wc_ref) * mm(wh_ref)
    c0 = carry_ref[:, 6:7, :]
    c1 = carry_ref[:, 7:8, :]
    row = lax.broadcasted_iota(jnp.int32, u.shape, 1)
    u1 = jnp.where(row == 0, c1, pltpu.roll(u, 1, 1))
    u2 = jnp.where(row == 0, c0, jnp.where(row == 1, c1, pltpu.roll(u, 2, 1)))
    y = cw_ref[0:1, :] * u2 + cw_ref[1:2, :] * u1 + cw_ref[2:3, :] * u
    carry_ref[...] = u[:, tt - 8:, :]
    nc_ref[...] = carry_ref[:, 6:8, :]
    t_ref[...] = (mm(wb_ref) * y * _silu(mm(wz_ref))).astype(t_ref.dtype)


def _conv_branch(xn, row0, b, t, w, state, conv_w, tb, tt, tc):
    d = xn.shape[1]
    assert row0 % (tb * tt) == 0 and (tb == 1 or tt == t)
    wspec = lambda off: pl.BlockSpec((d, tc), lambda c, bi, ti, o=off // tc: (0, o + c))
    first = row0 // (tb * tt)
    return pl.pallas_call(
        _conv_kernel,
        out_shape=(jax.ShapeDtypeStruct((b, t, CONV_W), BF16),
                   jax.ShapeDtypeStruct((b, CONV_K - 1, CONV_W), F32)),
        grid=(CONV_W // tc, b // tb, t // tt),
        in_specs=[pl.BlockSpec((tb * tt, d), lambda c, bi, ti: (first + bi * (t // tt) + ti, 0)),
                  wspec(OFF_BG), wspec(OFF_CG), wspec(OFF_H), wspec(OFF_ZC),
                  pl.BlockSpec((tb, CONV_K - 1, tc), lambda c, bi, ti: (bi, 0, c)),
                  pl.BlockSpec((CONV_K, tc), lambda c, bi, ti: (0, c))],
        out_specs=(pl.BlockSpec((tb, tt, tc), lambda c, bi, ti: (bi, ti, c)),
                   pl.BlockSpec((tb, CONV_K - 1, tc), lambda c, bi, ti: (bi, 0, c))),
        scratch_shapes=[pltpu.VMEM((tb, 8, tc), F32)],
        compiler_params=_params(("parallel", "parallel", "arbitrary")),
        name="conv_branch",
    )(xn, w, w, w, w, state, conv_w)


_NT = (((1,), (1,)), ((), ()))


def _head_out(o, hn, za, lam_init):
    o = o * lax.rsqrt(jnp.mean(o * o, axis=-1, keepdims=True) + EPS)
    return o * hn * (1.0 - lam_init) * _silu(za)


def _pattn_kernel(lam_ref, q_ref, k_ref, v_ref, bias_ref, za_ref, hn_ref, o_ref,
                  s_ref, p_ref, *, lam_init):
    gi = pl.program_id(2)
    tb = ATT_BLOCK
    hw = tb // 2

    def block_tasks(n, r):
        rows = slice(r * tb, (r + 1) * tb)
        q = q_ref[0, rows, :]
        lane = lax.broadcasted_iota(jnp.int32, q.shape, 1)
        zero = jnp.zeros_like(q)
        qmaps = (jnp.where(lane < D_QK, q, zero), jnp.where(lane >= D_QK, q, zero))
        chunks = [slice(c * hw, (c + 1) * hw) for c in range(2 * (n + 1))]
        sums = [[], []]
        maxes = [None, None]
        outs = [None, None]
        slab_of = lambda mi: (2 * r + mi) % ATT_SLABS

        def scores(mi):
            mpart = None
            for j in range(n + 1):
                s = lax.dot_general(qmaps[mi], k_ref[0, j * tb:(j + 1) * tb, :], _NT,
                                    preferred_element_type=F32)
                if j == n:
                    s = s + bias_ref[0, 1]
                elif j == n - 1:
                    s = s + bias_ref[0, 0]
                s_ref[slab_of(mi), :, j * tb:(j + 1) * tb] = s
                part = jnp.maximum(s[:, :hw], s[:, hw:])
                mpart = part if mpart is None else jnp.maximum(mpart, part)
                if j == n:
                    maxes[mi] = jnp.max(mpart, axis=-1, keepdims=True)
                yield

        def numerator(mi):
            slab = slab_of(mi)
            for st in range(tb // ATT_STRIP):
                srows = slice(st * ATT_STRIP, (st + 1) * ATT_STRIP)
                m = jnp.broadcast_to(maxes[mi][srows], (ATT_STRIP, hw))
                yield
                lpart = None
                for c in chunks:
                    e = jnp.exp2(s_ref[slab, srows, c] - m)
                    p_ref[slab, srows, c] = e.astype(BF16)
                    lpart = e if lpart is None else lpart + e
                    yield
                sums[mi].append(jnp.sum(lpart, axis=-1, keepdims=True))

        def values(mi):
            o = jnp.dot(p_ref[slab_of(mi), :, :(n + 1) * tb], v_ref[0, :(n + 1) * tb, :],
                        preferred_element_type=F32)
            outs[mi] = o / jnp.concatenate(sums[mi], axis=0)
            if mi == 1:
                o = outs[0] - lam_ref[0, 0] * outs[1]
                o_ref[0, rows, :] = _head_out(o, hn_ref[...], za_ref[0, rows, :].astype(F32),
                                              lam_init).astype(o_ref.dtype)
            yield

        return [(scores(mi), n + 1, numerator(mi), (tb // ATT_STRIP) * (1 + len(chunks)), values(mi))
                for mi in range(2)]

    def emit_group(g):
        units = [u for r in range(ATT_GROUP) for u in block_tasks(g * ATT_GROUP + r, r)]
        for t in range(len(units) + 2):
            if t >= 2:
                next(units[t - 2][4])
            a, na = (units[t][0], units[t][1]) if t < len(units) else (iter(()), 0)
            b, nb = (units[t - 1][2], units[t - 1][3]) if 1 <= t <= len(units) else (iter(()), 0)
            da = db = 0
            while da < na or db < nb:
                if db >= nb or (da < na and da * nb <= db * na):
                    next(a)
                    da += 1
                else:
                    next(b)
                    db += 1
            for rest in (a, b):
                for _ in rest:
                    pass

    for g in range(k_ref.shape[1] // (tb * ATT_GROUP)):
        pl.when(gi == g)(functools.partial(emit_group, g))


def _prompt_attention(lam, q, kb, vb, pbias, za, head_norm, lam_init, b, t):
    m = b * t
    tb = ATT_BLOCK
    tg = tb * ATT_GROUP
    assert tb >= NEAR and t % tg == 0
    blk = pl.BlockSpec((1, tg, D_V), lambda bi, h, gi: (h, bi * (t // tg) + gi, 0))
    full = pl.BlockSpec((1, t, D_V), lambda bi, h, gi: (h, bi, 0))
    return pl.pallas_call(
        functools.partial(_pattn_kernel, lam_init=lam_init),
        out_shape=jax.ShapeDtypeStruct((N_HEADS, m, D_V), BF16),
        grid=(b, N_HEADS, t // tg),
        in_specs=[pl.BlockSpec(memory_space=pltpu.SMEM),
                  blk, full, full,
                  pl.BlockSpec((1, 2, tb, tb), lambda bi, h, gi: (h, 0, 0, 0)),
                  blk,
                  pl.BlockSpec((1, D_V), lambda bi, h, gi: (0, 0))],
        out_specs=blk,
        scratch_shapes=[pltpu.VMEM((ATT_SLABS, tb, t + V7X_LANES), F32),
                        pltpu.VMEM((ATT_SLABS, tb, t + V7X_LANES), BF16)],
        compiler_params=_params(("parallel", "parallel", "arbitrary")),
        name="prompt_attention",
    )(lam, q, kb, vb, pbias, za, head_norm)


def _sattn_kernel(lam_ref, q_ref, kc_ref, vc_ref, kn_ref, vn_ref, bc_ref, bn_ref,
                  za_ref, hn_ref, o_ref, s_ref, vt_ref, *, lam_init):
    nq = q_ref.shape[1]
    past = kc_ref.shape[1] // N_HEADS
    far_n = past - bc_ref.shape[2]
    lam = lam_ref[0, 0]
    ck = SAMPLE_KEYS
    nchunk = past // ck

    def q2m_of(h):
        q = q_ref[h]
        lane = lax.broadcasted_iota(jnp.int32, q.shape, 1)
        zero = jnp.zeros_like(q)
        return jnp.concatenate([jnp.where(lane < D_QK, q, zero), jnp.where(lane >= D_QK, q, zero)], axis=0)

    q2m = [q2m_of(h) for h in range(N_HEADS)]

    def heads_major(ref, c):
        rows = ref[0, c * ck * N_HEADS:(c + 1) * ck * N_HEADS, :]
        return jnp.swapaxes(rows.reshape(ck, N_HEADS, D_V), 0, 1).astype(BF16)

    def scores(c):
        kt = heads_major(kc_ref, c)
        yield
        for h in range(N_HEADS):
            s_ref[h, :, c * ck:(c + 1) * ck] = lax.dot_general(q2m[h], kt[h], _NT,
                                                               preferred_element_type=F32)
            yield

    def values(c):
        vt = heads_major(vc_ref, c)
        yield
        for h in range(N_HEADS):
            vt_ref[h, c * ck:(c + 1) * ck, :] = vt[h]
        yield

    def head(h):
        s = s_ref[h, :, :past]
        s = jnp.concatenate([s[:, :far_n], s[:, far_n:] + bc_ref[h]], axis=1)
        sn = lax.dot_general(q2m[h], kn_ref[h], _NT, preferred_element_type=F32) + bn_ref[h]
        mx = jnp.maximum(jnp.max(s, axis=-1, keepdims=True), jnp.max(sn, axis=-1, keepdims=True))
        e = jnp.exp2(s - mx)
        en = jnp.exp2(sn - mx)
        denom = jnp.sum(e, axis=-1, keepdims=True) + jnp.sum(en, axis=-1, keepdims=True)
        row = lax.broadcasted_iota(jnp.int32, denom.shape, 0)
        w = jnp.where(row < nq, 1.0, -lam) / denom

        def combine(x):
            p = x * w
            return (p[:nq] + p[nq:]).astype(BF16)

        a, an = combine(e), combine(en)
        yield
        o = (jnp.dot(a, vt_ref[h], preferred_element_type=F32)
             + jnp.dot(an, vn_ref[h], preferred_element_type=F32))
        o_ref[h] = _head_out(o, hn_ref[...], za_ref[h].astype(F32), lam_init).astype(o_ref.dtype)
        yield

    def together(*gens):
        gens = list(gens)
        while gens:
            gens = [g for g in gens if next(g, StopIteration) is not StopIteration]

    def first_step(g):
        next(g, None)
        yield

    per_chunk = [scores(c) for c in range(nchunk)]
    next(per_chunk[0])
    for c in range(nchunk):
        ahead = [first_step(per_chunk[c + 1])] if c + 1 < nchunk else []
        together(per_chunk[c], values(c), *ahead)
    for h in range(0, N_HEADS, 2):
        together(head(h), head(h + 1))


def _sample_attention(lam, q, cache_k, cache_v, kb, vb, sbc, sbn, za, head_norm, lam_init, layer,
                      row0, b):
    nq = sbn.shape[2]
    rows = cache_k.shape[1]
    stacked = pl.BlockSpec((N_HEADS, nq, D_V), lambda bi: (0, row0 // nq + bi, 0))
    own = pl.BlockSpec((N_HEADS, nq, D_V), lambda bi: (0, bi, 0))
    cache = pl.BlockSpec((1, rows, D_V), lambda bi: (layer * b + bi, 0, 0))
    whole = lambda a: pl.BlockSpec(a.shape, lambda bi: (0,) * a.ndim)
    return pl.pallas_call(
        functools.partial(_sattn_kernel, lam_init=lam_init),
        out_shape=jax.ShapeDtypeStruct((N_HEADS, b * nq, D_V), BF16),
        grid=(b,),
        in_specs=[pl.BlockSpec(memory_space=pltpu.SMEM),
                  stacked, cache, cache, own, own, whole(sbc), whole(sbn), stacked, whole(head_norm)],
        out_specs=own,
        scratch_shapes=[pltpu.VMEM((N_HEADS, 2 * nq, rows // N_HEADS + V7X_LANES), F32),
                        pltpu.VMEM((N_HEADS, rows // N_HEADS, D_V), BF16)],
        compiler_params=_params(("parallel",)),
        name="sample_attention",
    )(lam, q, cache_k, cache_v, kb, vb, sbc, sbn, za, head_norm)


def _out_kernel(og_ref, t_ref, ga_ref, gc_ref, wpa_ref, wpc_ref, wo_ref, x_ref, g_ref,
                o_ref, m_ref):
    c = pl.program_id(1)
    nchunk, _, tc = m_ref.shape
    og = jnp.concatenate([og_ref[h] for h in range(og_ref.shape[0])], axis=1)
    ya = jnp.dot(og, wpa_ref[c], preferred_element_type=F32)
    yc = jnp.dot(t_ref[...], wpc_ref[c], preferred_element_type=F32)
    merged = (jax.nn.sigmoid(ga_ref[...].astype(F32)) * ya
              + jax.nn.sigmoid(gc_ref[...].astype(F32)) * yc)
    m_ref[c] = merged.astype(BF16)

    @pl.when(c == nchunk - 1)
    def _():
        y = jnp.dot(m_ref[0], wo_ref[0], preferred_element_type=F32)
        for cc in range(1, nchunk):
            y = y + jnp.dot(m_ref[cc], wo_ref[cc], preferred_element_type=F32)
        y = y * lax.rsqrt(jnp.mean(y * y, axis=-1, keepdims=True) + EPS)
        o_ref[...] = x_ref[...] + y * g_ref[...]


def _cast_chunks_kernel(w_ref, o_ref):
    o_ref[0] = w_ref[...].astype(o_ref.dtype)


def _cast_chunks(w, tc, rows=False):
    k, n = w.shape
    blk, imap = ((tc, n), lambda j: (j, 0)) if rows else ((k, tc), lambda j: (0, j))
    nchunk = (k if rows else n) // tc
    return pl.pallas_call(
        _cast_chunks_kernel,
        out_shape=jax.ShapeDtypeStruct((nchunk,) + blk, BF16),
        grid=(nchunk,),
        in_specs=[pl.BlockSpec(blk, imap)],
        out_specs=pl.BlockSpec((1,) + blk, lambda j: (j, 0, 0)),
        compiler_params=_params(("parallel",)),
        name="cast_chunks",
    )(w)


def _merge_out(og, t, gates, row0, wpa, wpc, wo, x, g, tm, tc):
    m, d = x.shape
    assert row0 % tm == 0
    first = row0 // tm
    resident = lambda a: pl.BlockSpec(a.shape, lambda i, c: (0, 0, 0), pipeline_mode=pl.Buffered(1))
    return pl.pallas_call(
        _out_kernel,
        out_shape=jax.ShapeDtypeStruct((m, d), x.dtype),
        grid=(m // tm, d // tc),
        in_specs=[pl.BlockSpec((N_HEADS, tm, D_V), lambda i, c: (0, i, 0)),
                  pl.BlockSpec((tm, CONV_W), lambda i, c: (i, 0)),
                  pl.BlockSpec((tm, tc), lambda i, c: (first + i, c)),
                  pl.BlockSpec((tm, tc), lambda i, c: (first + i, d // tc + c)),
                  resident(wpa), resident(wpc), resident(wo),
                  pl.BlockSpec((tm, d), lambda i, c: (i, 0)),
                  pl.BlockSpec((1, d), lambda i, c: (0, 0))],
        out_specs=pl.BlockSpec((tm, d), lambda i, c: (i, 0)),
        scratch_shapes=[pltpu.VMEM((d // tc, tm, tc), BF16)],
        compiler_params=_params(("parallel", "arbitrary")),
        name="merge_out",
    )(og, t, gates, gates, wpa, wpc, wo, x, g)


def _proj_specs():
    scale = LOG2E / math.sqrt(D_QK)
    heads = (BF16, "heads")
    return (("q", OFF_Q, ATT_W, PROJ_COLS, lambda r: (r * scale,), (heads,)),
            ("k", OFF_K, ATT_W, PROJ_COLS, lambda r: (r, r), (F32, heads)),
            ("v", OFF_V, ATT_W, PROJ_COLS, lambda r: (r, r), (F32, heads)),
            ("za", OFF_ZA, ATT_W, PROJ_COLS, lambda r: (r,), (heads,)),
            ("gates", OFF_GA, 2 * D_MODEL, PROJ_COLS, lambda r: (r,), (BF16,)))


def _layer(xp, xs, conv_p, conv_s, weights, attend_p, attend_s):
    (norm_pre, norm_post, w_in, conv_w, wpa, wpc, wo) = weights
    (bp, tp, d), (bs, ts, _) = xp.shape, xs.shape
    mp, ms = bp * tp, bs * ts
    xp2, xs2 = xp.reshape(mp, d), xs.reshape(ms, d)
    tm = (mp + ms) // 8
    assert tm * 8 == mp + ms and tm % 16 == 0
    res, xn = {}, None
    for name, off, n, tn, epilogue, dtypes in _proj_specs():
        if xn is None:
            xn, *res[name] = _proj((xp2, xs2), w_in, off, n, epilogue, dtypes, PROJ_ROWS, tn,
                                   "proj_" + name, gain=norm_pre)
        elif F32 in dtypes:
            k_p, k_s, kb_p, kb_s = _proj(xn, w_in, off, n, epilogue, dtypes, PROJ_ROWS, tn,
                                         "proj_" + name, split=mp)
            res[name] = [(k_p, kb_p), (k_s, kb_s)]
        else:
            res[name] = _proj(xn, w_in, off, n, epilogue, dtypes, tm, tn, "proj_" + name)
    (q,), (za,), (gates,) = (res[n] for n in ("q", "za", "gates"))
    out = []
    for g, (row0, x2, b, t, conv_state, attend, conv_tiles) in enumerate((
            (0, xp2, bp, tp, conv_p, attend_p, (1, 1024, 512)),
            (mp, xs2, bs, ts, conv_s, attend_s, (bs, ts, 512)))):
        m = b * t
        (k, kb), (v, vb) = res["k"][g], res["v"][g]
        tcv, new_conv = _conv_branch(xn, row0, b, t, w_in, conv_state, conv_w, *conv_tiles)
        og = attend(q, kb, vb, za)
        y = _merge_out(og, tcv.reshape(m, CONV_W), gates, row0, wpa, wpc, wo, x2,
                       norm_post.reshape(1, d), PROJ_ROWS, MERGE_CHUNK)
        out.append((y.reshape(b, t, d), k.reshape(b, t, N_HEADS, 2 * D_QK),
                    v.reshape(b, t, N_HEADS, D_V), new_conv))
    return out


def kernel(x_prompt, x_sample, cache_k, cache_v, state_conv, norm_pre, norm_post, w_in,
           lambda_q1, lambda_k1, lambda_q2, lambda_k2, head_norm, conv_w, w_proj_attn,
           w_proj_conv, w_out, rel_bias):
    depth = w_in.shape[0]
    past = cache_k.shape[2]
    dec_b, dec_t = x_sample.shape[0], x_sample.shape[1]
    xp, xs = x_prompt, x_sample
    zero_conv = jnp.zeros((xp.shape[0], CONV_K - 1, CONV_W), xp.dtype)
    pbias = _prompt_bias(rel_bias)
    outs = [[] for _ in range(6)]
    for l in range(depth):
        lam_init = 0.8 - 0.6 * math.exp(-0.3 * l)
        sbc, sbn, lam = _sample_bias_and_lambda(
            rel_bias, lambda_q1[l][None], lambda_k1[l][None], lambda_q2[l][None], lambda_k2[l][None],
            past, dec_t, lam_init)
        hn = head_norm[l].reshape(1, D_V)
        weights = (norm_pre[l], norm_post[l], w_in[l], conv_w[l],
                   _cast_chunks(w_proj_attn[l], MERGE_CHUNK), _cast_chunks(w_proj_conv[l], MERGE_CHUNK),
                   _cast_chunks(w_out[l], MERGE_CHUNK, rows=True))

        pb, pt = x_prompt.shape[0], x_prompt.shape[1]

        def attend_prompt(q, kb, vb, za):
            return _prompt_attention(lam, q, kb, vb, pbias, za, hn, lam_init, pb, pt)

        ck = cache_k.reshape(depth * dec_b, past * N_HEADS, D_V)
        cv = cache_v.reshape(depth * dec_b, past * N_HEADS, D_V)

        def attend_sample(q, kb, vb, za, l=l):
            return _sample_attention(lam, q, ck, cv, kb, vb, sbc, sbn, za, hn, lam_init, l,
                                     pb * pt, dec_b)

        (xp, kp, vp, cp), (xs, ksm, vsm, csm) = _layer(xp, xs, zero_conv, state_conv[l], weights,
                                                       attend_prompt, attend_sample)
        for lst, val in zip(outs, (kp, vp, cp, ksm, vsm, csm)):
            lst.append(val)
    return (xp, xs) + tuple(jnp.stack(o, axis=0) for o in outs)
```
